```python
import jax, jax.numpy as jnp
from jax import lax
import numpy as np

D_MODEL = 1024
BATCH = 2
SEQ = 8192
DEPTH = 1

GRID_W = 64
ML_HEADS = 4
ML_DH = 128
ML_WIDTH = ML_HEADS * ML_DH
ML_CHUNK = 64
CONV_W = 5
NA_HEADS = 8
NA_DH = 64
NA_WIDTH = NA_HEADS * NA_DH
NA_KH_MAX = 8
NA_KW = 16
D_MIX = ML_WIDTH + NA_WIDTH
N_GATES = 4 * ML_HEADS
D_IN = 3 * ML_WIDTH + N_GATES + 3 * NA_WIDTH
SPLITS = (ML_WIDTH, 2 * ML_WIDTH, 3 * ML_WIDTH, 3 * ML_WIDTH + N_GATES,
          3 * ML_WIDTH + N_GATES + NA_WIDTH, 3 * ML_WIDTH + N_GATES + 2 * NA_WIDTH)
D_FF = 2816
EPS = 1e-6

kernel_name = "hybrid_mlstm_natten_macaron_encoder"


def rmsnorm(x, g):
    xf = x.astype(jnp.float32)
    y = xf * lax.rsqrt(jnp.mean(xf * xf, axis=-1, keepdims=True) + EPS)
    return (y * g.astype(jnp.float32)).astype(x.dtype)


def swiglu(x, w_gate, w_up, w_down):
    return (jax.nn.silu(x @ w_gate) * (x @ w_up)) @ w_down


def dwconv_centred(u, w, b):
    C = u.shape[-1]
    y = lax.conv_general_dilated(
        u, w[:, None, :].astype(u.dtype), window_strides=(1,),
        padding=[(CONV_W // 2, CONV_W // 2)],
        dimension_numbers=("NWC", "WIO", "NWC"), feature_group_count=C)
    return y + b


def mlstm_chunkwise(q, k, v, log_i, log_f):
    B, H, T, d = q.shape
    L = ML_CHUNK
    NC = T // L
    q = q.astype(jnp.float32).reshape(B, H, NC, L, d)
    k = k.astype(jnp.float32).reshape(B, H, NC, L, d)
    v = v.astype(jnp.float32).reshape(B, H, NC, L, d)
    li = log_i.reshape(B, H, NC, L)
    lf = log_f.reshape(B, H, NC, L)
    b = jnp.cumsum(lf, axis=-1)
    g = b[..., -1]
    a = g[..., None] - b + li
    m_loc = jnp.max(a, axis=-1)
    w = jnp.exp(a - m_loc[..., None])
    C_loc = jnp.einsum('bhnl,bhnld,bhnle->bhnde', w, v, k)
    n_loc = jnp.einsum('bhnl,bhnle->bhne', w, k)

    def step(carry, xs):
        C, n, m = carry
        g_k, ml_k, Cl_k, nl_k = xs
        m_new = jnp.maximum(g_k + m, ml_k)
        s_old = jnp.exp(g_k + m - m_new)
        s_loc = jnp.exp(ml_k - m_new)
        C_new = s_old[..., None, None] * C + s_loc[..., None, None] * Cl_k
        n_new = s_old[..., None] * n + s_loc[..., None] * nl_k
        return (C_new, n_new, m_new), (C, n, m)

    init = (jnp.zeros((B, H, d, d), jnp.float32), jnp.zeros((B, H, d), jnp.float32),
            jnp.zeros((B, H), jnp.float32))
    xs = (jnp.moveaxis(g, 2, 0), jnp.moveaxis(m_loc, 2, 0),
          jnp.moveaxis(C_loc, 2, 0), jnp.moveaxis(n_loc, 2, 0))
    _, (C0, n0, m0) = lax.scan(step, init, xs)
    C0 = jnp.moveaxis(C0, 0, 2)
    n0 = jnp.moveaxis(n0, 0, 2)
    m0 = jnp.moveaxis(m0, 0, 2)

    D = b[..., :, None] - b[..., None, :] + li[..., None, :]
    mask = np.tril(np.ones((L, L), dtype=bool))
    D = jnp.where(mask, D, -jnp.inf)
    inter = b + m0[..., None]
    m_t = jnp.maximum(inter, jnp.max(D, axis=-1))
    s = jnp.einsum('bhnld,bhnsd->bhnls', q, k) * jnp.exp(D - m_t[..., None])
    sc_inter = jnp.exp(inter - m_t)
    num = jnp.einsum('bhnls,bhnsd->bhnld', s, v) + \
        sc_inter[..., None] * jnp.einsum('bhnde,bhnle->bhnld', C0, q)
    den = jnp.sum(s, axis=-1) + sc_inter * jnp.einsum('bhne,bhnle->bhnl', n0, q)
    den = jnp.maximum(jnp.abs(den), jnp.exp(-m_t))
    h = num / den[..., None]
    return h.reshape(B, H, T, d)


def mlstm_mixer(c, v, o, gates, conv_w, conv_b, w_q, w_k, gn):
    B, T, _ = c.shape
    u = jax.nn.silu(dwconv_centred(c, conv_w, conv_b)).reshape(B, T, ML_HEADS, ML_DH)
    q = jnp.einsum('bthd,hde->bhte', u, w_q)
    k = jnp.einsum('bthd,hde->bhte', u, w_k) * (ML_DH ** -0.5)
    vh = v.reshape(B, T, ML_HEADS, ML_DH).transpose(0, 2, 1, 3)
    gt = gates.astype(jnp.float32).reshape(B, T, 2, 2, ML_HEADS).transpose(2, 3, 0, 4, 1)
    log_i = gt[:, 0]
    log_f = jax.nn.log_sigmoid(gt[:, 1])
    flip = lambda a: jnp.flip(a, axis=2)
    h_fwd = mlstm_chunkwise(q, k, vh, log_i[0], log_f[0])
    h_bwd = flip(mlstm_chunkwise(flip(q), flip(k), flip(vh), flip(log_i[1]), flip(log_f[1])))
    hs = (h_fwd + h_bwd).transpose(0, 2, 1, 3)
    hn = rmsnorm(hs, gn).reshape(B, T, ML_WIDTH)
    return (jax.nn.sigmoid(o.astype(jnp.float32)) * hn).astype(c.dtype)


def na_mixer(q, k, v, gq, gk, rpb):
    B, T, _ = q.shape
    rows = T // GRID_W
    kh = min(NA_KH_MAX, rows)
    qh = rmsnorm(q.reshape(B, T, NA_HEADS, NA_DH), gq) * (NA_DH ** -0.5)
    khd = rmsnorm(k.reshape(B, T, NA_HEADS, NA_DH), gk)
    to_grid = lambda a: a.reshape(B, rows, GRID_W, NA_HEADS, NA_DH).transpose(0, 3, 1, 2, 4)
    qg, kg = to_grid(qh), to_grid(khd)
    vg = to_grid(v.reshape(B, T, NA_HEADS, NA_DH))
    col_start = np.clip(np.arange(GRID_W) - NA_KW // 2, 0, GRID_W - NA_KW)
    col_idx = col_start[:, None] + np.arange(NA_KW)[None, :]
    dc_idx = col_idx - np.arange(GRID_W)[:, None] + (NA_KW - 1)

    def row_block(r):
        rs = jnp.clip(r - kh // 2, 0, rows - kh)
        k_rows = lax.dynamic_slice_in_dim(kg, rs, kh, axis=2)
        v_rows = lax.dynamic_slice_in_dim(vg, rs, kh, axis=2)
        k_win = k_rows[:, :, :, col_idx, :]
        v_win = v_rows[:, :, :, col_idx, :]
        q_r = lax.dynamic_index_in_dim(qg, r, axis=2, keepdims=False)
        s = jnp.einsum('bhcd,bhrcwd->bhcrw', q_r, k_win).astype(jnp.float32)
        dr = rs + jnp.arange(kh) - r + (NA_KH_MAX - 1)
        bias = rpb[:, dr][:, :, dc_idx].transpose(0, 2, 1, 3)
        s = s + bias[None].astype(jnp.float32)
        p = jax.nn.softmax(s.reshape(B, NA_HEADS, GRID_W, kh * NA_KW), axis=-1)
        p = p.reshape(B, NA_HEADS, GRID_W, kh, NA_KW).astype(v_win.dtype)
        return jnp.einsum('bhcrw,bhrcwd->bhcd', p, v_win)

    out = lax.map(row_block, jnp.arange(rows))
    return out.transpose(1, 0, 3, 2, 4).reshape(B, T, NA_WIDTH).astype(q.dtype)


def setup_inputs(seed: int = 0) -> dict:
    key = jax.random.key(seed)
    ks = jax.random.split(key, 24)
    nrm = lambda k, shape, s: jax.random.normal(k, shape, jnp.float32) * s
    gain = lambda k, shape: 1.0 + 0.02 * jax.random.normal(k, shape, jnp.float32)
    ig = 0.1 * jax.random.normal(ks[7], (DEPTH, 2, 1, ML_HEADS), jnp.float32)
    fg = jnp.linspace(3.0, 6.0, ML_HEADS, dtype=jnp.float32) + \
        0.1 * jax.random.normal(ks[8], (DEPTH, 2, 1, ML_HEADS), jnp.float32)
    b_gates = jnp.concatenate([ig, fg], axis=2).reshape(DEPTH, N_GATES)
    return {
        "x": nrm(ks[0], (BATCH, SEQ, D_MODEL), 1.0),
        "norm_ffn1": gain(ks[1], (DEPTH, D_MODEL)),
        "w1_gate": nrm(ks[2], (DEPTH, D_MODEL, D_FF), D_MODEL ** -0.5),
        "w1_up": nrm(ks[3], (DEPTH, D_MODEL, D_FF), D_MODEL ** -0.5),
        "w1_down": nrm(ks[4], (DEPTH, D_FF, D_MODEL), D_FF ** -0.5),
        "norm_mix": gain(ks[5], (DEPTH, D_MODEL)),
        "w_in": nrm(ks[6], (DEPTH, D_MODEL, D_IN), D_MODEL ** -0.5),
        "b_gates": b_gates,
        "conv_w": nrm(ks[9], (DEPTH, CONV_W, ML_WIDTH), CONV_W ** -0.5),
        "conv_b": nrm(ks[10], (DEPTH, ML_WIDTH), 0.02),
        "w_q_ml": nrm(ks[11], (DEPTH, ML_HEADS, ML_DH, ML_DH), ML_DH ** -0.5),
        "w_k_ml": nrm(ks[12], (DEPTH, ML_HEADS, ML_DH, ML_DH), ML_DH ** -0.5),
        "gn_ml": gain(ks[13], (DEPTH, ML_HEADS, ML_DH)),
        "gq_na": gain(ks[14], (DEPTH, NA_DH)),
        "gk_na": gain(ks[15], (DEPTH, NA_DH)),
        "rpb": nrm(ks[16], (DEPTH, NA_HEADS, 2 * NA_KH_MAX - 1, 2 * NA_KW - 1), 0.1),
        "w_out": nrm(ks[17], (DEPTH, D_MIX, D_MODEL), D_MIX ** -0.5),
        "norm_ffn2": gain(ks[18], (DEPTH, D_MODEL)),
        "w2_gate": nrm(ks[19], (DEPTH, D_MODEL, D_FF), D_MODEL ** -0.5),
        "w2_up": nrm(ks[20], (DEPTH, D_MODEL, D_FF), D_MODEL ** -0.5),
        "w2_down": nrm(ks[21], (DEPTH, D_FF, D_MODEL), D_FF ** -0.5),
        "norm_final": gain(ks[22], (DEPTH, D_MODEL)),
    }


def reference(x, norm_ffn1, w1_gate, w1_up, w1_down, norm_mix, w_in, b_gates, conv_w,
              conv_b, w_q_ml, w_k_ml, gn_ml, gq_na, gk_na, rpb, w_out, norm_ffn2,
              w2_gate, w2_up, w2_down, norm_final):
    for l in range(DEPTH):
        x = x + 0.5 * swiglu(rmsnorm(x, norm_ffn1[l]), w1_gate[l], w1_up[l], w1_down[l])
        h = rmsnorm(x, norm_mix[l])
        proj = h @ w_in[l]
        c_ml, v_ml, o_ml, g_ml, q_na, k_na, v_na = jnp.split(proj, SPLITS, axis=-1)
        y_ml = mlstm_mixer(c_ml, v_ml, o_ml, g_ml + b_gates[l], conv_w[l], conv_b[l],
                           w_q_ml[l], w_k_ml[l], gn_ml[l])
        y_na = na_mixer(q_na, k_na, v_na, gq_na[l], gk_na[l], rpb[l])
        x = x + jnp.concatenate([y_ml, y_na], axis=-1) @ w_out[l]
        x = x + 0.5 * swiglu(rmsnorm(x, norm_ffn2[l]), w2_gate[l], w2_up[l], w2_down[l])
        x = rmsnorm(x, norm_final[l])
    return x
```

```python
import functools

import jax
import jax.numpy as jnp
import numpy as np
from jax import lax
from jax.experimental import pallas as pl
from jax.experimental.pallas import tpu as pltpu

F32 = jnp.float32
BF16 = jnp.bfloat16

D_MODEL = 1024
D_FF = 2816
GRID_W = 64
ML_HEADS = 4
ML_DH = 128
ML_WIDTH = ML_HEADS * ML_DH
CONV_W = 5
NA_HEADS = 8
NA_DH = 64
NA_WIDTH = NA_HEADS * NA_DH
NA_KH = 8
NA_KW = 16
N_GATES = 4 * ML_HEADS
EPS = 1e-6
NEG = -1e30

SUBLANES = 8
LANES = 128

FFN_TM = 256
FFN_CK = 1408
ML_L = 128
ML_TT = 256
CONV_TT = 512
NA_R = 8
VMEM_LIMIT = 56 * 1024 * 1024


def _dot(a, b):
    return jnp.dot(a, b, preferred_element_type=F32)


def _dot_nt(a, b):
    return lax.dot_general(a, b, (((1,), (1,)), ((), ())), preferred_element_type=F32)


def _dot_tn(a, b):
    return lax.dot_general(a, b, (((0,), (0,)), ((), ())), preferred_element_type=F32)


def _dot_exact(a, b):
    return jnp.dot(a, b, preferred_element_type=F32, precision=lax.Precision.HIGHEST)


def _rms(x, g):
    ms = jnp.mean(x * x, axis=-1, keepdims=True)
    return x * lax.rsqrt(ms + EPS) * g


def _log_sigmoid(x):
    return jnp.minimum(x, 0.0) - jnp.log1p(jnp.exp(-jnp.abs(x)))


def _swiglu_half(h, wg_ref, wu_ref, wd_ref):
    acc = None
    for j in range(D_FF // FFN_CK):
        sl = slice(j * FFN_CK, (j + 1) * FFN_CK)
        g = _dot(h, wg_ref[:, sl])
        u = _dot(h, wu_ref[:, sl])
        a = (g * jax.nn.sigmoid(g) * u).astype(BF16)
        part = _dot(a, wd_ref[sl, :])
        acc = part if acc is None else acc + part
    return 0.5 * acc


def _group_mean_sq(x, ones_ref):
    xx = x * x
    hi = xx.astype(BF16)
    lo = (xx - hi.astype(F32)).astype(BF16)
    s = _dot(hi, ones_ref[...]) + _dot(lo, ones_ref[...])
    return s * (1.0 / NA_DH)


def _ffn1_inproj_kernel(x_ref, g1_ref, wg_ref, wu_ref, wd_ref, gm_ref, wmain_ref, wgc_ref, bgc_ref,
                        wgr_ref, bgr_ref, ones_ref, gq_ref, gk_ref,
                        x1_ref, c_ref, vml_ref, o_ref, gcol_ref, grow_ref, qn_ref, kn_ref, vn_ref):
    x = x_ref[...]
    h = _rms(x, g1_ref[...]).astype(BF16)
    x1 = x + _swiglu_half(h, wg_ref, wu_ref, wd_ref)
    x1_ref[...] = x1
    h2 = _rms(x1, gm_ref[...]).astype(BF16)
    p = _dot(h2, wmain_ref[...])
    W = ML_WIDTH
    c_ref[...] = p[:, 0:W]
    vml_ref[...] = p[:, W:2 * W].astype(BF16)
    o_ref[...] = p[:, 2 * W:3 * W]
    q = p[:, 3 * W:4 * W]
    k = p[:, 4 * W:5 * W]
    vn_ref[...] = p[:, 5 * W:6 * W].astype(BF16)
    qn = q * lax.rsqrt(_group_mean_sq(q, ones_ref) + EPS) * gq_ref[...] * (NA_DH ** -0.5)
    kn = k * lax.rsqrt(_group_mean_sq(k, ones_ref) + EPS) * gk_ref[...]
    qn_ref[...] = qn.astype(BF16)
    kn_ref[...] = kn.astype(BF16)
    gc = _dot(h2, wgc_ref[...]) + bgc_ref[...]
    lane = lax.broadcasted_iota(jnp.int32, gc.shape, 1) % LANES
    gcol_ref[...] = jnp.where((lane >= ML_HEADS) & (lane < 2 * ML_HEADS), _log_sigmoid(gc), gc)
    gr = _dot_nt(wgr_ref[...], h2) + bgr_ref[...]
    row = lax.broadcasted_iota(jnp.int32, gr.shape, 0) % (2 * ML_HEADS)
    gr = jnp.where(row >= ML_HEADS, _log_sigmoid(gr), gr)
    for cidx in range(FFN_TM // ML_L):
        grow_ref[cidx] = gr[:, cidx * ML_L:(cidx + 1) * ML_L]


def _const_spec(shape):
    nd = len(shape)
    return pl.BlockSpec(shape, lambda *_: (0,) * nd, pipeline_mode=pl.Buffered(1))


def _ffn1_inproj(x2d, g1, wg, wu, wd, gm, wmain, wgc, bgc, wgr, bgr, ones, gq, gk):
    n = x2d.shape[0]
    tm = FFN_TM
    W = ML_WIDTH
    tok = lambda w: pl.BlockSpec((tm, w), lambda i: (i, 0))
    out_shape = (
        jax.ShapeDtypeStruct((n, D_MODEL), F32),
        jax.ShapeDtypeStruct((n, W), F32),
        jax.ShapeDtypeStruct((n, W), BF16),
        jax.ShapeDtypeStruct((n, W), F32),
        jax.ShapeDtypeStruct((n, 2 * LANES), F32),
        jax.ShapeDtypeStruct((n // ML_L, N_GATES, ML_L), F32),
        jax.ShapeDtypeStruct((n, W), BF16),
        jax.ShapeDtypeStruct((n, W), BF16),
        jax.ShapeDtypeStruct((n, W), BF16),
    )
    out_specs = (
        tok(D_MODEL), tok(W), tok(W), tok(W), tok(2 * LANES),
        pl.BlockSpec((tm // ML_L, N_GATES, ML_L), lambda i: (i, 0, 0)),
        tok(W), tok(W), tok(W),
    )
    in_specs = [tok(D_MODEL)] + [_const_spec(a.shape) for a in
                                 (g1, wg, wu, wd, gm, wmain, wgc, bgc, wgr, bgr, ones, gq, gk)]
    return pl.pallas_call(
        _ffn1_inproj_kernel,
        grid=(n // tm,),
        in_specs=in_specs,
        out_specs=out_specs,
        out_shape=out_shape,
        compiler_params=pltpu.CompilerParams(
            dimension_semantics=("parallel",), vmem_limit_bytes=VMEM_LIMIT),
        name="ffn1_inproj",
    )(x2d, g1, wg, wu, wd, gm, wmain, wgc, bgc, wgr, bgr, ones, gq, gk)


def _conv_qk_kernel(c_ref, prev_ref, next_ref, cw_ref, cb_ref, wq_ref, wk_ref, q_ref, k_ref, pad_ref):
    i = pl.program_id(1)
    nb = pl.num_programs(1)
    tt = CONV_TT
    half = CONV_W // 2
    pad_ref[0:SUBLANES, :] = jnp.where(i > 0, prev_ref[...], 0.0)
    pad_ref[SUBLANES:SUBLANES + tt, :] = c_ref[...]
    pad_ref[SUBLANES + tt:2 * SUBLANES + tt, :] = jnp.where(i < nb - 1, next_ref[...], 0.0)
    y = cb_ref[...]
    for kk in range(CONV_W):
        start = SUBLANES + kk - half
        y = y + pad_ref[start:start + tt, :] * cw_ref[kk:kk + 1, :]
    u = y * jax.nn.sigmoid(y)
    for h in range(ML_HEADS):
        sl = slice(h * ML_DH, (h + 1) * ML_DH)
        uh = u[:, sl].astype(BF16)
        q_ref[:, sl] = _dot(uh, wq_ref[h]).astype(BF16)
        k_ref[:, sl] = (_dot(uh, wk_ref[h]) * (ML_DH ** -0.5)).astype(BF16)


def _conv_qk(c3, cw, cb, wq, wk):
    b, t, w = c3.shape
    tt = CONV_TT
    nb = t // tt
    per = tt // SUBLANES
    last = t // SUBLANES - 1
    tile = pl.BlockSpec((None, tt, w), lambda bi, i: (bi, i, 0))
    prev = pl.BlockSpec((None, SUBLANES, w), lambda bi, i: (bi, jnp.maximum(i * per - 1, 0), 0))
    nxt = pl.BlockSpec((None, SUBLANES, w), lambda bi, i: (bi, jnp.minimum((i + 1) * per, last), 0))
    return pl.pallas_call(
        _conv_qk_kernel,
        grid=(b, nb),
        in_specs=[tile, prev, nxt] + [
            pl.BlockSpec(a.shape, functools.partial(lambda nd, bi, i: (0,) * nd, a.ndim))
            for a in (cw, cb, wq, wk)],
        out_specs=(tile, tile),
        out_shape=(jax.ShapeDtypeStruct((b, t, w), BF16), jax.ShapeDtypeStruct((b, t, w), BF16)),
        scratch_shapes=[pltpu.VMEM((tt + 2 * SUBLANES, w), F32)],
        compiler_params=pltpu.CompilerParams(dimension_semantics=("parallel", "parallel")),
        name="conv_qk",
    )(c3, c3, c3, cw, cb, wq, wk)


def _mlstm_kernel(q_ref, k_ref, v_ref, o_ref, gc_ref, gr_ref, gn_ref, y_ref,
                  hf_ref, c_state, n_state, m_state, *, nblk):
    d = pl.program_id(1)
    i = pl.program_id(2)
    blk = i + d * (nblk - 1 - 2 * i)
    L = ML_L
    nch = ML_TT // L
    H = ML_HEADS

    @pl.when(i == 0)
    def _():
        c_state[...] = jnp.zeros_like(c_state)
        n_state[...] = jnp.zeros_like(n_state)
        m_state[...] = jnp.zeros_like(m_state)

    t_idx = lax.broadcasted_iota(jnp.int32, (L, L), 0)
    s_idx = lax.broadcasted_iota(jnp.int32, (L, L), 1)
    fwd = d == 0
    sgn = 1 - 2 * d
    mask = (t_idx - s_idx) * sgn >= 0
    mask_t = (s_idx - t_idx) * sgn >= 0
    tri = mask.astype(F32)
    tri_t = mask_t.astype(F32)

    for j in range(nch):
        cj = j + d * (nch - 1 - 2 * j)
        off = pl.multiple_of(cj * L, L)
        G = gc_ref[pl.ds(off, L), :]
        GT = gr_ref[cj]
        Bc = _dot_exact(tri, G)
        BT = _dot_exact(GT, tri_t)
        gtot = jnp.sum(GT, axis=1, keepdims=True)
        hs = []
        for h in range(H):
            sl = slice(h * ML_DH, (h + 1) * ML_DH)
            q = q_ref[pl.ds(off, L), sl]
            k = k_ref[pl.ds(off, L), sl]
            v = v_ref[pl.ds(off, L), sl]
            li_col = G[:, h:h + 1]
            b_col = Bc[:, H + h:H + h + 1]
            li_row = GT[h:h + 1, :]
            b_row = BT[H + h:H + h + 1, :]
            g = gtot[H + h:H + h + 1, :]
            c0 = c_state[h]
            n0 = n_state[h]
            m0 = m_state[h][:, 0:1]

            D = jnp.where(mask, b_col - b_row + li_row, NEG)
            inter = b_col + m0
            m_t = jnp.maximum(inter, jnp.max(D, axis=1, keepdims=True))
            s = _dot_nt(q, k) * jnp.exp(D - m_t)
            sc = jnp.exp(inter - m_t)
            num = _dot(s.astype(BF16), v) + sc * _dot_nt(q, c0.astype(BF16))
            den = jnp.sum(s, axis=1, keepdims=True) + \
                sc * jnp.sum(q.astype(F32) * n0, axis=1, keepdims=True)
            den = jnp.maximum(jnp.abs(den), jnp.exp(-m_t))
            hs.append(num / den)

            a_col = g - b_col + li_col
            a_row = g - b_row + li_row
            m_loc = jnp.max(a_row, axis=1, keepdims=True)
            w_col = jnp.exp(a_col - m_loc)
            c_loc = _dot_tn((w_col * v.astype(F32)).astype(BF16), k)
            n_loc = jnp.sum(w_col * k.astype(F32), axis=0, keepdims=True)
            m_new = jnp.maximum(g + m0, m_loc)
            s_old = jnp.exp(g + m0 - m_new)
            s_loc = jnp.exp(m_loc - m_new)
            c_state[h] = s_old * c0 + s_loc * c_loc
            n_state[h] = s_old * n0 + s_loc * n_loc
            m_state[h] = jnp.broadcast_to(m_new, (1, LANES))

        hrow = pl.multiple_of(blk * ML_TT + cj * L, L)

        @pl.when(fwd)
        def _():
            for h in range(H):
                hf_ref[pl.ds(hrow, L), h * ML_DH:(h + 1) * ML_DH] = hs[h]

        @pl.when(jnp.logical_not(fwd))
        def _():
            for h in range(H):
                sl = slice(h * ML_DH, (h + 1) * ML_DH)
                tot = hf_ref[pl.ds(hrow, L), sl] + hs[h]
                hn = _rms(tot, gn_ref[h])
                y_ref[pl.ds(off, L), sl] = (jax.nn.sigmoid(o_ref[pl.ds(off, L), sl]) * hn).astype(BF16)


def _mlstm(q3, k3, v3, o3, gcol3, grow, gn):
    b, t, w = q3.shape
    tt = ML_TT
    nblk = t // tt
    nch = tt // ML_L
    blk_of = lambda d, i: i + d * (nblk - 1 - 2 * i)
    tile = pl.BlockSpec((None, tt, w), lambda bi, d, i: (bi, blk_of(d, i), 0))
    gc_spec = pl.BlockSpec((None, tt, LANES), lambda bi, d, i: (bi, blk_of(d, i), d))
    gr_spec = pl.BlockSpec((nch, 2 * ML_HEADS, ML_L), lambda bi, d, i: (bi * nblk + blk_of(d, i), d, 0))
    gn_spec = pl.BlockSpec(gn.shape, lambda bi, d, i: (0, 0, 0))
    y_spec = pl.BlockSpec((None, tt, w), lambda bi, d, i: (bi, d * blk_of(d, i) + (1 - d) * (nblk - 1), 0))
    return pl.pallas_call(
        functools.partial(_mlstm_kernel, nblk=nblk),
        grid=(b, 2, nblk),
        in_specs=[tile, tile, tile, tile, gc_spec, gr_spec, gn_spec],
        out_specs=y_spec,
        out_shape=jax.ShapeDtypeStruct((b, t, w), BF16),
        scratch_shapes=[
            pltpu.VMEM((t, w), F32),
            pltpu.VMEM((ML_HEADS, ML_DH, ML_DH), F32),
            pltpu.VMEM((ML_HEADS, 1, ML_DH), F32),
            pltpu.VMEM((ML_HEADS, 1, LANES), F32),
        ],
        compiler_params=pltpu.CompilerParams(
            dimension_semantics=("arbitrary", "arbitrary", "arbitrary"), vmem_limit_bytes=VMEM_LIMIT),
        name="mlstm",
    )(q3, k3, v3, o3, gcol3, grow, gn)


def _na_kernel(q_ref, k_ref, v_ref, bias_ref, out_ref, *, rows):
    r0 = pl.program_id(1) * NA_R
    nkeys = NA_KH * GRID_W

    def row_body(ri, carry):
        r = r0 + ri
        rs = jnp.clip(r - NA_KH // 2, 0, rows - NA_KH)
        delta = r - rs
        koff = pl.multiple_of(rs * GRID_W, GRID_W)
        qoff = pl.multiple_of(ri * GRID_W, GRID_W)
        qrow = q_ref[pl.ds(qoff, GRID_W), :]
        kslab = k_ref[pl.ds(koff, nkeys), :]
        vslab = v_ref[pl.ds(koff, nkeys), :]
        outs = []
        for h in range(NA_HEADS):
            sl = slice(h * NA_DH, (h + 1) * NA_DH)
            s = _dot_nt(qrow[:, sl], kslab[:, sl]) + bias_ref[delta, h]
            m = jnp.max(s, axis=-1, keepdims=True)
            p = jnp.exp(s - m)
            l = jnp.sum(p, axis=-1, keepdims=True)
            outs.append(_dot(p.astype(BF16), vslab[:, sl]) / l)
        out_ref[pl.ds(qoff, GRID_W), :] = jnp.concatenate(outs, axis=-1).astype(BF16)
        return carry

    lax.fori_loop(0, NA_R, row_body, 0)


def _na(q3, k3, v3, bias):
    b, t, w = q3.shape
    rows = t // GRID_W
    tq = NA_R * GRID_W
    qtile = pl.BlockSpec((None, tq, w), lambda bi, i: (bi, i, 0))
    seq = pl.BlockSpec((None, t, w), lambda bi, i: (bi, 0, 0), pipeline_mode=pl.Buffered(1))
    bias_spec = pl.BlockSpec(bias.shape, lambda bi, i: (0, 0, 0, 0), pipeline_mode=pl.Buffered(1))
    return pl.pallas_call(
        functools.partial(_na_kernel, rows=rows),
        grid=(b, rows // NA_R),
        in_specs=[qtile, seq, seq, bias_spec],
        out_specs=qtile,
        out_shape=jax.ShapeDtypeStruct((b, t, w), BF16),
        compiler_params=pltpu.CompilerParams(
            dimension_semantics=("parallel", "parallel"), vmem_limit_bytes=VMEM_LIMIT),
        name="natten",
    )(q3, k3, v3, bias)


def _na_bias_table(rpb):
    c = np.arange(GRID_W)
    cs = np.clip(c - NA_KW // 2, 0, GRID_W - NA_KW)
    cc = np.arange(GRID_W)
    valid = (cc[None, :] >= cs[:, None]) & (cc[None, :] < cs[:, None] + NA_KW)
    dc = np.clip(cc[None, :] - c[:, None] + NA_KW - 1, 0, 2 * NA_KW - 2)
    delta = np.arange(NA_KH)
    j = np.arange(NA_KH)
    dr = j[None, :] + (NA_KH - 1) - delta[:, None]
    tab = rpb[:, dr[:, :, None, None], dc[None, None, :, :]]
    tab = jnp.where(valid[None, None, None], tab.astype(F32), NEG)
    tab = tab.transpose(1, 0, 3, 2, 4)
    return tab.reshape(NA_KH, NA_HEADS, GRID_W, NA_KH * GRID_W)


def _outproj_ffn2_kernel(x1_ref, yml_ref, yna_ref, woml_ref, wona_ref, g2_ref, wg_ref, wu_ref, wd_ref,
                         gf_ref, out_ref):
    x2 = x1_ref[...] + _dot(yml_ref[...], woml_ref[...]) + _dot(yna_ref[...], wona_ref[...])
    h = _rms(x2, g2_ref[...]).astype(BF16)
    x3 = x2 + _swiglu_half(h, wg_ref, wu_ref, wd_ref)
    out_ref[...] = _rms(x3, gf_ref[...])


def _outproj_ffn2(x1, yml, yna, woml, wona, g2, wg, wu, wd, gf):
    n = x1.shape[0]
    tm = FFN_TM
    tok = lambda w: pl.BlockSpec((tm, w), lambda i: (i, 0))
    return pl.pallas_call(
        _outproj_ffn2_kernel,
        grid=(n // tm,),
        in_specs=[tok(D_MODEL), tok(ML_WIDTH), tok(NA_WIDTH)] + [
            _const_spec(a.shape) for a in (woml, wona, g2, wg, wu, wd, gf)],
        out_specs=tok(D_MODEL),
        out_shape=jax.ShapeDtypeStruct((n, D_MODEL), F32),
        compiler_params=pltpu.CompilerParams(
            dimension_semantics=("parallel",), vmem_limit_bytes=VMEM_LIMIT),
        name="outproj_ffn2",
    )(x1, yml, yna, woml, wona, g2, wg, wu, wd, gf)


def _layer(x, norm_ffn1, w1_gate, w1_up, w1_down, norm_mix, w_in, b_gates, conv_w, conv_b, w_q_ml,
           w_k_ml, gn_ml, gq_na, gk_na, rpb, w_out, norm_ffn2, w2_gate, w2_up, w2_down, norm_final):
    b, t, dm = x.shape
    n = b * t
    W = ML_WIDTH
    row = lambda a: a.reshape(1, -1).astype(F32)
    g0 = 3 * W
    wmain = jnp.concatenate([w_in[:, :g0], w_in[:, g0 + N_GATES:]], axis=1).astype(BF16)
    wgate = w_in[:, g0:g0 + N_GATES]
    half = N_GATES // 2
    zpad = jnp.zeros((dm, LANES - half), wgate.dtype)
    wgc = jnp.concatenate([wgate[:, :half], zpad, wgate[:, half:], zpad], axis=1).astype(BF16)
    bpad = jnp.zeros((LANES - half,), F32)
    bgc = jnp.concatenate([b_gates[:half], bpad, b_gates[half:], bpad]).reshape(1, -1)
    wgr = wgate.T.astype(BF16)
    bgr = b_gates.reshape(-1, 1).astype(F32)
    ones = jnp.asarray(np.kron(np.eye(NA_HEADS), np.ones((NA_DH, NA_DH))), BF16)
    gq = jnp.tile(gq_na.astype(F32), NA_HEADS).reshape(1, -1)
    gk = jnp.tile(gk_na.astype(F32), NA_HEADS).reshape(1, -1)

    x1, c, vml, o, gcol, grow, qn, kn, vn = _ffn1_inproj(
        x.reshape(n, dm), row(norm_ffn1), w1_gate.astype(BF16), w1_up.astype(BF16), w1_down.astype(BF16),
        row(norm_mix), wmain, wgc, bgc, wgr, bgr, ones, gq, gk)

    seq = lambda a: a.reshape(b, t, a.shape[-1])
    q_ml, k_ml = _conv_qk(seq(c), conv_w.astype(F32), row(conv_b), w_q_ml.astype(BF16), w_k_ml.astype(BF16))
    y_ml = _mlstm(q_ml, k_ml, seq(vml), seq(o), seq(gcol), grow, gn_ml.reshape(ML_HEADS, 1, ML_DH).astype(F32))
    y_na = _na(seq(qn), seq(kn), seq(vn), _na_bias_table(rpb))

    out = _outproj_ffn2(x1, y_ml.reshape(n, W), y_na.reshape(n, NA_WIDTH),
                        w_out[:W].astype(BF16), w_out[W:].astype(BF16), row(norm_ffn2),
                        w2_gate.astype(BF16), w2_up.astype(BF16), w2_down.astype(BF16), row(norm_final))
    return out.reshape(b, t, dm)


def kernel(x, norm_ffn1, w1_gate, w1_up, w1_down, norm_mix, w_in, b_gates, conv_w, conv_b, w_q_ml, w_k_ml,
           gn_ml, gq_na, gk_na, rpb, w_out, norm_ffn2, w2_gate, w2_up, w2_down, norm_final):
    depth = norm_ffn1.shape[0]
    for l in range(depth):
        x = _layer(x, norm_ffn1[l], w1_gate[l], w1_up[l], w1_down[l], norm_mix[l], w_in[l], b_gates[l],
                   conv_w[l], conv_b[l], w_q_ml[l], w_k_ml[l], gn_ml[l], gq_na[l], gk_na[l], rpb[l],
                   w_out[l], norm_ffn2[l], w2_gate[l], w2_up[l], w2_down[l], norm_final[l])
    return x
```

```python
import functools

import jax
import jax.numpy as jnp
import numpy as np
from jax import lax
from jax.experimental import pallas as pl
from jax.experimental.pallas import tpu as pltpu

F32 = jnp.float32
BF16 = jnp.bfloat16

D_MODEL = 1024
D_FF = 2816
GRID_W = 64
ML_HEADS = 4
ML_DH = 128
ML_WIDTH = ML_HEADS * ML_DH
CONV_W = 5
NA_HEADS = 8
NA_DH = 64
NA_WIDTH = NA_HEADS * NA_DH
NA_KH = 8
NA_KW = 16
N_GATES = 4 * ML_HEADS
EPS = 1e-6
NEG = -1e30

SUBLANES = 8
LANES = 128

FFN_TM = 256
FFN_CK = 1408
ML_L = 128
ML_TT = 256
CONV_TT = 512
NA_R = 8
VMEM_LIMIT = 56 * 1024 * 1024


def _dot(a, b):
    return jnp.dot(a, b, preferred_element_type=F32)


def _dot_nt(a, b):
    return lax.dot_general(a, b, (((1,), (1,)), ((), ())), preferred_element_type=F32)


def _dot_tn(a, b):
    return lax.dot_general(a, b, (((0,), (0,)), ((), ())), preferred_element_type=F32)


def _dot_exact(a, b):
    return jnp.dot(a, b, preferred_element_type=F32, precision=lax.Precision.HIGHEST)


def _rms(x, g):
    ms = jnp.mean(x * x, axis=-1, keepdims=True)
    return x * lax.rsqrt(ms + EPS) * g


def _log_sigmoid(x):
    return jnp.minimum(x, 0.0) - jnp.log1p(jnp.exp(-jnp.abs(x)))


def _swiglu_half(h, wg_ref, wu_ref, wd_ref):
    acc = None
    for j in range(D_FF // FFN_CK):
        sl = slice(j * FFN_CK, (j + 1) * FFN_CK)
        g = _dot(h, wg_ref[:, sl])
        u = _dot(h, wu_ref[:, sl])
        a = (g * jax.nn.sigmoid(g) * u).astype(BF16)
        part = _dot(a, wd_ref[sl, :])
        acc = part if acc is None else acc + part
    return 0.5 * acc


def _group_mean_sq(x, ones_ref):
    xx = x * x
    hi = xx.astype(BF16)
    lo = (xx - hi.astype(F32)).astype(BF16)
    s = _dot(hi, ones_ref[...]) + _dot(lo, ones_ref[...])
    return s * (1.0 / NA_DH)


def _ffn1_inproj_kernel(x_ref, g1_ref, wg_ref, wu_ref, wd_ref, gm_ref, wmain_ref, wgc_ref, bgc_ref,
                        wgr_ref, bgr_ref, ones_ref, gq_ref, gk_ref,
                        x1_ref, c_ref, vml_ref, o_ref, gcol_ref, grow_ref, qn_ref, kn_ref, vn_ref):
    x = x_ref[...]
    h = _rms(x, g1_ref[...]).astype(BF16)
    x1 = x + _swiglu_half(h, wg_ref, wu_ref, wd_ref)
    x1_ref[...] = x1
    h2 = _rms(x1, gm_ref[...]).astype(BF16)
    p = _dot(h2, wmain_ref[...])
    W = ML_WIDTH
    c_ref[...] = p[:, 0:W]
    vml_ref[...] = p[:, W:2 * W].astype(BF16)
    o_ref[...] = p[:, 2 * W:3 * W]
    q = p[:, 3 * W:4 * W]
    k = p[:, 4 * W:5 * W]
    vn_ref[...] = p[:, 5 * W:6 * W].astype(BF16)
    qn = q * lax.rsqrt(_group_mean_sq(q, ones_ref) + EPS) * gq_ref[...] * (NA_DH ** -0.5)
    kn = k * lax.rsqrt(_group_mean_sq(k, ones_ref) + EPS) * gk_ref[...]
    qn_ref[...] = qn.astype(BF16)
    kn_ref[...] = kn.astype(BF16)
    gc = _dot(h2, wgc_ref[...]) + bgc_ref[...]
    lane = lax.broadcasted_iota(jnp.int32, gc.shape, 1) % LANES
    gcol_ref[...] = jnp.where((lane >= ML_HEADS) & (lane < 2 * ML_HEADS), _log_sigmoid(gc), gc)
    gr = _dot_nt(wgr_ref[...], h2) + bgr_ref[...]
    row = lax.broadcasted_iota(jnp.int32, gr.shape, 0) % (2 * ML_HEADS)
    gr = jnp.where(row >= ML_HEADS, _log_sigmoid(gr), gr)
    for cidx in range(FFN_TM // ML_L):
        grow_ref[cidx] = gr[:, cidx * ML_L:(cidx + 1) * ML_L]


def _const_spec(shape):
    nd = len(shape)
    return pl.BlockSpec(shape, lambda *_: (0,) * nd, pipeline_mode=pl.Buffered(1))


def _ffn1_inproj(x2d, g1, wg, wu, wd, gm, wmain, wgc, bgc, wgr, bgr, ones, gq, gk):
    n = x2d.shape[0]
    tm = FFN_TM
    W = ML_WIDTH
    tok = lambda w: pl.BlockSpec((tm, w), lambda i: (i, 0))
    out_shape = (
        jax.ShapeDtypeStruct((n, D_MODEL), F32),
        jax.ShapeDtypeStruct((n, W), F32),
        jax.ShapeDtypeStruct((n, W), BF16),
        jax.ShapeDtypeStruct((n, W), F32),
        jax.ShapeDtypeStruct((n, 2 * LANES), F32),
        jax.ShapeDtypeStruct((n // ML_L, N_GATES, ML_L), F32),
        jax.ShapeDtypeStruct((n, W), BF16),
        jax.ShapeDtypeStruct((n, W), BF16),
        jax.ShapeDtypeStruct((n, W), BF16),
    )
    out_specs = (
        tok(D_MODEL), tok(W), tok(W), tok(W), tok(2 * LANES),
        pl.BlockSpec((tm // ML_L, N_GATES, ML_L), lambda i: (i, 0, 0)),
        tok(W), tok(W), tok(W),
    )
    in_specs = [tok(D_MODEL)] + [_const_spec(a.shape) for a in
                                 (g1, wg, wu, wd, gm, wmain, wgc, bgc, wgr, bgr, ones, gq, gk)]
    return pl.pallas_call(
        _ffn1_inproj_kernel,
        grid=(n // tm,),
        in_specs=in_specs,
        out_specs=out_specs,
        out_shape=out_shape,
        compiler_params=pltpu.CompilerParams(
            dimension_semantics=("parallel",), vmem_limit_bytes=VMEM_LIMIT),
        name="ffn1_inproj",
    )(x2d, g1, wg, wu, wd, gm, wmain, wgc, bgc, wgr, bgr, ones, gq, gk)


def _conv_qk_kernel(c_ref, prev_ref, next_ref, cw_ref, cb_ref, wq_ref, wk_ref, q_ref, k_ref, pad_ref):
    i = pl.program_id(1)
    nb = pl.num_programs(1)
    tt = CONV_TT
    half = CONV_W // 2
    pad_ref[0:SUBLANES, :] = jnp.where(i > 0, prev_ref[...], 0.0)
    pad_ref[SUBLANES:SUBLANES + tt, :] = c_ref[...]
    pad_ref[SUBLANES + tt:2 * SUBLANES + tt, :] = jnp.where(i < nb - 1, next_ref[...], 0.0)
    y = cb_ref[...]
    for kk in range(CONV_W):
        start = SUBLANES + kk - half
        y = y + pad_ref[start:start + tt, :] * cw_ref[kk:kk + 1, :]
    u = y * jax.nn.sigmoid(y)
    for h in range(ML_HEADS):
        sl = slice(h * ML_DH, (h + 1) * ML_DH)
        uh = u[:, sl].astype(BF16)
        q_ref[:, sl] = _dot(uh, wq_ref[h]).astype(BF16)
        k_ref[:, sl] = (_dot(uh, wk_ref[h]) * (ML_DH ** -0.5)).astype(BF16)


def _conv_qk(c3, cw, cb, wq, wk):
    b, t, w = c3.shape
    tt = CONV_TT
    nb = t // tt
    per = tt // SUBLANES
    last = t // SUBLANES - 1
    tile = pl.BlockSpec((None, tt, w), lambda bi, i: (bi, i, 0))
    prev = pl.BlockSpec((None, SUBLANES, w), lambda bi, i: (bi, jnp.maximum(i * per - 1, 0), 0))
    nxt = pl.BlockSpec((None, SUBLANES, w), lambda bi, i: (bi, jnp.minimum((i + 1) * per, last), 0))
    return pl.pallas_call(
        _conv_qk_kernel,
        grid=(b, nb),
        in_specs=[tile, prev, nxt] + [
            pl.BlockSpec(a.shape, functools.partial(lambda nd, bi, i: (0,) * nd, a.ndim))
            for a in (cw, cb, wq, wk)],
        out_specs=(tile, tile),
        out_shape=(jax.ShapeDtypeStruct((b, t, w), BF16), jax.ShapeDtypeStruct((b, t, w), BF16)),
        scratch_shapes=[pltpu.VMEM((tt + 2 * SUBLANES, w), F32)],
        compiler_params=pltpu.CompilerParams(dimension_semantics=("parallel", "parallel")),
        name="conv_qk",
    )(c3, c3, c3, cw, cb, wq, wk)


def _mlstm_kernel(q_ref, k_ref, v_ref, o_ref, gc_ref, gr_ref, gn_ref, y_ref,
                  hf_ref, c_state, n_state, m_state, *, nblk):
    d = pl.program_id(1)
    i = pl.program_id(2)
    blk = i + d * (nblk - 1 - 2 * i)
    L = ML_L
    nch = ML_TT // L
    H = ML_HEADS

    @pl.when(i == 0)
    def _():
        c_state[...] = jnp.zeros_like(c_state)
        n_state[...] = jnp.zeros_like(n_state)
        m_state[...] = jnp.zeros_like(m_state)

    t_idx = lax.broadcasted_iota(jnp.int32, (L, L), 0)
    s_idx = lax.broadcasted_iota(jnp.int32, (L, L), 1)
    fwd = d == 0
    sgn = 1 - 2 * d
    mask = (t_idx - s_idx) * sgn >= 0
    mask_t = (s_idx - t_idx) * sgn >= 0
    tri = mask.astype(F32)
    tri_t = mask_t.astype(F32)

    for j in range(nch):
        cj = j + d * (nch - 1 - 2 * j)
        off = pl.multiple_of(cj * L, L)
        G = gc_ref[pl.ds(off, L), :]
        GT = gr_ref[cj]
        Bc = _dot_exact(tri, G)
        BT = _dot_exact(GT, tri_t)
        gtot = jnp.sum(GT, axis=1, keepdims=True)
        hs = []
        for h in range(H):
            sl = slice(h * ML_DH, (h + 1) * ML_DH)
            q = q_ref[pl.ds(off, L), sl]
            k = k_ref[pl.ds(off, L), sl]
            v = v_ref[pl.ds(off, L), sl]
            li_col = G[:, h:h + 1]
            b_col = Bc[:, H + h:H + h + 1]
            li_row = GT[h:h + 1, :]
            b_row = BT[H + h:H + h + 1, :]
            g = gtot[H + h:H + h + 1, :]
            c0 = c_state[h]
            n0 = n_state[h]
            m0 = m_state[h][:, 0:1]

            D = jnp.where(mask, b_col - b_row + li_row, NEG)
            inter = b_col + m0
            m_t = jnp.maximum(inter, jnp.max(D, axis=1, keepdims=True))
            s = _dot_nt(q, k) * jnp.exp(D - m_t)
            sc = jnp.exp(inter - m_t)
            num = _dot(s.astype(BF16), v) + sc * _dot_nt(q, c0.astype(BF16))
            den = jnp.sum(s, axis=1, keepdims=True) + \
                sc * jnp.sum(q.astype(F32) * n0, axis=1, keepdims=True)
            den = jnp.maximum(jnp.abs(den), jnp.exp(-m_t))
            hs.append(num / den)

            a_col = g - b_col + li_col
            a_row = g - b_row + li_row
            m_loc = jnp.max(a_row, axis=1, keepdims=True)
            w_col = jnp.exp(a_col - m_loc)
            c_loc = _dot_tn((w_col * v.astype(F32)).astype(BF16), k)
            n_loc = jnp.sum(w_col * k.astype(F32), axis=0, keepdims=True)
            m_new = jnp.maximum(g + m0, m_loc)
            s_old = jnp.exp(g + m0 - m_new)
            s_loc = jnp.exp(m_loc - m_new)
            c_state[h] = s_old * c0 + s_loc * c_loc
            n_state[h] = s_old * n0 + s_loc * n_loc
            m_state[h] = jnp.broadcast_to(m_new, (1, LANES))

        hrow = pl.multiple_of(blk * ML_TT + cj * L, L)

        @pl.when(fwd)
        def _():
            for h in range(H):
                hf_ref[pl.ds(hrow, L), h * ML_DH:(h + 1) * ML_DH] = hs[h]

        @pl.when(jnp.logical_not(fwd))
        def _():
            for h in range(H):
                sl = slice(h * ML_DH, (h + 1) * ML_DH)
                tot = hf_ref[pl.ds(hrow, L), sl] + hs[h]
                hn = _rms(tot, gn_ref[h])
                y_ref[pl.ds(off, L), sl] = (jax.nn.sigmoid(o_ref[pl.ds(off, L), sl]) * hn).astype(BF16)


def _mlstm(q3, k3, v3, o3, gcol3, grow, gn):
    b, t, w = q3.shape
    tt = ML_TT
    nblk = t // tt
    nch = tt // ML_L
    blk_of = lambda d, i: i + d * (nblk - 1 - 2 * i)
    tile = pl.BlockSpec((None, tt, w), lambda bi, d, i: (bi, blk_of(d, i), 0))
    gc_spec = pl.BlockSpec((None, tt, LANES), lambda bi, d, i: (bi, blk_of(d, i), d))
    gr_spec = pl.BlockSpec((nch, 2 * ML_HEADS, ML_L), lambda bi, d, i: (bi * nblk + blk_of(d, i), d, 0))
    gn_spec = pl.BlockSpec(gn.shape, lambda bi, d, i: (0, 0, 0))
    y_spec = pl.BlockSpec((None, tt, w), lambda bi, d, i: (bi, d * blk_of(d, i) + (1 - d) * (nblk - 1), 0))
    return pl.pallas_call(
        functools.partial(_mlstm_kernel, nblk=nblk),
        grid=(b, 2, nblk),
        in_specs=[tile, tile, tile, tile, gc_spec, gr_spec, gn_spec],
        out_specs=y_spec,
        out_shape=jax.ShapeDtypeStruct((b, t, w), BF16),
        scratch_shapes=[
            pltpu.VMEM((t, w), F32),
            pltpu.VMEM((ML_HEADS, ML_DH, ML_DH), F32),
            pltpu.VMEM((ML_HEADS, 1, ML_DH), F32),
            pltpu.VMEM((ML_HEADS, 1, LANES), F32),
        ],
        compiler_params=pltpu.CompilerParams(
            dimension_semantics=("arbitrary", "arbitrary", "arbitrary"), vmem_limit_bytes=VMEM_LIMIT),
        name="mlstm",
    )(q3, k3, v3, o3, gcol3, grow, gn)


def _na_kernel(q_ref, k_ref, v_ref, bias_ref, out_ref, *, rows):
    r0 = pl.program_id(1) * NA_R
    nkeys = NA_KH * GRID_W

    def row_body(ri, carry):
        r = r0 + ri
        rs = jnp.clip(r - NA_KH // 2, 0, rows - NA_KH)
        delta = r - rs
        koff = pl.multiple_of(rs * GRID_W, GRID_W)
        qoff = pl.multiple_of(ri * GRID_W, GRID_W)
        qrow = q_ref[pl.ds(qoff, GRID_W), :]
        kslab = k_ref[pl.ds(koff, nkeys), :]
        vslab = v_ref[pl.ds(koff, nkeys), :]
        outs = []
        for h in range(NA_HEADS):
            sl = slice(h * NA_DH, (h + 1) * NA_DH)
            s = _dot_nt(qrow[:, sl], kslab[:, sl]) + bias_ref[delta, h]
            m = jnp.max(s, axis=-1, keepdims=True)
            p = jnp.exp(s - m)
            l = jnp.sum(p, axis=-1, keepdims=True)
            outs.append(_dot(p.astype(BF16), vslab[:, sl]) / l)
        out_ref[pl.ds(qoff, GRID_W), :] = jnp.concatenate(outs, axis=-1).astype(BF16)
        return carry

    lax.fori_loop(0, NA_R, row_body, 0)


def _na(q3, k3, v3, bias):
    b, t, w = q3.shape
    rows = t // GRID_W
    tq = NA_R * GRID_W
    qtile = pl.BlockSpec((None, tq, w), lambda bi, i: (bi, i, 0))
    seq = pl.BlockSpec((None, t, w), lambda bi, i: (bi, 0, 0), pipeline_mode=pl.Buffered(1))
    bias_spec = pl.BlockSpec(bias.shape, lambda bi, i: (0, 0, 0, 0), pipeline_mode=pl.Buffered(1))
    return pl.pallas_call(
        functools.partial(_na_kernel, rows=rows),
        grid=(b, rows // NA_R),
        in_specs=[qtile, seq, seq, bias_spec],
        out_specs=qtile,
        out_shape=jax.ShapeDtypeStruct((b, t, w), BF16),
        compiler_params=pltpu.CompilerParams(
            dimension_semantics=("parallel", "parallel"), vmem_limit_bytes=VMEM_LIMIT),
        name="natten",
    )(q3, k3, v3, bias)


def _na_bias_table(rpb):
    c = np.arange(GRID_W)
    cs = np.clip(c - NA_KW // 2, 0, GRID_W - NA_KW)
    cc = np.arange(GRID_W)
    valid = (cc[None, :] >= cs[:, None]) & (cc[None, :] < cs[:, None] + NA_KW)
    nh, ndr, ndc = rpb.shape
    lead = GRID_W - NA_KW
    w = jnp.pad(rpb.astype(F32), ((0, 0), (0, 0), (lead, 2 * GRID_W - lead - ndc)))
    skew = jnp.broadcast_to(w[:, :, None, :], (nh, ndr, GRID_W, 2 * GRID_W)).reshape(nh, ndr, -1)
    skew = skew[:, :, :GRID_W * (2 * GRID_W - 1)].reshape(nh, ndr, GRID_W, 2 * GRID_W - 1)
    toep = jnp.where(valid[None, None], skew[..., GRID_W - 1:], NEG)
    tabs = [toep[:, NA_KH - 1 - dl:2 * NA_KH - 1 - dl].transpose(0, 2, 1, 3).reshape(nh, GRID_W, -1)
            for dl in range(NA_KH)]
    return jnp.stack(tabs)


def _outproj_ffn2_kernel(x1_ref, yml_ref, yna_ref, woml_ref, wona_ref, g2_ref, wg_ref, wu_ref, wd_ref,
                         gf_ref, out_ref):
    x2 = x1_ref[...] + _dot(yml_ref[...], woml_ref[...]) + _dot(yna_ref[...], wona_ref[...])
    h = _rms(x2, g2_ref[...]).astype(BF16)
    x3 = x2 + _swiglu_half(h, wg_ref, wu_ref, wd_ref)
    out_ref[...] = _rms(x3, gf_ref[...])


def _outproj_ffn2(x1, yml, yna, woml, wona, g2, wg, wu, wd, gf):
    n = x1.shape[0]
    tm = FFN_TM
    tok = lambda w: pl.BlockSpec((tm, w), lambda i: (i, 0))
    return pl.pallas_call(
        _outproj_ffn2_kernel,
        grid=(n // tm,),
        in_specs=[tok(D_MODEL), tok(ML_WIDTH), tok(NA_WIDTH)] + [
            _const_spec(a.shape) for a in (woml, wona, g2, wg, wu, wd, gf)],
        out_specs=tok(D_MODEL),
        out_shape=jax.ShapeDtypeStruct((n, D_MODEL), F32),
        compiler_params=pltpu.CompilerParams(
            dimension_semantics=("parallel",), vmem_limit_bytes=VMEM_LIMIT),
        name="outproj_ffn2",
    )(x1, yml, yna, woml, wona, g2, wg, wu, wd, gf)


def _layer(x, norm_ffn1, w1_gate, w1_up, w1_down, norm_mix, w_in, b_gates, conv_w, conv_b, w_q_ml,
           w_k_ml, gn_ml, gq_na, gk_na, rpb, w_out, norm_ffn2, w2_gate, w2_up, w2_down, norm_final):
    b, t, dm = x.shape
    n = b * t
    W = ML_WIDTH
    row = lambda a: a.reshape(1, -1).astype(F32)
    g0 = 3 * W
    wmain = jnp.concatenate([w_in[:, :g0], w_in[:, g0 + N_GATES:]], axis=1).astype(BF16)
    wgate = w_in[:, g0:g0 + N_GATES]
    half = N_GATES // 2
    zpad = jnp.zeros((dm, LANES - half), wgate.dtype)
    wgc = jnp.concatenate([wgate[:, :half], zpad, wgate[:, half:], zpad], axis=1).astype(BF16)
    bpad = jnp.zeros((LANES - half,), F32)
    bgc = jnp.concatenate([b_gates[:half], bpad, b_gates[half:], bpad]).reshape(1, -1)
    wgr = wgate.T.astype(BF16)
    bgr = b_gates.reshape(-1, 1).astype(F32)
    ones = jnp.asarray(np.kron(np.eye(NA_HEADS), np.ones((NA_DH, NA_DH))), BF16)
    gq = jnp.tile(gq_na.astype(F32), NA_HEADS).reshape(1, -1)
    gk = jnp.tile(gk_na.astype(F32), NA_HEADS).reshape(1, -1)

    x1, c, vml, o, gcol, grow, qn, kn, vn = _ffn1_inproj(
        x.reshape(n, dm), row(norm_ffn1), w1_gate.astype(BF16), w1_up.astype(BF16), w1_down.astype(BF16),
        row(norm_mix), wmain, wgc, bgc, wgr, bgr, ones, gq, gk)

    seq = lambda a: a.reshape(b, t, a.shape[-1])
    q_ml, k_ml = _conv_qk(seq(c), conv_w.astype(F32), row(conv_b), w_q_ml.astype(BF16), w_k_ml.astype(BF16))
    y_ml = _mlstm(q_ml, k_ml, seq(vml), seq(o), seq(gcol), grow, gn_ml.reshape(ML_HEADS, 1, ML_DH).astype(F32))
    y_na = _na(seq(qn), seq(kn), seq(vn), _na_bias_table(rpb))

    out = _outproj_ffn2(x1, y_ml.reshape(n, W), y_na.reshape(n, NA_WIDTH),
                        w_out[:W].astype(BF16), w_out[W:].astype(BF16), row(norm_ffn2),
                        w2_gate.astype(BF16), w2_up.astype(BF16), w2_down.astype(BF16), row(norm_final))
    return out.reshape(b, t, dm)


def kernel(x, norm_ffn1, w1_gate, w1_up, w1_down, norm_mix, w_in, b_gates, conv_w, conv_b, w_q_ml, w_k_ml,
           gn_ml, gq_na, gk_na, rpb, w_out, norm_ffn2, w2_gate, w2_up, w2_down, norm_final):
    depth = norm_ffn1.shape[0]
    for l in range(depth):
        x = _layer(x, norm_ffn1[l], w1_gate[l], w1_up[l], w1_down[l], norm_mix[l], w_in[l], b_gates[l],
                   conv_w[l], conv_b[l], w_q_ml[l], w_k_ml[l], gn_ml[l], gq_na[l], gk_na[l], rpb[l],
                   w_out[l], norm_ffn2[l], w2_gate[l], w2_up[l], w2_down[l], norm_final[l])
    return x
```

```python
import functools

import jax
import jax.numpy as jnp
import numpy as np
from jax import lax
from jax.experimental import pallas as pl
from jax.experimental.pallas import tpu as pltpu

F32 = jnp.float32
BF16 = jnp.bfloat16

D_MODEL = 1024
D_FF = 2816
GRID_W = 64
ML_HEADS = 4
ML_DH = 128
ML_WIDTH = ML_HEADS * ML_DH
CONV_W = 5
NA_HEADS = 8
NA_DH = 64
NA_WIDTH = NA_HEADS * NA_DH
NA_KH = 8
NA_KW = 16
N_GATES = 4 * ML_HEADS
EPS = 1e-6
NEG = -1e30

SUBLANES = 8
LANES = 128

FFN_TM = 256
FFN_CK = 1408
ML_L = 128
ML_TT = 256
CONV_TT = 512
NA_R = 8
NA_PACK = 4
VMEM_LIMIT = 56 * 1024 * 1024


def _dot(a, b):
    return jnp.dot(a, b, preferred_element_type=F32)


def _dot_nt(a, b):
    return lax.dot_general(a, b, (((1,), (1,)), ((), ())), preferred_element_type=F32)


def _dot_tn(a, b):
    return lax.dot_general(a, b, (((0,), (0,)), ((), ())), preferred_element_type=F32)


def _dot_exact(a, b):
    return jnp.dot(a, b, preferred_element_type=F32, precision=lax.Precision.HIGHEST)


def _rms(x, g):
    ms = jnp.mean(x * x, axis=-1, keepdims=True)
    return x * lax.rsqrt(ms + EPS) * g


def _log_sigmoid(x):
    return jnp.minimum(x, 0.0) - jnp.log1p(jnp.exp(-jnp.abs(x)))


def _swiglu_half(h, wg_ref, wu_ref, wd_ref):
    acc = None
    for j in range(D_FF // FFN_CK):
        sl = slice(j * FFN_CK, (j + 1) * FFN_CK)
        g = _dot(h, wg_ref[:, sl])
        u = _dot(h, wu_ref[:, sl])
        a = (g * jax.nn.sigmoid(g) * u).astype(BF16)
        part = _dot(a, wd_ref[sl, :])
        acc = part if acc is None else acc + part
    return 0.5 * acc


def _group_mean_sq(x, ones_ref):
    xx = x * x
    hi = xx.astype(BF16)
    lo = (xx - hi.astype(F32)).astype(BF16)
    s = _dot(hi, ones_ref[...]) + _dot(lo, ones_ref[...])
    return s * (1.0 / NA_DH)


def _ffn1_inproj_kernel(x_ref, g1_ref, wg_ref, wu_ref, wd_ref, gm_ref, wmain_ref, wgc_ref, bgc_ref,
                        wgr_ref, bgr_ref, ones_ref, gq_ref, gk_ref,
                        x1_ref, c_ref, vml_ref, o_ref, gcol_ref, grow_ref, qn_ref, kn_ref, vn_ref):
    x = x_ref[...]
    h = _rms(x, g1_ref[...]).astype(BF16)
    x1 = x + _swiglu_half(h, wg_ref, wu_ref, wd_ref)
    x1_ref[...] = x1
    h2 = _rms(x1, gm_ref[...]).astype(BF16)
    p = _dot(h2, wmain_ref[...])
    W = ML_WIDTH
    c_ref[...] = p[:, 0:W]
    vml_ref[...] = p[:, W:2 * W].astype(BF16)
    o_ref[...] = p[:, 2 * W:3 * W]
    q = p[:, 3 * W:4 * W]
    k = p[:, 4 * W:5 * W]
    vn_ref[...] = p[:, 5 * W:6 * W].astype(BF16)
    qn = q * lax.rsqrt(_group_mean_sq(q, ones_ref) + EPS) * gq_ref[...] * (NA_DH ** -0.5)
    kn = k * lax.rsqrt(_group_mean_sq(k, ones_ref) + EPS) * gk_ref[...]
    qn_ref[...] = qn.astype(BF16)
    kn_ref[...] = kn.astype(BF16)
    gc = _dot(h2, wgc_ref[...]) + bgc_ref[...]
    lane = lax.broadcasted_iota(jnp.int32, gc.shape, 1) % LANES
    gcol_ref[...] = jnp.where((lane >= ML_HEADS) & (lane < 2 * ML_HEADS), _log_sigmoid(gc), gc)
    gr = _dot_nt(wgr_ref[...], h2) + bgr_ref[...]
    row = lax.broadcasted_iota(jnp.int32, gr.shape, 0) % (2 * ML_HEADS)
    gr = jnp.where(row >= ML_HEADS, _log_sigmoid(gr), gr)
    for cidx in range(FFN_TM // ML_L):
        grow_ref[cidx] = gr[:, cidx * ML_L:(cidx + 1) * ML_L]


def _const_spec(shape):
    nd = len(shape)
    return pl.BlockSpec(shape, lambda *_: (0,) * nd, pipeline_mode=pl.Buffered(1))


def _ffn1_inproj(x2d, g1, wg, wu, wd, gm, wmain, wgc, bgc, wgr, bgr, ones, gq, gk):
    n = x2d.shape[0]
    tm = FFN_TM
    W = ML_WIDTH
    tok = lambda w: pl.BlockSpec((tm, w), lambda i: (i, 0))
    out_shape = (
        jax.ShapeDtypeStruct((n, D_MODEL), F32),
        jax.ShapeDtypeStruct((n, W), F32),
        jax.ShapeDtypeStruct((n, W), BF16),
        jax.ShapeDtypeStruct((n, W), F32),
        jax.ShapeDtypeStruct((n, 2 * LANES), F32),
        jax.ShapeDtypeStruct((n // ML_L, N_GATES, ML_L), F32),
        jax.ShapeDtypeStruct((n, W), BF16),
        jax.ShapeDtypeStruct((n, W), BF16),
        jax.ShapeDtypeStruct((n, W), BF16),
    )
    out_specs = (
        tok(D_MODEL), tok(W), tok(W), tok(W), tok(2 * LANES),
        pl.BlockSpec((tm // ML_L, N_GATES, ML_L), lambda i: (i, 0, 0)),
        tok(W), tok(W), tok(W),
    )
    in_specs = [tok(D_MODEL)] + [_const_spec(a.shape) for a in
                                 (g1, wg, wu, wd, gm, wmain, wgc, bgc, wgr, bgr, ones, gq, gk)]
    return pl.pallas_call(
        _ffn1_inproj_kernel,
        grid=(n // tm,),
        in_specs=in_specs,
        out_specs=out_specs,
        out_shape=out_shape,
        compiler_params=pltpu.CompilerParams(
            dimension_semantics=("parallel",), vmem_limit_bytes=VMEM_LIMIT),
        name="ffn1_inproj",
    )(x2d, g1, wg, wu, wd, gm, wmain, wgc, bgc, wgr, bgr, ones, gq, gk)


def _conv_qk_kernel(c_ref, prev_ref, next_ref, cw_ref, cb_ref, wq_ref, wk_ref, q_ref, k_ref, pad_ref):
    i = pl.program_id(1)
    nb = pl.num_programs(1)
    tt = CONV_TT
    half = CONV_W // 2
    pad_ref[0:SUBLANES, :] = jnp.where(i > 0, prev_ref[...], 0.0)
    pad_ref[SUBLANES:SUBLANES + tt, :] = c_ref[...]
    pad_ref[SUBLANES + tt:2 * SUBLANES + tt, :] = jnp.where(i < nb - 1, next_ref[...], 0.0)
    y = cb_ref[...]
    for kk in range(CONV_W):
        start = SUBLANES + kk - half
        y = y + pad_ref[start:start + tt, :] * cw_ref[kk:kk + 1, :]
    u = y * jax.nn.sigmoid(y)
    for h in range(ML_HEADS):
        sl = slice(h * ML_DH, (h + 1) * ML_DH)
        uh = u[:, sl].astype(BF16)
        q_ref[:, sl] = _dot(uh, wq_ref[h]).astype(BF16)
        k_ref[:, sl] = (_dot(uh, wk_ref[h]) * (ML_DH ** -0.5)).astype(BF16)


def _conv_qk(c3, cw, cb, wq, wk):
    b, t, w = c3.shape
    tt = CONV_TT
    nb = t // tt
    per = tt // SUBLANES
    last = t // SUBLANES - 1
    tile = pl.BlockSpec((None, tt, w), lambda bi, i: (bi, i, 0))
    prev = pl.BlockSpec((None, SUBLANES, w), lambda bi, i: (bi, jnp.maximum(i * per - 1, 0), 0))
    nxt = pl.BlockSpec((None, SUBLANES, w), lambda bi, i: (bi, jnp.minimum((i + 1) * per, last), 0))
    return pl.pallas_call(
        _conv_qk_kernel,
        grid=(b, nb),
        in_specs=[tile, prev, nxt] + [
            pl.BlockSpec(a.shape, functools.partial(lambda nd, bi, i: (0,) * nd, a.ndim))
            for a in (cw, cb, wq, wk)],
        out_specs=(tile, tile),
        out_shape=(jax.ShapeDtypeStruct((b, t, w), BF16), jax.ShapeDtypeStruct((b, t, w), BF16)),
        scratch_shapes=[pltpu.VMEM((tt + 2 * SUBLANES, w), F32)],
        compiler_params=pltpu.CompilerParams(dimension_semantics=("parallel", "parallel")),
        name="conv_qk",
    )(c3, c3, c3, cw, cb, wq, wk)


def _mlstm_kernel(q_ref, k_ref, v_ref, o_ref, gc_ref, gr_ref, gn_ref, y_ref,
                  hf_ref, c_state, n_state, m_state, *, nblk):
    d = pl.program_id(1)
    i = pl.program_id(2)
    blk = i + d * (nblk - 1 - 2 * i)
    L = ML_L
    nch = ML_TT // L
    H = ML_HEADS

    @pl.when(i == 0)
    def _():
        c_state[...] = jnp.zeros_like(c_state)
        n_state[...] = jnp.zeros_like(n_state)
        m_state[...] = jnp.zeros_like(m_state)

    t_idx = lax.broadcasted_iota(jnp.int32, (L, L), 0)
    s_idx = lax.broadcasted_iota(jnp.int32, (L, L), 1)
    fwd = d == 0
    sgn = 1 - 2 * d
    mask = (t_idx - s_idx) * sgn >= 0
    mask_t = (s_idx - t_idx) * sgn >= 0
    tri = mask.astype(F32)
    tri_t = mask_t.astype(F32)

    for j in range(nch):
        cj = j + d * (nch - 1 - 2 * j)
        off = pl.multiple_of(cj * L, L)
        G = gc_ref[pl.ds(off, L), :]
        GT = gr_ref[cj]
        Bc = _dot_exact(tri, G)
        BT = _dot_exact(GT, tri_t)
        gtot = jnp.sum(GT, axis=1, keepdims=True)
        hs = []
        for h in range(H):
            sl = slice(h * ML_DH, (h + 1) * ML_DH)
            q = q_ref[pl.ds(off, L), sl]
            k = k_ref[pl.ds(off, L), sl]
            v = v_ref[pl.ds(off, L), sl]
            li_col = G[:, h:h + 1]
            b_col = Bc[:, H + h:H + h + 1]
            li_row = GT[h:h + 1, :]
            b_row = BT[H + h:H + h + 1, :]
            g = gtot[H + h:H + h + 1, :]
            c0 = c_state[h]
            n0 = n_state[h]
            m0 = m_state[h][:, 0:1]

            D = jnp.where(mask, b_col - b_row + li_row, NEG)
            inter = b_col + m0
            m_t = jnp.maximum(inter, jnp.max(D, axis=1, keepdims=True))
            s = _dot_nt(q, k) * jnp.exp(D - m_t)
            sc = jnp.exp(inter - m_t)
            num = _dot(s.astype(BF16), v) + sc * _dot_nt(q, c0.astype(BF16))
            den = jnp.sum(s, axis=1, keepdims=True) + \
                sc * jnp.sum(q.astype(F32) * n0, axis=1, keepdims=True)
            den = jnp.maximum(jnp.abs(den), jnp.exp(-m_t))
            hs.append(num / den)

            a_col = g - b_col + li_col
            a_row = g - b_row + li_row
            m_loc = jnp.max(a_row, axis=1, keepdims=True)
            w_col = jnp.exp(a_col - m_loc)
            c_loc = _dot_tn((w_col * v.astype(F32)).astype(BF16), k)
            n_loc = jnp.sum(w_col * k.astype(F32), axis=0, keepdims=True)
            m_new = jnp.maximum(g + m0, m_loc)
            s_old = jnp.exp(g + m0 - m_new)
            s_loc = jnp.exp(m_loc - m_new)
            c_state[h] = s_old * c0 + s_loc * c_loc
            n_state[h] = s_old * n0 + s_loc * n_loc
            m_state[h] = jnp.broadcast_to(m_new, (1, LANES))

        hrow = pl.multiple_of(blk * ML_TT + cj * L, L)

        @pl.when(fwd)
        def _():
            for h in range(H):
                hf_ref[pl.ds(hrow, L), h * ML_DH:(h + 1) * ML_DH] = hs[h]

        @pl.when(jnp.logical_not(fwd))
        def _():
            for h in range(H):
                sl = slice(h * ML_DH, (h + 1) * ML_DH)
                tot = hf_ref[pl.ds(hrow, L), sl] + hs[h]
                hn = _rms(tot, gn_ref[h])
                y_ref[pl.ds(off, L), sl] = (jax.nn.sigmoid(o_ref[pl.ds(off, L), sl]) * hn).astype(BF16)


def _mlstm(q3, k3, v3, o3, gcol3, grow, gn):
    b, t, w = q3.shape
    tt = ML_TT
    nblk = t // tt
    nch = tt // ML_L
    blk_of = lambda d, i: i + d * (nblk - 1 - 2 * i)
    tile = pl.BlockSpec((None, tt, w), lambda bi, d, i: (bi, blk_of(d, i), 0))
    gc_spec = pl.BlockSpec((None, tt, LANES), lambda bi, d, i: (bi, blk_of(d, i), d))
    gr_spec = pl.BlockSpec((nch, 2 * ML_HEADS, ML_L), lambda bi, d, i: (bi * nblk + blk_of(d, i), d, 0))
    gn_spec = pl.BlockSpec(gn.shape, lambda bi, d, i: (0, 0, 0))
    y_spec = pl.BlockSpec((None, tt, w), lambda bi, d, i: (bi, d * blk_of(d, i) + (1 - d) * (nblk - 1), 0))
    return pl.pallas_call(
        functools.partial(_mlstm_kernel, nblk=nblk),
        grid=(b, 2, nblk),
        in_specs=[tile, tile, tile, tile, gc_spec, gr_spec, gn_spec],
        out_specs=y_spec,
        out_shape=jax.ShapeDtypeStruct((b, t, w), BF16),
        scratch_shapes=[
            pltpu.VMEM((t, w), F32),
            pltpu.VMEM((ML_HEADS, ML_DH, ML_DH), F32),
            pltpu.VMEM((ML_HEADS, 1, ML_DH), F32),
            pltpu.VMEM((ML_HEADS, 1, LANES), F32),
        ],
        compiler_params=pltpu.CompilerParams(
            dimension_semantics=("arbitrary", "arbitrary", "arbitrary"), vmem_limit_bytes=VMEM_LIMIT),
        name="mlstm",
    )(q3, k3, v3, o3, gcol3, grow, gn)


def _na_kernel(q_ref, k_ref, v_ref, bias_ref, out_ref, *, rows):
    r0 = pl.program_id(1) * NA_R
    nkeys = NA_KH * GRID_W
    gw = NA_PACK * NA_DH
    row_blk = lax.broadcasted_iota(jnp.int32, (NA_PACK * GRID_W, gw), 0) // GRID_W
    lane_blk = lax.broadcasted_iota(jnp.int32, (NA_PACK * GRID_W, gw), 1) // NA_DH
    diag = row_blk == lane_blk
    out_blk = lax.broadcasted_iota(jnp.int32, (GRID_W, gw), 1) // NA_DH

    def row_body(ri, carry):
        r = r0 + ri
        rs = jnp.clip(r - NA_KH // 2, 0, rows - NA_KH)
        delta = r - rs
        koff = pl.multiple_of(rs * GRID_W, GRID_W)
        qoff = pl.multiple_of(ri * GRID_W, GRID_W)
        outs = []
        for g in range(NA_HEADS // NA_PACK):
            sl = slice(g * gw, (g + 1) * gw)
            q4 = q_ref[pl.ds(qoff, GRID_W), sl]
            qbd = jnp.where(diag, jnp.concatenate([q4] * NA_PACK, axis=0), jnp.zeros((), BF16))
            s = _dot_nt(qbd, k_ref[pl.ds(koff, nkeys), sl]) + bias_ref[delta, g]
            m = jnp.max(s, axis=-1, keepdims=True)
            p = jnp.exp(s - m)
            l = jnp.sum(p, axis=-1, keepdims=True)
            o = _dot(p.astype(BF16), v_ref[pl.ds(koff, nkeys), sl]) * (1.0 / l)
            og = o[(NA_PACK - 1) * GRID_W:, :]
            for h in range(NA_PACK - 2, -1, -1):
                og = jnp.where(out_blk == h, o[h * GRID_W:(h + 1) * GRID_W, :], og)
            outs.append(og)
        out_ref[pl.ds(qoff, GRID_W), :] = jnp.concatenate(outs, axis=-1).astype(BF16)
        return carry

    lax.fori_loop(0, NA_R, row_body, 0)


def _na(q3, k3, v3, bias):
    b, t, w = q3.shape
    rows = t // GRID_W
    tq = NA_R * GRID_W
    qtile = pl.BlockSpec((None, tq, w), lambda bi, i: (bi, i, 0))
    seq = pl.BlockSpec((None, t, w), lambda bi, i: (bi, 0, 0), pipeline_mode=pl.Buffered(1))
    bias_spec = pl.BlockSpec(bias.shape, lambda bi, i: (0, 0, 0, 0), pipeline_mode=pl.Buffered(1))
    return pl.pallas_call(
        functools.partial(_na_kernel, rows=rows),
        grid=(b, rows // NA_R),
        in_specs=[qtile, seq, seq, bias_spec],
        out_specs=qtile,
        out_shape=jax.ShapeDtypeStruct((b, t, w), BF16),
        compiler_params=pltpu.CompilerParams(
            dimension_semantics=("parallel", "parallel"), vmem_limit_bytes=VMEM_LIMIT),
        name="natten",
    )(q3, k3, v3, bias)


def _na_bias_table(rpb):
    c = np.arange(GRID_W)
    cs = np.clip(c - NA_KW // 2, 0, GRID_W - NA_KW)
    cc = np.arange(GRID_W)
    valid = (cc[None, :] >= cs[:, None]) & (cc[None, :] < cs[:, None] + NA_KW)
    nh, ndr, ndc = rpb.shape
    lead = GRID_W - NA_KW
    w = jnp.pad(rpb.astype(F32), ((0, 0), (0, 0), (lead, 2 * GRID_W - lead - ndc)))
    skew = jnp.broadcast_to(w[:, :, None, :], (nh, ndr, GRID_W, 2 * GRID_W)).reshape(nh, ndr, -1)
    skew = skew[:, :, :GRID_W * (2 * GRID_W - 1)].reshape(nh, ndr, GRID_W, 2 * GRID_W - 1)
    toep = jnp.where(valid[None, None], skew[..., GRID_W - 1:], NEG)
    tabs = [toep[:, NA_KH - 1 - dl:2 * NA_KH - 1 - dl].transpose(0, 2, 1, 3).reshape(nh, GRID_W, -1)
            for dl in range(NA_KH)]
    return jnp.stack(tabs).reshape(NA_KH, NA_HEADS // NA_PACK, NA_PACK * GRID_W, NA_KH * GRID_W)


def _outproj_ffn2_kernel(x1_ref, yml_ref, yna_ref, woml_ref, wona_ref, g2_ref, wg_ref, wu_ref, wd_ref,
                         gf_ref, out_ref):
    x2 = x1_ref[...] + _dot(yml_ref[...], woml_ref[...]) + _dot(yna_ref[...], wona_ref[...])
    h = _rms(x2, g2_ref[...]).astype(BF16)
    x3 = x2 + _swiglu_half(h, wg_ref, wu_ref, wd_ref)
    out_ref[...] = _rms(x3, gf_ref[...])


def _outproj_ffn2(x1, yml, yna, woml, wona, g2, wg, wu, wd, gf):
    n = x1.shape[0]
    tm = FFN_TM
    tok = lambda w: pl.BlockSpec((tm, w), lambda i: (i, 0))
    return pl.pallas_call(
        _outproj_ffn2_kernel,
        grid=(n // tm,),
        in_specs=[tok(D_MODEL), tok(ML_WIDTH), tok(NA_WIDTH)] + [
            _const_spec(a.shape) for a in (woml, wona, g2, wg, wu, wd, gf)],
        out_specs=tok(D_MODEL),
        out_shape=jax.ShapeDtypeStruct((n, D_MODEL), F32),
        compiler_params=pltpu.CompilerParams(
            dimension_semantics=("parallel",), vmem_limit_bytes=VMEM_LIMIT),
        name="outproj_ffn2",
    )(x1, yml, yna, woml, wona, g2, wg, wu, wd, gf)


def _layer(x, norm_ffn1, w1_gate, w1_up, w1_down, norm_mix, w_in, b_gates, conv_w, conv_b, w_q_ml,
           w_k_ml, gn_ml, gq_na, gk_na, rpb, w_out, norm_ffn2, w2_gate, w2_up, w2_down, norm_final):
    b, t, dm = x.shape
    n = b * t
    W = ML_WIDTH
    row = lambda a: a.reshape(1, -1).astype(F32)
    g0 = 3 * W
    wmain = jnp.concatenate([w_in[:, :g0], w_in[:, g0 + N_GATES:]], axis=1).astype(BF16)
    wgate = w_in[:, g0:g0 + N_GATES]
    half = N_GATES // 2
    zpad = jnp.zeros((dm, LANES - half), wgate.dtype)
    wgc = jnp.concatenate([wgate[:, :half], zpad, wgate[:, half:], zpad], axis=1).astype(BF16)
    bpad = jnp.zeros((LANES - half,), F32)
    bgc = jnp.concatenate([b_gates[:half], bpad, b_gates[half:], bpad]).reshape(1, -1)
    wgr = wgate.T.astype(BF16)
    bgr = b_gates.reshape(-1, 1).astype(F32)
    ones = jnp.asarray(np.kron(np.eye(NA_HEADS), np.ones((NA_DH, NA_DH))), BF16)
    gq = jnp.tile(gq_na.astype(F32), NA_HEADS).reshape(1, -1)
    gk = jnp.tile(gk_na.astype(F32), NA_HEADS).reshape(1, -1)

    x1, c, vml, o, gcol, grow, qn, kn, vn = _ffn1_inproj(
        x.reshape(n, dm), row(norm_ffn1), w1_gate.astype(BF16), w1_up.astype(BF16), w1_down.astype(BF16),
        row(norm_mix), wmain, wgc, bgc, wgr, bgr, ones, gq, gk)

    seq = lambda a: a.reshape(b, t, a.shape[-1])
    q_ml, k_ml = _conv_qk(seq(c), conv_w.astype(F32), row(conv_b), w_q_ml.astype(BF16), w_k_ml.astype(BF16))
    y_ml = _mlstm(q_ml, k_ml, seq(vml), seq(o), seq(gcol), grow, gn_ml.reshape(ML_HEADS, 1, ML_DH).astype(F32))
    y_na = _na(seq(qn), seq(kn), seq(vn), _na_bias_table(rpb))

    out = _outproj_ffn2(x1, y_ml.reshape(n, W), y_na.reshape(n, NA_WIDTH),
                        w_out[:W].astype(BF16), w_out[W:].astype(BF16), row(norm_ffn2),
                        w2_gate.astype(BF16), w2_up.astype(BF16), w2_down.astype(BF16), row(norm_final))
    return out.reshape(b, t, dm)


def kernel(x, norm_ffn1, w1_gate, w1_up, w1_down, norm_mix, w_in, b_gates, conv_w, conv_b, w_q_ml, w_k_ml,
           gn_ml, gq_na, gk_na, rpb, w_out, norm_ffn2, w2_gate, w2_up, w2_down, norm_final):
    depth = norm_ffn1.shape[0]
    for l in range(depth):
        x = _layer(x, norm_ffn1[l], w1_gate[l], w1_up[l], w1_down[l], norm_mix[l], w_in[l], b_gates[l],
                   conv_w[l], conv_b[l], w_q_ml[l], w_k_ml[l], gn_ml[l], gq_na[l], gk_na[l], rpb[l],
                   w_out[l], norm_ffn2[l], w2_gate[l], w2_up[l], w2_down[l], norm_final[l])
    return x
```

```python
import functools

import jax
import jax.numpy as jnp
import numpy as np
from jax import lax
from jax.experimental import pallas as pl
from jax.experimental.pallas import tpu as pltpu

F32 = jnp.float32
BF16 = jnp.bfloat16

D_MODEL = 1024
D_FF = 2816
GRID_W = 64
ML_HEADS = 4
ML_DH = 128
ML_WIDTH = ML_HEADS * ML_DH
CONV_W = 5
NA_HEADS = 8
NA_DH = 64
NA_WIDTH = NA_HEADS * NA_DH
NA_KH = 8
NA_KW = 16
N_GATES = 4 * ML_HEADS
EPS = 1e-6
NEG = -1e30

SUBLANES = 8
LANES = 128

FFN_TM = 256
FFN_CK = 1408
ML_L = 128
ML_TT = 256
CONV_TT = 512
NA_R = 8
NA_PACK = 4
VMEM_LIMIT = 56 * 1024 * 1024


def _dot(a, b):
    return jnp.dot(a, b, preferred_element_type=F32)


def _dot_nt(a, b):
    return lax.dot_general(a, b, (((1,), (1,)), ((), ())), preferred_element_type=F32)


def _dot_tn(a, b):
    return lax.dot_general(a, b, (((0,), (0,)), ((), ())), preferred_element_type=F32)


def _dot_exact(a, b):
    return jnp.dot(a, b, preferred_element_type=F32, precision=lax.Precision.HIGHEST)


def _rms(x, g):
    ms = jnp.mean(x * x, axis=-1, keepdims=True)
    return x * lax.rsqrt(ms + EPS) * g


def _log_sigmoid(x):
    return jnp.minimum(x, 0.0) - jnp.log1p(jnp.exp(-jnp.abs(x)))


def _swiglu_half(h, wg_ref, wu_ref, wd_ref):
    acc = None
    for j in range(D_FF // FFN_CK):
        sl = slice(j * FFN_CK, (j + 1) * FFN_CK)
        g = _dot(h, wg_ref[:, sl])
        u = _dot(h, wu_ref[:, sl])
        a = (g * jax.nn.sigmoid(g) * u).astype(BF16)
        part = _dot(a, wd_ref[sl, :])
        acc = part if acc is None else acc + part
    return 0.5 * acc


def _group_mean_sq(x, ones_ref):
    xx = x * x
    hi = xx.astype(BF16)
    lo = (xx - hi.astype(F32)).astype(BF16)
    s = _dot(hi, ones_ref[...]) + _dot(lo, ones_ref[...])
    return s * (1.0 / NA_DH)


def _ffn1_inproj_kernel(x_ref, g1_ref, wg_ref, wu_ref, wd_ref, gm_ref, wmain_ref,
                        wgr_ref, bgr_ref, ones_ref, gq_ref, gk_ref,
                        x1_ref, c_ref, vml_ref, o_ref, ga_ref, gb_ref, qn_ref, kn_ref, vn_ref):
    x = x_ref[...]
    h = _rms(x, g1_ref[...]).astype(BF16)
    x1 = x + _swiglu_half(h, wg_ref, wu_ref, wd_ref)
    x1_ref[...] = x1
    h2 = _rms(x1, gm_ref[...]).astype(BF16)
    p = _dot(h2, wmain_ref[...])
    W = ML_WIDTH
    c_ref[...] = p[:, 0:W]
    vml_ref[...] = p[:, W:2 * W].astype(BF16)
    o_ref[...] = p[:, 2 * W:3 * W]
    q = p[:, 3 * W:4 * W]
    k = p[:, 4 * W:5 * W]
    vn_ref[...] = p[:, 5 * W:6 * W].astype(BF16)
    qn = q * lax.rsqrt(_group_mean_sq(q, ones_ref) + EPS) * gq_ref[...] * (NA_DH ** -0.5)
    kn = k * lax.rsqrt(_group_mean_sq(k, ones_ref) + EPS) * gk_ref[...]
    qn_ref[...] = qn.astype(BF16)
    kn_ref[...] = kn.astype(BF16)
    gr = _dot_nt(wgr_ref[...], h2) + bgr_ref[...]
    row = lax.broadcasted_iota(jnp.int32, gr.shape, 0) % (4 * ML_HEADS)
    gr = jnp.where((row >= ML_HEADS) & (row < 3 * ML_HEADS), _log_sigmoid(gr), gr)
    for dd in range(2):
        for cidx in range(FFN_TM // ML_L):
            lanes = slice(cidx * ML_L, (cidx + 1) * ML_L)
            ga_ref[dd, cidx] = gr[dd * 16:dd * 16 + 8, lanes]
            gb_ref[dd, cidx] = gr[dd * 16 + 8:dd * 16 + 16, lanes]


def _const_spec(shape):
    nd = len(shape)
    return pl.BlockSpec(shape, lambda *_: (0,) * nd, pipeline_mode=pl.Buffered(1))


def _ffn1_inproj(x2d, g1, wg, wu, wd, gm, wmain, wgr, bgr, ones, gq, gk):
    n = x2d.shape[0]
    tm = FFN_TM
    W = ML_WIDTH
    tok = lambda w: pl.BlockSpec((tm, w), lambda i: (i, 0))
    out_shape = (
        jax.ShapeDtypeStruct((n, D_MODEL), F32),
        jax.ShapeDtypeStruct((n, W), F32),
        jax.ShapeDtypeStruct((n, W), BF16),
        jax.ShapeDtypeStruct((n, W), F32),
        jax.ShapeDtypeStruct((2, n // ML_L, SUBLANES, ML_L), F32),
        jax.ShapeDtypeStruct((2, n // ML_L, SUBLANES, ML_L), F32),
        jax.ShapeDtypeStruct((n, W), BF16),
        jax.ShapeDtypeStruct((n, W), BF16),
        jax.ShapeDtypeStruct((n, W), BF16),
    )
    out_specs = (
        tok(D_MODEL), tok(W), tok(W), tok(W),
        pl.BlockSpec((2, tm // ML_L, SUBLANES, ML_L), lambda i: (0, i, 0, 0)),
        pl.BlockSpec((2, tm // ML_L, SUBLANES, ML_L), lambda i: (0, i, 0, 0)),
        tok(W), tok(W), tok(W),
    )
    in_specs = [tok(D_MODEL)] + [_const_spec(a.shape) for a in
                                 (g1, wg, wu, wd, gm, wmain, wgr, bgr, ones, gq, gk)]
    return pl.pallas_call(
        _ffn1_inproj_kernel,
        grid=(n // tm,),
        in_specs=in_specs,
        out_specs=out_specs,
        out_shape=out_shape,
        compiler_params=pltpu.CompilerParams(
            dimension_semantics=("parallel",), vmem_limit_bytes=VMEM_LIMIT),
        name="ffn1_inproj",
    )(x2d, g1, wg, wu, wd, gm, wmain, wgr, bgr, ones, gq, gk)


def _conv_qk_kernel(c_ref, prev_ref, next_ref, cw_ref, cb_ref, wq_ref, wkt_ref, q_ref, kt_ref, pad_ref):
    i = pl.program_id(1)
    nb = pl.num_programs(1)
    tt = CONV_TT
    half = CONV_W // 2
    pad_ref[0:SUBLANES, :] = jnp.where(i > 0, prev_ref[...], 0.0)
    pad_ref[SUBLANES:SUBLANES + tt, :] = c_ref[...]
    pad_ref[SUBLANES + tt:2 * SUBLANES + tt, :] = jnp.where(i < nb - 1, next_ref[...], 0.0)
    y = cb_ref[...]
    for kk in range(CONV_W):
        start = SUBLANES + kk - half
        y = y + pad_ref[start:start + tt, :] * cw_ref[kk:kk + 1, :]
    u = y * jax.nn.sigmoid(y)
    for h in range(ML_HEADS):
        sl = slice(h * ML_DH, (h + 1) * ML_DH)
        uh = u[:, sl].astype(BF16)
        q_ref[:, sl] = _dot(uh, wq_ref[h]).astype(BF16)
        kt = (_dot_nt(wkt_ref[h], uh) * (ML_DH ** -0.5)).astype(BF16)
        for cidx in range(tt // ML_L):
            kt_ref[cidx, sl, :] = kt[:, cidx * ML_L:(cidx + 1) * ML_L]


def _conv_qk(c3, cw, cb, wq, wkt):
    b, t, w = c3.shape
    tt = CONV_TT
    nb = t // tt
    per = tt // SUBLANES
    last = t // SUBLANES - 1
    tile = pl.BlockSpec((None, tt, w), lambda bi, i: (bi, i, 0))
    prev = pl.BlockSpec((None, SUBLANES, w), lambda bi, i: (bi, jnp.maximum(i * per - 1, 0), 0))
    nxt = pl.BlockSpec((None, SUBLANES, w), lambda bi, i: (bi, jnp.minimum((i + 1) * per, last), 0))
    return pl.pallas_call(
        _conv_qk_kernel,
        grid=(b, nb),
        in_specs=[tile, prev, nxt] + [
            pl.BlockSpec(a.shape, functools.partial(lambda nd, bi, i: (0,) * nd, a.ndim))
            for a in (cw, cb, wq, wkt)],
        out_specs=(tile, pl.BlockSpec((None, tt // ML_L, w, ML_L), lambda bi, i: (bi, i, 0, 0))),
        out_shape=(jax.ShapeDtypeStruct((b, t, w), BF16),
                   jax.ShapeDtypeStruct((b, t // ML_L, w, ML_L), BF16)),
        scratch_shapes=[pltpu.VMEM((tt + 2 * SUBLANES, w), F32)],
        compiler_params=pltpu.CompilerParams(dimension_semantics=("parallel", "parallel")),
        name="conv_qk",
    )(c3, c3, c3, cw, cb, wq, wkt)


def _lane_scan(x, pos, d, combine, ident):
    L = x.shape[-1]
    k = 1
    while k < L:
        shifted = pltpu.roll(x, jnp.where(d == 0, k, L - k), axis=1)
        x = combine(x, jnp.where(pos >= k, shifted, ident))
        k *= 2
    return x


def _mlstm_gates_kernel(ga_ref, gb_ref, rowq_ref, colq_ref, b_s, g_s, mloc_s, rmax_s, vec_s):
    d = pl.program_id(1)
    R = SUBLANES
    L = ML_L
    nc = ga_ref.shape[0] // R
    xa = ga_ref[...]
    xb = gb_ref[...]
    lane = lax.broadcasted_iota(jnp.int32, xa.shape, 1)
    pos = lane + d * (L - 1 - 2 * lane)
    b = _lane_scan(xb, pos, d, jnp.add, 0.0)
    g = jnp.broadcast_to(jnp.sum(xb, axis=1, keepdims=True), xb.shape)
    w = xa - b
    a = g + w
    m_loc = jnp.broadcast_to(jnp.max(a, axis=1, keepdims=True), a.shape)
    rowq_ref[0] = w
    rowq_ref[1] = jnp.exp(a - m_loc)
    b_s[...] = b
    g_s[...] = g
    mloc_s[...] = m_loc
    rmax_s[...] = _lane_scan(w, pos, d, jnp.maximum, NEG)
    vec_s[...] = jnp.zeros_like(vec_s)

    def chunk(k, m0):
        ck = k + d * (nc - 1 - 2 * k)
        rows = pl.ds(pl.multiple_of(ck * R, R), R)
        gk = g_s[rows, :]
        mlk = mloc_s[rows, :]
        mm = jnp.maximum(m0, rmax_s[rows, :])
        m_new = jnp.maximum(gk + m0, mlk)
        rowq_ref[2, rows, :] = jnp.exp(gk + m0 - m_new)
        rowq_ref[3, rows, :] = jnp.exp(mlk - m_new)
        vec_s[ck, 0:R, :] = mm
        vec_s[ck, R:2 * R, :] = jnp.exp(m0 - mm)
        vec_s[ck, 2 * R:3 * R, :] = jnp.exp(-(b_s[rows, :] + mm))
        colq_ref[ck] = vec_s[ck].T
        return m_new

    lax.fori_loop(0, nc, chunk, jnp.zeros((R, L), F32), unroll=8)


def _mlstm_gates(ga, gb, b, t):
    nc = t // ML_L
    R = SUBLANES
    in_spec = pl.BlockSpec((None, nc * R, ML_L), lambda bi, d: (d, bi, 0))
    return pl.pallas_call(
        _mlstm_gates_kernel,
        grid=(b, 2),
        in_specs=[in_spec, in_spec],
        out_specs=(pl.BlockSpec((None, None, 4, nc * R, ML_L), lambda bi, d: (bi, d, 0, 0, 0)),
                   pl.BlockSpec((None, None, nc, ML_L, LANES), lambda bi, d: (bi, d, 0, 0, 0))),
        out_shape=(jax.ShapeDtypeStruct((b, 2, 4, nc * R, ML_L), F32),
                   jax.ShapeDtypeStruct((b, 2, nc, ML_L, LANES), F32)),
        scratch_shapes=[pltpu.VMEM((nc * R, ML_L), F32)] * 4 + [pltpu.VMEM((nc, LANES, ML_L), F32)],
        compiler_params=pltpu.CompilerParams(dimension_semantics=("parallel", "parallel")),
        name="mlstm_gates",
    )(ga, gb)


def _mlstm_kernel(q_ref, kt_ref, v_ref, o_ref, rowq_ref, colq_ref, gn_ref, y_ref,
                  hf_ref, state_ref, *, nblk):
    d = pl.program_id(1)
    i = pl.program_id(2)
    blk = i + d * (nblk - 1 - 2 * i)
    L = ML_L
    nch = ML_TT // L
    H = ML_HEADS
    DH = ML_DH
    R = SUBLANES

    @pl.when(i == 0)
    def _():
        state_ref[...] = jnp.zeros_like(state_ref)

    t_idx = lax.broadcasted_iota(jnp.int32, (L, L), 0)
    s_idx = lax.broadcasted_iota(jnp.int32, (L, L), 1)
    fwd = d == 0
    mask = (t_idx - s_idx) * (1 - 2 * d) >= 0
    ones = jnp.ones((L, DH), BF16)

    for j in range(nch):
        cj = j + d * (nch - 1 - 2 * j)
        off = pl.multiple_of(cj * L, L)
        rows = pl.ds(pl.multiple_of(cj * R, R), R)
        w = rowq_ref[0, rows, :]
        w_end = rowq_ref[1, rows, :]
        s_old = rowq_ref[2, rows, :]
        s_loc = rowq_ref[3, rows, :]
        cols = colq_ref[cj]

        hs = []
        for h in range(H):
            sl = slice(h * DH, (h + 1) * DH)
            q = q_ref[pl.ds(off, L), sl]
            kt = kt_ref[cj, sl, :]
            vaug = jnp.concatenate([v_ref[pl.ds(off, L), sl], ones], axis=1)
            st = state_ref[h]
            mm_col = cols[:, h:h + 1]
            sc_col = cols[:, R + h:R + h + 1]
            em_col = cols[:, 2 * R + h:2 * R + h + 1]

            e = jnp.exp(jnp.where(mask, w[h:h + 1, :] - mm_col, NEG))
            s = (_dot(q, kt) * e).astype(BF16)
            qs = (q.astype(F32) * sc_col).astype(BF16)
            r = _dot(jnp.concatenate([s, qs], axis=1), jnp.concatenate([vaug, st.astype(BF16)], axis=0))
            hs.append(r[:, :DH] / jnp.maximum(jnp.abs(r[:, DH:]), em_col))

            kw = (kt.astype(F32) * w_end[h:h + 1, :]).astype(BF16)
            upd = _dot(kw, vaug)
            so = jnp.broadcast_to(s_old[h:h + 1, :], (DH, L))
            sl_ = jnp.broadcast_to(s_loc[h:h + 1, :], (DH, L))
            state_ref[h] = jnp.concatenate([so, so], axis=1) * st + jnp.concatenate([sl_, sl_], axis=1) * upd

        hrow = pl.multiple_of(blk * ML_TT + cj * L, L)

        @pl.when(fwd)
        def _():
            for h in range(H):
                hf_ref[pl.ds(hrow, L), h * ML_DH:(h + 1) * ML_DH] = hs[h]

        @pl.when(jnp.logical_not(fwd))
        def _():
            for h in range(H):
                sl = slice(h * ML_DH, (h + 1) * ML_DH)
                tot = hf_ref[pl.ds(hrow, L), sl] + hs[h]
                hn = _rms(tot, gn_ref[h])
                y_ref[pl.ds(off, L), sl] = (jax.nn.sigmoid(o_ref[pl.ds(off, L), sl]) * hn).astype(BF16)


def _mlstm(q3, kt4, v3, o3, rowq, colq, gn):
    b, t, w = q3.shape
    tt = ML_TT
    nblk = t // tt
    nch = tt // ML_L
    blk_of = lambda d, i: i + d * (nblk - 1 - 2 * i)
    tile = pl.BlockSpec((None, tt, w), lambda bi, d, i: (bi, blk_of(d, i), 0))
    kt_spec = pl.BlockSpec((None, nch, w, ML_L), lambda bi, d, i: (bi, blk_of(d, i), 0, 0))
    rq_spec = pl.BlockSpec((None, None, 4, nch * SUBLANES, ML_L), lambda bi, d, i: (bi, d, 0, blk_of(d, i), 0))
    cq_spec = pl.BlockSpec((None, None, nch, ML_L, LANES), lambda bi, d, i: (bi, d, blk_of(d, i), 0, 0))
    gn_spec = pl.BlockSpec(gn.shape, lambda bi, d, i: (0, 0, 0))
    y_spec = pl.BlockSpec((None, tt, w), lambda bi, d, i: (bi, d * blk_of(d, i) + (1 - d) * (nblk - 1), 0))
    return pl.pallas_call(
        functools.partial(_mlstm_kernel, nblk=nblk),
        grid=(b, 2, nblk),
        in_specs=[tile, kt_spec, tile, tile, rq_spec, cq_spec, gn_spec],
        out_specs=y_spec,
        out_shape=jax.ShapeDtypeStruct((b, t, w), BF16),
        scratch_shapes=[
            pltpu.VMEM((t, w), F32),
            pltpu.VMEM((ML_HEADS, ML_DH, 2 * ML_DH), F32),
        ],
        compiler_params=pltpu.CompilerParams(
            dimension_semantics=("arbitrary", "arbitrary", "arbitrary"), vmem_limit_bytes=VMEM_LIMIT),
        name="mlstm",
    )(q3, kt4, v3, o3, rowq, colq, gn)


def _na_kernel(q_ref, k_ref, v_ref, bias_ref, out_ref, *, rows):
    r0 = pl.program_id(1) * NA_R
    nkeys = NA_KH * GRID_W
    gw = NA_PACK * NA_DH
    row_blk = lax.broadcasted_iota(jnp.int32, (NA_PACK * GRID_W, gw), 0) // GRID_W
    lane_blk = lax.broadcasted_iota(jnp.int32, (NA_PACK * GRID_W, gw), 1) // NA_DH
    diag = row_blk == lane_blk
    out_blk = lax.broadcasted_iota(jnp.int32, (GRID_W, gw), 1) // NA_DH

    def row_body(ri, carry):
        r = r0 + ri
        rs = jnp.clip(r - NA_KH // 2, 0, rows - NA_KH)
        delta = r - rs
        koff = pl.multiple_of(rs * GRID_W, GRID_W)
        qoff = pl.multiple_of(ri * GRID_W, GRID_W)
        outs = []
        for g in range(NA_HEADS // NA_PACK):
            sl = slice(g * gw, (g + 1) * gw)
            q4 = q_ref[pl.ds(qoff, GRID_W), sl]
            qbd = jnp.where(diag, jnp.concatenate([q4] * NA_PACK, axis=0), jnp.zeros((), BF16))
            s = _dot_nt(qbd, k_ref[pl.ds(koff, nkeys), sl]) + bias_ref[delta, g]
            m = jnp.max(s, axis=-1, keepdims=True)
            p = jnp.exp(s - m)
            l = jnp.sum(p, axis=-1, keepdims=True)
            o = _dot(p.astype(BF16), v_ref[pl.ds(koff, nkeys), sl]) * (1.0 / l)
            og = o[(NA_PACK - 1) * GRID_W:, :]
            for h in range(NA_PACK - 2, -1, -1):
                og = jnp.where(out_blk == h, o[h * GRID_W:(h + 1) * GRID_W, :], og)
            outs.append(og)
        out_ref[pl.ds(qoff, GRID_W), :] = jnp.concatenate(outs, axis=-1).astype(BF16)
        return carry

    lax.fori_loop(0, NA_R, row_body, 0)


def _na(q3, k3, v3, bias):
    b, t, w = q3.shape
    rows = t // GRID_W
    tq = NA_R * GRID_W
    qtile = pl.BlockSpec((None, tq, w), lambda bi, i: (bi, i, 0))
    seq = pl.BlockSpec((None, t, w), lambda bi, i: (bi, 0, 0), pipeline_mode=pl.Buffered(1))
    bias_spec = pl.BlockSpec(bias.shape, lambda bi, i: (0, 0, 0, 0), pipeline_mode=pl.Buffered(1))
    return pl.pallas_call(
        functools.partial(_na_kernel, rows=rows),
        grid=(b, rows // NA_R),
        in_specs=[qtile, seq, seq, bias_spec],
        out_specs=qtile,
        out_shape=jax.ShapeDtypeStruct((b, t, w), BF16),
        compiler_params=pltpu.CompilerParams(
            dimension_semantics=("parallel", "parallel"), vmem_limit_bytes=VMEM_LIMIT),
        name="natten",
    )(q3, k3, v3, bias)


def _na_bias_table(rpb):
    c = np.arange(GRID_W)
    cs = np.clip(c - NA_KW // 2, 0, GRID_W - NA_KW)
    cc = np.arange(GRID_W)
    valid = (cc[None, :] >= cs[:, None]) & (cc[None, :] < cs[:, None] + NA_KW)
    nh, ndr, ndc = rpb.shape
    lead = GRID_W - NA_KW
    w = jnp.pad(rpb.astype(F32), ((0, 0), (0, 0), (lead, 2 * GRID_W - lead - ndc)))
    skew = jnp.broadcast_to(w[:, :, None, :], (nh, ndr, GRID_W, 2 * GRID_W)).reshape(nh, ndr, -1)
    skew = skew[:, :, :GRID_W * (2 * GRID_W - 1)].reshape(nh, ndr, GRID_W, 2 * GRID_W - 1)
    toep = jnp.where(valid[None, None], skew[..., GRID_W - 1:], NEG)
    tabs = [toep[:, NA_KH - 1 - dl:2 * NA_KH - 1 - dl].transpose(0, 2, 1, 3).reshape(nh, GRID_W, -1)
            for dl in range(NA_KH)]
    return jnp.stack(tabs).reshape(NA_KH, NA_HEADS // NA_PACK, NA_PACK * GRID_W, NA_KH * GRID_W)


def _outproj_ffn2_kernel(x1_ref, yml_ref, yna_ref, woml_ref, wona_ref, g2_ref, wg_ref, wu_ref, wd_ref,
                         gf_ref, out_ref):
    x2 = x1_ref[...] + _dot(yml_ref[...], woml_ref[...]) + _dot(yna_ref[...], wona_ref[...])
    h = _rms(x2, g2_ref[...]).astype(BF16)
    x3 = x2 + _swiglu_half(h, wg_ref, wu_ref, wd_ref)
    out_ref[...] = _rms(x3, gf_ref[...])


def _outproj_ffn2(x1, yml, yna, woml, wona, g2, wg, wu, wd, gf):
    n = x1.shape[0]
    tm = FFN_TM
    tok = lambda w: pl.BlockSpec((tm, w), lambda i: (i, 0))
    return pl.pallas_call(
        _outproj_ffn2_kernel,
        grid=(n // tm,),
        in_specs=[tok(D_MODEL), tok(ML_WIDTH), tok(NA_WIDTH)] + [
            _const_spec(a.shape) for a in (woml, wona, g2, wg, wu, wd, gf)],
        out_specs=tok(D_MODEL),
        out_shape=jax.ShapeDtypeStruct((n, D_MODEL), F32),
        compiler_params=pltpu.CompilerParams(
            dimension_semantics=("parallel",), vmem_limit_bytes=VMEM_LIMIT),
        name="outproj_ffn2",
    )(x1, yml, yna, woml, wona, g2, wg, wu, wd, gf)


def _layer(x, norm_ffn1, w1_gate, w1_up, w1_down, norm_mix, w_in, b_gates, conv_w, conv_b, w_q_ml,
           w_k_ml, gn_ml, gq_na, gk_na, rpb, w_out, norm_ffn2, w2_gate, w2_up, w2_down, norm_final):
    b, t, dm = x.shape
    n = b * t
    W = ML_WIDTH
    row = lambda a: a.reshape(1, -1).astype(F32)
    g0 = 3 * W
    wmain = jnp.concatenate([w_in[:, :g0], w_in[:, g0 + N_GATES:]], axis=1).astype(BF16)
    wgate = w_in[:, g0:g0 + N_GATES]
    H = ML_HEADS
    order = np.concatenate([dd * 2 * H + np.r_[0:2 * H, H:2 * H, 0:H] for dd in range(2)])
    wgr = wgate.T[order].astype(BF16)
    bgr = b_gates[order].reshape(-1, 1).astype(F32)
    ones = jnp.asarray(np.kron(np.eye(NA_HEADS), np.ones((NA_DH, NA_DH))), BF16)
    gq = jnp.tile(gq_na.astype(F32), NA_HEADS).reshape(1, -1)
    gk = jnp.tile(gk_na.astype(F32), NA_HEADS).reshape(1, -1)

    x1, c, vml, o, ga, gb, qn, kn, vn = _ffn1_inproj(
        x.reshape(n, dm), row(norm_ffn1), w1_gate.astype(BF16), w1_up.astype(BF16), w1_down.astype(BF16),
        row(norm_mix), wmain, wgr, bgr, ones, gq, gk)

    seq = lambda a: a.reshape(b, t, a.shape[-1])
    q_ml, kt_ml = _conv_qk(seq(c), conv_w.astype(F32), row(conv_b), w_q_ml.astype(BF16),
                           w_k_ml.transpose(0, 2, 1).astype(BF16))
    rowq, colq = _mlstm_gates(ga.reshape(2, -1, ML_L), gb.reshape(2, -1, ML_L), b, t)
    y_ml = _mlstm(q_ml, kt_ml, seq(vml), seq(o), rowq, colq, gn_ml.reshape(ML_HEADS, 1, ML_DH).astype(F32))
    y_na = _na(seq(qn), seq(kn), seq(vn), _na_bias_table(rpb))

    out = _outproj_ffn2(x1, y_ml.reshape(n, W), y_na.reshape(n, NA_WIDTH),
                        w_out[:W].astype(BF16), w_out[W:].astype(BF16), row(norm_ffn2),
                        w2_gate.astype(BF16), w2_up.astype(BF16), w2_down.astype(BF16), row(norm_final))
    return out.reshape(b, t, dm)


def kernel(x, norm_ffn1, w1_gate, w1_up, w1_down, norm_mix, w_in, b_gates, conv_w, conv_b, w_q_ml, w_k_ml,
           gn_ml, gq_na, gk_na, rpb, w_out, norm_ffn2, w2_gate, w2_up, w2_down, norm_final):
    depth = norm_ffn1.shape[0]
    for l in range(depth):
        x = _layer(x, norm_ffn1[l], w1_gate[l], w1_up[l], w1_down[l], norm_mix[l], w_in[l], b_gates[l],
                   conv_w[l], conv_b[l], w_q_ml[l], w_k_ml[l], gn_ml[l], gq_na[l], gk_na[l], rpb[l],
                   w_out[l], norm_ffn2[l], w2_gate[l], w2_up[l], w2_down[l], norm_final[l])
    return x
```

```python
import functools

import jax
import jax.numpy as jnp
import numpy as np
from jax import lax
from jax.experimental import pallas as pl
from jax.experimental.pallas import tpu as pltpu

F32 = jnp.float32
BF16 = jnp.bfloat16

D_MODEL = 1024
D_FF = 2816
GRID_W = 64
ML_HEADS = 4
ML_DH = 128
ML_WIDTH = ML_HEADS * ML_DH
CONV_W = 5
NA_HEADS = 8
NA_DH = 64
NA_WIDTH = NA_HEADS * NA_DH
NA_KH = 8
NA_KW = 16
N_GATES = 4 * ML_HEADS
EPS = 1e-6
NEG = -1e30

SUBLANES = 8
LANES = 128

FFN_TM = 512
FFN_CK = 1408
ML_L = 128
ML_TT = 256
CONV_TT = 512
NA_R = 8
NA_PACK = 4
VMEM_LIMIT = 56 * 1024 * 1024


def _dot(a, b):
    return jnp.dot(a, b, preferred_element_type=F32)


def _dot_nt(a, b):
    return lax.dot_general(a, b, (((1,), (1,)), ((), ())), preferred_element_type=F32)


def _dot_tn(a, b):
    return lax.dot_general(a, b, (((0,), (0,)), ((), ())), preferred_element_type=F32)


def _dot_exact(a, b):
    return jnp.dot(a, b, preferred_element_type=F32, precision=lax.Precision.HIGHEST)


def _rms(x, g):
    ms = jnp.mean(x * x, axis=-1, keepdims=True)
    return x * lax.rsqrt(ms + EPS) * g


def _log_sigmoid(x):
    return jnp.minimum(x, 0.0) - jnp.log1p(jnp.exp(-jnp.abs(x)))


def _swiglu_half(h, wg_ref, wu_ref, wd_ref):
    acc = None
    for j in range(D_FF // FFN_CK):
        sl = slice(j * FFN_CK, (j + 1) * FFN_CK)
        g = _dot(h, wg_ref[:, sl])
        u = _dot(h, wu_ref[:, sl])
        a = (g * jax.nn.sigmoid(g) * u).astype(BF16)
        part = _dot(a, wd_ref[sl, :])
        acc = part if acc is None else acc + part
    return 0.5 * acc


def _group_mean_sq(x, ones_ref):
    xx = x * x
    return _dot(xx.astype(BF16), ones_ref[...]) * (1.0 / NA_DH)


def _ffn1_inproj_kernel(x_ref, g1_ref, wg_ref, wu_ref, wd_ref, gm_ref, wmain_ref,
                        wgr_ref, bgr_ref, ones_ref, gq_ref, gk_ref,
                        x1_ref, c_ref, vml_ref, o_ref, ga_ref, gb_ref, qn_ref, kn_ref, vn_ref):
    x = x_ref[...]
    h = _rms(x, g1_ref[...]).astype(BF16)
    x1 = x + _swiglu_half(h, wg_ref, wu_ref, wd_ref)
    x1_ref[...] = x1
    h2 = _rms(x1, gm_ref[...]).astype(BF16)
    p = _dot(h2, wmain_ref[...])
    W = ML_WIDTH
    c_ref[...] = p[:, 0:W]
    vml_ref[...] = p[:, W:2 * W].astype(BF16)
    o_ref[...] = p[:, 2 * W:3 * W]
    q = p[:, 3 * W:4 * W]
    k = p[:, 4 * W:5 * W]
    vn_ref[...] = p[:, 5 * W:6 * W].astype(BF16)
    qn = q * lax.rsqrt(_group_mean_sq(q, ones_ref) + EPS) * gq_ref[...] * (NA_DH ** -0.5)
    kn = k * lax.rsqrt(_group_mean_sq(k, ones_ref) + EPS) * gk_ref[...]
    qn_ref[...] = qn.astype(BF16)
    kn_ref[...] = kn.astype(BF16)
    gr = _dot_nt(wgr_ref[...], h2) + bgr_ref[...]
    row = lax.broadcasted_iota(jnp.int32, gr.shape, 0) % (4 * ML_HEADS)
    gr = jnp.where((row >= ML_HEADS) & (row < 3 * ML_HEADS), _log_sigmoid(gr), gr)
    for dd in range(2):
        for cidx in range(FFN_TM // ML_L):
            lanes = slice(cidx * ML_L, (cidx + 1) * ML_L)
            ga_ref[dd, cidx] = gr[dd * 16:dd * 16 + 8, lanes]
            gb_ref[dd, cidx] = gr[dd * 16 + 8:dd * 16 + 16, lanes]


def _const_spec(shape):
    nd = len(shape)
    return pl.BlockSpec(shape, lambda *_: (0,) * nd, pipeline_mode=pl.Buffered(1))


def _ffn1_inproj(x2d, g1, wg, wu, wd, gm, wmain, wgr, bgr, ones, gq, gk):
    n = x2d.shape[0]
    tm = FFN_TM
    W = ML_WIDTH
    tok = lambda w: pl.BlockSpec((tm, w), lambda i: (i, 0))
    out_shape = (
        jax.ShapeDtypeStruct((n, D_MODEL), F32),
        jax.ShapeDtypeStruct((n, W), F32),
        jax.ShapeDtypeStruct((n, W), BF16),
        jax.ShapeDtypeStruct((n, W), F32),
        jax.ShapeDtypeStruct((2, n // ML_L, SUBLANES, ML_L), F32),
        jax.ShapeDtypeStruct((2, n // ML_L, SUBLANES, ML_L), F32),
        jax.ShapeDtypeStruct((n, W), BF16),
        jax.ShapeDtypeStruct((n, W), BF16),
        jax.ShapeDtypeStruct((n, W), BF16),
    )
    out_specs = (
        tok(D_MODEL), tok(W), tok(W), tok(W),
        pl.BlockSpec((2, tm // ML_L, SUBLANES, ML_L), lambda i: (0, i, 0, 0)),
        pl.BlockSpec((2, tm // ML_L, SUBLANES, ML_L), lambda i: (0, i, 0, 0)),
        tok(W), tok(W), tok(W),
    )
    in_specs = [tok(D_MODEL)] + [_const_spec(a.shape) for a in
                                 (g1, wg, wu, wd, gm, wmain, wgr, bgr, ones, gq, gk)]
    return pl.pallas_call(
        _ffn1_inproj_kernel,
        grid=(n // tm,),
        in_specs=in_specs,
        out_specs=out_specs,
        out_shape=out_shape,
        compiler_params=pltpu.CompilerParams(
            dimension_semantics=("parallel",), vmem_limit_bytes=VMEM_LIMIT),
        name="ffn1_inproj",
    )(x2d, g1, wg, wu, wd, gm, wmain, wgr, bgr, ones, gq, gk)


def _conv_qk_kernel(c_ref, prev_ref, next_ref, cw_ref, cb_ref, wq_ref, wkt_ref, q_ref, kt_ref, pad_ref):
    i = pl.program_id(1)
    nb = pl.num_programs(1)
    tt = CONV_TT
    half = CONV_W // 2
    pad_ref[0:SUBLANES, :] = jnp.where(i > 0, prev_ref[...], 0.0)
    pad_ref[SUBLANES:SUBLANES + tt, :] = c_ref[...]
    pad_ref[SUBLANES + tt:2 * SUBLANES + tt, :] = jnp.where(i < nb - 1, next_ref[...], 0.0)
    y = cb_ref[...]
    for kk in range(CONV_W):
        start = SUBLANES + kk - half
        y = y + pad_ref[start:start + tt, :] * cw_ref[kk:kk + 1, :]
    u = y * jax.nn.sigmoid(y)
    for h in range(ML_HEADS):
        sl = slice(h * ML_DH, (h + 1) * ML_DH)
        uh = u[:, sl].astype(BF16)
        q_ref[:, sl] = _dot(uh, wq_ref[h]).astype(BF16)
        kt = (_dot_nt(wkt_ref[h], uh) * (ML_DH ** -0.5)).astype(BF16)
        for cidx in range(tt // ML_L):
            kt_ref[cidx, sl, :] = kt[:, cidx * ML_L:(cidx + 1) * ML_L]


def _conv_qk(c3, cw, cb, wq, wkt):
    b, t, w = c3.shape
    tt = CONV_TT
    nb = t // tt
    per = tt // SUBLANES
    last = t // SUBLANES - 1
    tile = pl.BlockSpec((None, tt, w), lambda bi, i: (bi, i, 0))
    prev = pl.BlockSpec((None, SUBLANES, w), lambda bi, i: (bi, jnp.maximum(i * per - 1, 0), 0))
    nxt = pl.BlockSpec((None, SUBLANES, w), lambda bi, i: (bi, jnp.minimum((i + 1) * per, last), 0))
    return pl.pallas_call(
        _conv_qk_kernel,
        grid=(b, nb),
        in_specs=[tile, prev, nxt] + [
            pl.BlockSpec(a.shape, functools.partial(lambda nd, bi, i: (0,) * nd, a.ndim))
            for a in (cw, cb, wq, wkt)],
        out_specs=(tile, pl.BlockSpec((None, tt // ML_L, w, ML_L), lambda bi, i: (bi, i, 0, 0))),
        out_shape=(jax.ShapeDtypeStruct((b, t, w), BF16),
                   jax.ShapeDtypeStruct((b, t // ML_L, w, ML_L), BF16)),
        scratch_shapes=[pltpu.VMEM((tt + 2 * SUBLANES, w), F32)],
        compiler_params=pltpu.CompilerParams(dimension_semantics=("parallel", "parallel")),
        name="conv_qk",
    )(c3, c3, c3, cw, cb, wq, wkt)


def _lane_scan(x, pos, d, combine, ident):
    L = x.shape[-1]
    k = 1
    while k < L:
        shifted = pltpu.roll(x, jnp.where(d == 0, k, L - k), axis=1)
        x = combine(x, jnp.where(pos >= k, shifted, ident))
        k *= 2
    return x


def _mlstm_gates_kernel(ga_ref, gb_ref, rowq_ref, colq_ref, b_s, g_s, mloc_s, rmax_s, vec_s):
    d = pl.program_id(1)
    R = SUBLANES
    L = ML_L
    nc = ga_ref.shape[0] // R
    xa = ga_ref[...]
    xb = gb_ref[...]
    lane = lax.broadcasted_iota(jnp.int32, xa.shape, 1)
    pos = lane + d * (L - 1 - 2 * lane)
    b = _lane_scan(xb, pos, d, jnp.add, 0.0)
    g = jnp.broadcast_to(jnp.sum(xb, axis=1, keepdims=True), xb.shape)
    w = xa - b
    a = g + w
    m_loc = jnp.broadcast_to(jnp.max(a, axis=1, keepdims=True), a.shape)
    rowq_ref[0] = w
    rowq_ref[1] = jnp.exp(a - m_loc)
    b_s[...] = b
    g_s[...] = g
    mloc_s[...] = m_loc
    rmax_s[...] = _lane_scan(w, pos, d, jnp.maximum, NEG)
    vec_s[...] = jnp.zeros_like(vec_s)

    def chunk(k, m0):
        ck = k + d * (nc - 1 - 2 * k)
        rows = pl.ds(pl.multiple_of(ck * R, R), R)
        gk = g_s[rows, :]
        mlk = mloc_s[rows, :]
        mm = jnp.maximum(m0, rmax_s[rows, :])
        m_new = jnp.maximum(gk + m0, mlk)
        rowq_ref[2, rows, :] = jnp.exp(gk + m0 - m_new)
        rowq_ref[3, rows, :] = jnp.exp(mlk - m_new)
        vec_s[ck, 0:R, :] = mm
        vec_s[ck, R:2 * R, :] = jnp.exp(m0 - mm)
        vec_s[ck, 2 * R:3 * R, :] = jnp.exp(-(b_s[rows, :] + mm))
        colq_ref[ck] = vec_s[ck].T
        return m_new

    lax.fori_loop(0, nc, chunk, jnp.zeros((R, L), F32), unroll=8)


def _mlstm_gates(ga, gb, b, t):
    nc = t // ML_L
    R = SUBLANES
    in_spec = pl.BlockSpec((None, nc * R, ML_L), lambda bi, d: (d, bi, 0))
    return pl.pallas_call(
        _mlstm_gates_kernel,
        grid=(b, 2),
        in_specs=[in_spec, in_spec],
        out_specs=(pl.BlockSpec((None, None, 4, nc * R, ML_L), lambda bi, d: (bi, d, 0, 0, 0)),
                   pl.BlockSpec((None, None, nc, ML_L, LANES), lambda bi, d: (bi, d, 0, 0, 0))),
        out_shape=(jax.ShapeDtypeStruct((b, 2, 4, nc * R, ML_L), F32),
                   jax.ShapeDtypeStruct((b, 2, nc, ML_L, LANES), F32)),
        scratch_shapes=[pltpu.VMEM((nc * R, ML_L), F32)] * 4 + [pltpu.VMEM((nc, LANES, ML_L), F32)],
        compiler_params=pltpu.CompilerParams(dimension_semantics=("parallel", "parallel")),
        name="mlstm_gates",
    )(ga, gb)


def _mlstm_kernel(q_ref, kt_ref, v_ref, o_ref, rowq_ref, colq_ref, gn_ref, y_ref,
                  hf_ref, state_ref, *, nblk):
    d = pl.program_id(1)
    i = pl.program_id(2)
    blk = i + d * (nblk - 1 - 2 * i)
    L = ML_L
    nch = ML_TT // L
    H = ML_HEADS
    DH = ML_DH
    R = SUBLANES

    @pl.when(i == 0)
    def _():
        state_ref[...] = jnp.zeros_like(state_ref)

    t_idx = lax.broadcasted_iota(jnp.int32, (L, L), 0)
    s_idx = lax.broadcasted_iota(jnp.int32, (L, L), 1)
    fwd = d == 0
    mask = (t_idx - s_idx) * (1 - 2 * d) >= 0
    ones = jnp.ones((L, DH), BF16)
    t_total = hf_ref.shape[0] - ML_TT
    hbase = pl.multiple_of(jnp.where(fwd, blk * ML_TT, t_total), ML_TT)

    for j in range(nch):
        cj = j + d * (nch - 1 - 2 * j)
        off = pl.multiple_of(cj * L, L)
        rows = pl.ds(pl.multiple_of(cj * R, R), R)
        w = rowq_ref[0, rows, :]
        w_end = rowq_ref[1, rows, :]
        s_old = rowq_ref[2, rows, :]
        s_loc = rowq_ref[3, rows, :]
        cols = colq_ref[cj]

        for h in range(H):
            sl = slice(h * DH, (h + 1) * DH)
            q = q_ref[pl.ds(off, L), sl]
            kt = kt_ref[cj, sl, :]
            vaug = jnp.concatenate([v_ref[pl.ds(off, L), sl], ones], axis=1)
            st = state_ref[h]
            mm_col = cols[:, h:h + 1]
            sc_col = cols[:, R + h:R + h + 1]
            em_col = cols[:, 2 * R + h:2 * R + h + 1]

            e = jnp.exp(jnp.where(mask, w[h:h + 1, :] - mm_col, NEG))
            s = (_dot(q, kt) * e).astype(BF16)
            qs = (q.astype(F32) * sc_col).astype(BF16)
            r = _dot(jnp.concatenate([s, qs], axis=1), jnp.concatenate([vaug, st.astype(BF16)], axis=0))
            hf_ref[pl.ds(hbase + off, L), sl] = r[:, :DH] / jnp.maximum(jnp.abs(r[:, DH:]), em_col)

            kw = (kt.astype(F32) * w_end[h:h + 1, :]).astype(BF16)
            upd = _dot(kw, vaug)
            so = jnp.broadcast_to(s_old[h:h + 1, :], (DH, L))
            sl_ = jnp.broadcast_to(s_loc[h:h + 1, :], (DH, L))
            state_ref[h] = jnp.concatenate([so, so], axis=1) * st + jnp.concatenate([sl_, sl_], axis=1) * upd

    @pl.when(jnp.logical_not(fwd))
    def _():
        frow = pl.multiple_of(blk * ML_TT, ML_TT)
        for h in range(H):
            sl = slice(h * DH, (h + 1) * DH)
            tot = hf_ref[pl.ds(frow, ML_TT), sl] + hf_ref[pl.ds(t_total, ML_TT), sl]
            hn = _rms(tot, gn_ref[h])
            y_ref[:, sl] = (jax.nn.sigmoid(o_ref[:, sl]) * hn).astype(BF16)


def _mlstm(q3, kt4, v3, o3, rowq, colq, gn):
    b, t, w = q3.shape
    tt = ML_TT
    nblk = t // tt
    nch = tt // ML_L
    blk_of = lambda d, i: i + d * (nblk - 1 - 2 * i)
    tile = pl.BlockSpec((None, tt, w), lambda bi, d, i: (bi, blk_of(d, i), 0))
    kt_spec = pl.BlockSpec((None, nch, w, ML_L), lambda bi, d, i: (bi, blk_of(d, i), 0, 0))
    rq_spec = pl.BlockSpec((None, None, 4, nch * SUBLANES, ML_L), lambda bi, d, i: (bi, d, 0, blk_of(d, i), 0))
    cq_spec = pl.BlockSpec((None, None, nch, ML_L, LANES), lambda bi, d, i: (bi, d, blk_of(d, i), 0, 0))
    gn_spec = pl.BlockSpec(gn.shape, lambda bi, d, i: (0, 0, 0))
    y_spec = pl.BlockSpec((None, tt, w), lambda bi, d, i: (bi, d * blk_of(d, i) + (1 - d) * (nblk - 1), 0))
    return pl.pallas_call(
        functools.partial(_mlstm_kernel, nblk=nblk),
        grid=(b, 2, nblk),
        in_specs=[tile, kt_spec, tile, tile, rq_spec, cq_spec, gn_spec],
        out_specs=y_spec,
        out_shape=jax.ShapeDtypeStruct((b, t, w), BF16),
        scratch_shapes=[
            pltpu.VMEM((t + tt, w), F32),
            pltpu.VMEM((ML_HEADS, ML_DH, 2 * ML_DH), F32),
        ],
        compiler_params=pltpu.CompilerParams(
            dimension_semantics=("arbitrary", "arbitrary", "arbitrary"), vmem_limit_bytes=VMEM_LIMIT),
        name="mlstm",
    )(q3, kt4, v3, o3, rowq, colq, gn)


def _na_kernel(q_ref, k_ref, v_ref, bias_ref, out_ref, *, rows):
    r0 = pl.program_id(1) * NA_R
    nkeys = NA_KH * GRID_W
    gw = NA_PACK * NA_DH
    row_blk = lax.broadcasted_iota(jnp.int32, (NA_PACK * GRID_W, gw), 0) // GRID_W
    lane_blk = lax.broadcasted_iota(jnp.int32, (NA_PACK * GRID_W, gw), 1) // NA_DH
    diag = row_blk == lane_blk
    out_blk = lax.broadcasted_iota(jnp.int32, (GRID_W, gw), 1) // NA_DH

    def row_body(ri, carry):
        r = r0 + ri
        rs = jnp.clip(r - NA_KH // 2, 0, rows - NA_KH)
        delta = r - rs
        koff = pl.multiple_of(rs * GRID_W, GRID_W)
        qoff = pl.multiple_of(ri * GRID_W, GRID_W)
        outs = []
        for g in range(NA_HEADS // NA_PACK):
            sl = slice(g * gw, (g + 1) * gw)
            q4 = q_ref[pl.ds(qoff, GRID_W), sl]
            qbd = jnp.where(diag, jnp.concatenate([q4] * NA_PACK, axis=0), jnp.zeros((), BF16))
            s = _dot_nt(qbd, k_ref[pl.ds(koff, nkeys), sl]) + bias_ref[delta, g]
            m = jnp.max(s, axis=-1, keepdims=True)
            p = jnp.exp(s - m)
            l = jnp.sum(p, axis=-1, keepdims=True)
            o = _dot(p.astype(BF16), v_ref[pl.ds(koff, nkeys), sl]) * (1.0 / l)
            og = o[(NA_PACK - 1) * GRID_W:, :]
            for h in range(NA_PACK - 2, -1, -1):
                og = jnp.where(out_blk == h, o[h * GRID_W:(h + 1) * GRID_W, :], og)
            outs.append(og)
        out_ref[pl.ds(qoff, GRID_W), :] = jnp.concatenate(outs, axis=-1).astype(BF16)
        return carry

    lax.fori_loop(0, NA_R, row_body, 0, unroll=2)


def _na(q3, k3, v3, bias):
    b, t, w = q3.shape
    rows = t // GRID_W
    tq = NA_R * GRID_W
    qtile = pl.BlockSpec((None, tq, w), lambda bi, i: (bi, i, 0))
    seq = pl.BlockSpec((None, t, w), lambda bi, i: (bi, 0, 0), pipeline_mode=pl.Buffered(1))
    bias_spec = pl.BlockSpec(bias.shape, lambda bi, i: (0, 0, 0, 0), pipeline_mode=pl.Buffered(1))
    return pl.pallas_call(
        functools.partial(_na_kernel, rows=rows),
        grid=(b, rows // NA_R),
        in_specs=[qtile, seq, seq, bias_spec],
        out_specs=qtile,
        out_shape=jax.ShapeDtypeStruct((b, t, w), BF16),
        compiler_params=pltpu.CompilerParams(
            dimension_semantics=("parallel", "parallel"), vmem_limit_bytes=VMEM_LIMIT),
        name="natten",
    )(q3, k3, v3, bias)


def _na_bias_table(rpb):
    c = np.arange(GRID_W)
    cs = np.clip(c - NA_KW // 2, 0, GRID_W - NA_KW)
    cc = np.arange(GRID_W)
    valid = (cc[None, :] >= cs[:, None]) & (cc[None, :] < cs[:, None] + NA_KW)
    nh, ndr, ndc = rpb.shape
    lead = GRID_W - NA_KW
    w = jnp.pad(rpb.astype(F32), ((0, 0), (0, 0), (lead, 2 * GRID_W - lead - ndc)))
    skew = jnp.broadcast_to(w[:, :, None, :], (nh, ndr, GRID_W, 2 * GRID_W)).reshape(nh, ndr, -1)
    skew = skew[:, :, :GRID_W * (2 * GRID_W - 1)].reshape(nh, ndr, GRID_W, 2 * GRID_W - 1)
    toep = jnp.where(valid[None, None], skew[..., GRID_W - 1:], NEG).transpose(0, 2, 1, 3)
    tabs = [toep[:, :, NA_KH - 1 - dl:2 * NA_KH - 1 - dl].reshape(nh, GRID_W, -1)
            for dl in range(NA_KH)]
    return jnp.stack(tabs).reshape(NA_KH, NA_HEADS // NA_PACK, NA_PACK * GRID_W, NA_KH * GRID_W)


def _outproj_ffn2_kernel(x1_ref, yml_ref, yna_ref, woml_ref, wona_ref, g2_ref, wg_ref, wu_ref, wd_ref,
                         gf_ref, out_ref):
    x2 = x1_ref[...] + _dot(yml_ref[...], woml_ref[...]) + _dot(yna_ref[...], wona_ref[...])
    h = _rms(x2, g2_ref[...]).astype(BF16)
    x3 = x2 + _swiglu_half(h, wg_ref, wu_ref, wd_ref)
    out_ref[...] = _rms(x3, gf_ref[...])


def _outproj_ffn2(x1, yml, yna, woml, wona, g2, wg, wu, wd, gf):
    n = x1.shape[0]
    tm = FFN_TM
    tok = lambda w: pl.BlockSpec((tm, w), lambda i: (i, 0))
    return pl.pallas_call(
        _outproj_ffn2_kernel,
        grid=(n // tm,),
        in_specs=[tok(D_MODEL), tok(ML_WIDTH), tok(NA_WIDTH)] + [
            _const_spec(a.shape) for a in (woml, wona, g2, wg, wu, wd, gf)],
        out_specs=tok(D_MODEL),
        out_shape=jax.ShapeDtypeStruct((n, D_MODEL), F32),
        compiler_params=pltpu.CompilerParams(
            dimension_semantics=("parallel",), vmem_limit_bytes=VMEM_LIMIT),
        name="outproj_ffn2",
    )(x1, yml, yna, woml, wona, g2, wg, wu, wd, gf)


def _layer(x, norm_ffn1, w1_gate, w1_up, w1_down, norm_mix, w_in, b_gates, conv_w, conv_b, w_q_ml,
           w_k_ml, gn_ml, gq_na, gk_na, rpb, w_out, norm_ffn2, w2_gate, w2_up, w2_down, norm_final):
    b, t, dm = x.shape
    n = b * t
    W = ML_WIDTH
    row = lambda a: a.reshape(1, -1).astype(F32)
    g0 = 3 * W
    wmain = jnp.concatenate([w_in[:, :g0], w_in[:, g0 + N_GATES:]], axis=1).astype(BF16)
    wgate = w_in[:, g0:g0 + N_GATES]
    H = ML_HEADS
    order = np.concatenate([dd * 2 * H + np.r_[0:2 * H, H:2 * H, 0:H] for dd in range(2)])
    wgr = wgate.T[order].astype(BF16)
    bgr = b_gates[order].reshape(-1, 1).astype(F32)
    ones = jnp.asarray(np.kron(np.eye(NA_HEADS), np.ones((NA_DH, NA_DH))), BF16)
    gq = jnp.tile(gq_na.astype(F32), NA_HEADS).reshape(1, -1)
    gk = jnp.tile(gk_na.astype(F32), NA_HEADS).reshape(1, -1)

    x1, c, vml, o, ga, gb, qn, kn, vn = _ffn1_inproj(
        x.reshape(n, dm), row(norm_ffn1), w1_gate.astype(BF16), w1_up.astype(BF16), w1_down.astype(BF16),
        row(norm_mix), wmain, wgr, bgr, ones, gq, gk)

    seq = lambda a: a.reshape(b, t, a.shape[-1])
    q_ml, kt_ml = _conv_qk(seq(c), conv_w.astype(F32), row(conv_b), w_q_ml.astype(BF16),
                           w_k_ml.transpose(0, 2, 1).astype(BF16))
    rowq, colq = _mlstm_gates(ga.reshape(2, -1, ML_L), gb.reshape(2, -1, ML_L), b, t)
    y_ml = _mlstm(q_ml, kt_ml, seq(vml), seq(o), rowq, colq, gn_ml.reshape(ML_HEADS, 1, ML_DH).astype(F32))
    y_na = _na(seq(qn), seq(kn), seq(vn), _na_bias_table(rpb))

    out = _outproj_ffn2(x1, y_ml.reshape(n, W), y_na.reshape(n, NA_WIDTH),
                        w_out[:W].astype(BF16), w_out[W:].astype(BF16), row(norm_ffn2),
                        w2_gate.astype(BF16), w2_up.astype(BF16), w2_down.astype(BF16), row(norm_final))
    return out.reshape(b, t, dm)


def kernel(x, norm_ffn1, w1_gate, w1_up, w1_down, norm_mix, w_in, b_gates, conv_w, conv_b, w_q_ml, w_k_ml,
           gn_ml, gq_na, gk_na, rpb, w_out, norm_ffn2, w2_gate, w2_up, w2_down, norm_final):
    depth = norm_ffn1.shape[0]
    for l in range(depth):
        x = _layer(x, norm_ffn1[l], w1_gate[l], w1_up[l], w1_down[l], norm_mix[l], w_in[l], b_gates[l],
                   conv_w[l], conv_b[l], w_q_ml[l], w_k_ml[l], gn_ml[l], gq_na[l], gk_na[l], rpb[l],
                   w_out[l], norm_ffn2[l], w2_gate[l], w2_up[l], w2_down[l], norm_final[l])
    return x
```

```python
import functools

import jax
import jax.numpy as jnp
import numpy as np
from jax import lax
from jax.experimental import pallas as pl
from jax.experimental.pallas import tpu as pltpu

F32 = jnp.float32
BF16 = jnp.bfloat16

D_MODEL = 1024
D_FF = 2816
GRID_W = 64
ML_HEADS = 4
ML_DH = 128
ML_WIDTH = ML_HEADS * ML_DH
CONV_W = 5
NA_HEADS = 8
NA_DH = 64
NA_WIDTH = NA_HEADS * NA_DH
NA_KH = 8
NA_KW = 16
N_GATES = 4 * ML_HEADS
EPS = 1e-6
NEG = -1e30

SUBLANES = 8
LANES = 128

FFN_TM = 512
FFN_CK = 1408
ML_L = 128
ML_TT = 256
ML_NROWQ = 6
ML_NONES = 16
CONV_TT = 512
NA_R = 8
NA_PACK = 4
VMEM_LIMIT = 56 * 1024 * 1024


def _dot(a, b):
    return jnp.dot(a, b, preferred_element_type=F32)


def _dot_nt(a, b):
    return lax.dot_general(a, b, (((1,), (1,)), ((), ())), preferred_element_type=F32)


def _dot_tn(a, b):
    return lax.dot_general(a, b, (((0,), (0,)), ((), ())), preferred_element_type=F32)


def _dot_exact(a, b):
    return jnp.dot(a, b, preferred_element_type=F32, precision=lax.Precision.HIGHEST)


def _rms(x, g):
    ms = jnp.mean(x * x, axis=-1, keepdims=True)
    return x * lax.rsqrt(ms + EPS) * g


def _log_sigmoid(x):
    return jnp.minimum(x, 0.0) - jnp.log1p(jnp.exp(-jnp.abs(x)))


def _swiglu_half(h, wg_ref, wu_ref, wd_ref):
    acc = None
    for j in range(D_FF // FFN_CK):
        sl = slice(j * FFN_CK, (j + 1) * FFN_CK)
        g = _dot(h, wg_ref[:, sl])
        u = _dot(h, wu_ref[:, sl])
        a = (g * jax.nn.sigmoid(g) * u).astype(BF16)
        part = _dot(a, wd_ref[sl, :])
        acc = part if acc is None else acc + part
    return 0.5 * acc


def _group_mean_sq(x, ones_ref):
    xx = x * x
    return _dot(xx.astype(BF16), ones_ref[...]) * (1.0 / NA_DH)


def _ffn1_inproj_kernel(x_ref, g1_ref, wg_ref, wu_ref, wd_ref, gm_ref, wc_ref, wvt_ref, wo_ref, wna_ref,
                        wgr_ref, bgr_ref, ones_ref, gq_ref, gk_ref,
                        x1_ref, c_ref, vt_ref, o_ref, ga_ref, gb_ref, qn_ref, kn_ref, vn_ref):
    x = x_ref[...]
    h = _rms(x, g1_ref[...]).astype(BF16)
    x1 = x + _swiglu_half(h, wg_ref, wu_ref, wd_ref)
    x1_ref[...] = x1
    h2 = _rms(x1, gm_ref[...]).astype(BF16)
    W = ML_WIDTH
    c_ref[...] = _dot(h2, wc_ref[...])
    o_ref[...] = _dot(h2, wo_ref[...])
    vt = _dot_nt(wvt_ref[...], h2).astype(BF16)
    for cidx in range(FFN_TM // ML_L):
        vt_ref[cidx] = vt[:, cidx * ML_L:(cidx + 1) * ML_L]
    p = _dot(h2, wna_ref[...])
    q = p[:, 0:W]
    k = p[:, W:2 * W]
    vn_ref[...] = p[:, 2 * W:3 * W].astype(BF16)
    qn = q * lax.rsqrt(_group_mean_sq(q, ones_ref) + EPS) * gq_ref[...] * (NA_DH ** -0.5)
    kn = k * lax.rsqrt(_group_mean_sq(k, ones_ref) + EPS) * gk_ref[...]
    qn_ref[...] = qn.astype(BF16)
    kn_ref[...] = kn.astype(BF16)
    gr = _dot_nt(wgr_ref[...], h2) + bgr_ref[...]
    row = lax.broadcasted_iota(jnp.int32, gr.shape, 0) % (4 * ML_HEADS)
    gr = jnp.where((row >= ML_HEADS) & (row < 3 * ML_HEADS), _log_sigmoid(gr), gr)
    for dd in range(2):
        for cidx in range(FFN_TM // ML_L):
            lanes = slice(cidx * ML_L, (cidx + 1) * ML_L)
            ga_ref[dd, cidx] = gr[dd * 16:dd * 16 + 8, lanes]
            gb_ref[dd, cidx] = gr[dd * 16 + 8:dd * 16 + 16, lanes]


def _const_spec(shape):
    nd = len(shape)
    return pl.BlockSpec(shape, lambda *_: (0,) * nd, pipeline_mode=pl.Buffered(1))


def _ffn1_inproj(x2d, g1, wg, wu, wd, gm, wc, wvt, wo, wna, wgr, bgr, ones, gq, gk):
    n = x2d.shape[0]
    tm = FFN_TM
    W = ML_WIDTH
    tok = lambda w: pl.BlockSpec((tm, w), lambda i: (i, 0))
    out_shape = (
        jax.ShapeDtypeStruct((n, D_MODEL), F32),
        jax.ShapeDtypeStruct((n, W), F32),
        jax.ShapeDtypeStruct((n // ML_L, W, ML_L), BF16),
        jax.ShapeDtypeStruct((n, W), F32),
        jax.ShapeDtypeStruct((2, n // ML_L, SUBLANES, ML_L), F32),
        jax.ShapeDtypeStruct((2, n // ML_L, SUBLANES, ML_L), F32),
        jax.ShapeDtypeStruct((n, W), BF16),
        jax.ShapeDtypeStruct((n, W), BF16),
        jax.ShapeDtypeStruct((n, W), BF16),
    )
    out_specs = (
        tok(D_MODEL), tok(W), pl.BlockSpec((tm // ML_L, W, ML_L), lambda i: (i, 0, 0)), tok(W),
        pl.BlockSpec((2, tm // ML_L, SUBLANES, ML_L), lambda i: (0, i, 0, 0)),
        pl.BlockSpec((2, tm // ML_L, SUBLANES, ML_L), lambda i: (0, i, 0, 0)),
        tok(W), tok(W), tok(W),
    )
    in_specs = [tok(D_MODEL)] + [_const_spec(a.shape) for a in
                                 (g1, wg, wu, wd, gm, wc, wvt, wo, wna, wgr, bgr, ones, gq, gk)]
    return pl.pallas_call(
        _ffn1_inproj_kernel,
        grid=(n // tm,),
        in_specs=in_specs,
        out_specs=out_specs,
        out_shape=out_shape,
        compiler_params=pltpu.CompilerParams(
            dimension_semantics=("parallel",), vmem_limit_bytes=VMEM_LIMIT),
        name="ffn1_inproj",
    )(x2d, g1, wg, wu, wd, gm, wc, wvt, wo, wna, wgr, bgr, ones, gq, gk)


def _conv_qk_kernel(c_ref, prev_ref, next_ref, cw_ref, cb_ref, wqt_ref, wk_ref, qt_ref, k_ref, pad_ref):
    i = pl.program_id(1)
    nb = pl.num_programs(1)
    tt = CONV_TT
    half = CONV_W // 2
    pad_ref[0:SUBLANES, :] = jnp.where(i > 0, prev_ref[...], 0.0)
    pad_ref[SUBLANES:SUBLANES + tt, :] = c_ref[...]
    pad_ref[SUBLANES + tt:2 * SUBLANES + tt, :] = jnp.where(i < nb - 1, next_ref[...], 0.0)
    y = cb_ref[...]
    for kk in range(CONV_W):
        start = SUBLANES + kk - half
        y = y + pad_ref[start:start + tt, :] * cw_ref[kk:kk + 1, :]
    u = y * jax.nn.sigmoid(y)
    for h in range(ML_HEADS):
        sl = slice(h * ML_DH, (h + 1) * ML_DH)
        uh = u[:, sl].astype(BF16)
        k_ref[:, sl] = (_dot(uh, wk_ref[h]) * (ML_DH ** -0.5)).astype(BF16)
        qt = _dot_nt(wqt_ref[h], uh).astype(BF16)
        for cidx in range(tt // ML_L):
            qt_ref[cidx, sl, :] = qt[:, cidx * ML_L:(cidx + 1) * ML_L]


def _conv_qk(c3, cw, cb, wqt, wk):
    b, t, w = c3.shape
    tt = CONV_TT
    nb = t // tt
    per = tt // SUBLANES
    last = t // SUBLANES - 1
    tile = pl.BlockSpec((None, tt, w), lambda bi, i: (bi, i, 0))
    prev = pl.BlockSpec((None, SUBLANES, w), lambda bi, i: (bi, jnp.maximum(i * per - 1, 0), 0))
    nxt = pl.BlockSpec((None, SUBLANES, w), lambda bi, i: (bi, jnp.minimum((i + 1) * per, last), 0))
    return pl.pallas_call(
        _conv_qk_kernel,
        grid=(b, nb),
        in_specs=[tile, prev, nxt] + [
            pl.BlockSpec(a.shape, functools.partial(lambda nd, bi, i: (0,) * nd, a.ndim))
            for a in (cw, cb, wqt, wk)],
        out_specs=(pl.BlockSpec((None, tt // ML_L, w, ML_L), lambda bi, i: (bi, i, 0, 0)), tile),
        out_shape=(jax.ShapeDtypeStruct((b, t // ML_L, w, ML_L), BF16),
                   jax.ShapeDtypeStruct((b, t, w), BF16)),
        scratch_shapes=[pltpu.VMEM((tt + 2 * SUBLANES, w), F32)],
        compiler_params=pltpu.CompilerParams(dimension_semantics=("parallel", "parallel")),
        name="conv_qk",
    )(c3, c3, c3, cw, cb, wqt, wk)


def _lane_scan(x, pos, d, combine, ident):
    L = x.shape[-1]
    k = 1
    while k < L:
        shifted = pltpu.roll(x, jnp.where(d == 0, k, L - k), axis=1)
        x = combine(x, jnp.where(pos >= k, shifted, ident))
        k *= 2
    return x


def _mlstm_gates_kernel(ga_ref, gb_ref, rowq_ref, colq_ref, b_s, g_s, mloc_s, rmax_s, w_s, vec_s):
    d = pl.program_id(1)
    R = SUBLANES
    L = ML_L
    nc = ga_ref.shape[0] // R
    xa = ga_ref[...]
    xb = gb_ref[...]
    lane = lax.broadcasted_iota(jnp.int32, xa.shape, 1)
    pos = lane + d * (L - 1 - 2 * lane)
    b = _lane_scan(xb, pos, d, jnp.add, 0.0)
    g = jnp.broadcast_to(jnp.sum(xb, axis=1, keepdims=True), xb.shape)
    w = xa - b
    a = g + w
    m_loc = jnp.broadcast_to(jnp.max(a, axis=1, keepdims=True), a.shape)
    rowq_ref[3] = jnp.exp(a - m_loc)
    b_s[...] = b
    g_s[...] = g
    mloc_s[...] = m_loc
    rmax_s[...] = _lane_scan(w, pos, d, jnp.maximum, NEG)
    w_s[...] = w
    vec_s[...] = jnp.zeros_like(vec_s)

    def chunk(k, m0):
        ck = k + d * (nc - 1 - 2 * k)
        rows = pl.ds(pl.multiple_of(ck * R, R), R)
        gk = g_s[rows, :]
        mlk = mloc_s[rows, :]
        mm = jnp.maximum(m0, rmax_s[rows, :])
        m_new = jnp.maximum(gk + m0, mlk)
        rowq_ref[0, rows, :] = mm
        rowq_ref[1, rows, :] = jnp.exp(m0 - mm)
        rowq_ref[2, rows, :] = jnp.exp(-(b_s[rows, :] + mm))
        rowq_ref[4, rows, :] = jnp.exp(gk + m0 - m_new)
        rowq_ref[5, rows, :] = jnp.exp(mlk - m_new)
        vec_s[ck, 0:R, :] = w_s[rows, :]
        colq_ref[ck] = vec_s[ck].T
        return m_new

    lax.fori_loop(0, nc, chunk, jnp.zeros((R, L), F32), unroll=8)


def _mlstm_gates(ga, gb, b, t):
    nc = t // ML_L
    R = SUBLANES
    in_spec = pl.BlockSpec((None, nc * R, ML_L), lambda bi, d: (d, bi, 0))
    return pl.pallas_call(
        _mlstm_gates_kernel,
        grid=(b, 2),
        in_specs=[in_spec, in_spec],
        out_specs=(pl.BlockSpec((None, None, ML_NROWQ, nc * R, ML_L), lambda bi, d: (bi, d, 0, 0, 0)),
                   pl.BlockSpec((None, None, nc, ML_L, LANES), lambda bi, d: (bi, d, 0, 0, 0))),
        out_shape=(jax.ShapeDtypeStruct((b, 2, ML_NROWQ, nc * R, ML_L), F32),
                   jax.ShapeDtypeStruct((b, 2, nc, ML_L, LANES), F32)),
        scratch_shapes=[pltpu.VMEM((nc * R, ML_L), F32)] * 5 + [pltpu.VMEM((nc, LANES, ML_L), F32)],
        compiler_params=pltpu.CompilerParams(dimension_semantics=("parallel", "parallel")),
        name="mlstm_gates",
    )(ga, gb)


def _mlstm_kernel(qt_ref, k_ref, vt_ref, o_ref, rowq_ref, colq_ref, gnb_ref, y_ref,
                  hf_ref, state_ref, sq_s, upd_s, *, nblk):
    d = pl.program_id(1)
    i = pl.program_id(2)
    blk = i + d * (nblk - 1 - 2 * i)
    L = ML_L
    nch = ML_TT // L
    H = ML_HEADS
    DH = ML_DH
    R = SUBLANES

    @pl.when(i == 0)
    def _():
        state_ref[...] = jnp.zeros_like(state_ref)

    s_idx = lax.broadcasted_iota(jnp.int32, (L, L), 0)
    t_idx = lax.broadcasted_iota(jnp.int32, (L, L), 1)
    fwd = d == 0
    mask = (t_idx - s_idx) * (1 - 2 * d) >= 0
    ones = jnp.ones((ML_NONES, L), BF16)
    NA = DH + ML_NONES
    c_total = hf_ref.shape[0] - nch
    hbase = jnp.where(fwd, blk * nch, c_total)

    def chunk_of(j):
        cj = j + d * (nch - 1 - 2 * j)
        return cj, pl.ds(pl.multiple_of(cj * R, R), R)

    for j in range(nch):
        cj, rows = chunk_of(j)
        off = pl.multiple_of(cj * L, L)
        mm = rowq_ref[0, rows, :]
        sc = rowq_ref[1, rows, :]
        w_end = rowq_ref[3, rows, :]
        cols = colq_ref[cj]
        for h in range(H):
            sl = slice(h * DH, (h + 1) * DH)
            k = k_ref[pl.ds(off, L), sl]
            qt = qt_ref[cj, sl, :]
            vaug = jnp.concatenate([vt_ref[cj, sl, :], ones], axis=0)
            e = jnp.exp(jnp.where(mask, cols[:, h:h + 1] - mm[h:h + 1, :], NEG))
            sq_s[j * H + h, 0:L, :] = (_dot(k, qt) * e).astype(BF16)
            sq_s[j * H + h, L:L + DH, :] = (qt.astype(F32) * sc[h:h + 1, :]).astype(BF16)
            vw = (vaug.astype(F32) * w_end[h:h + 1, :]).astype(BF16)
            upd_s[j * H + h] = _dot(vw, k)

    for j in range(nch):
        cj, rows = chunk_of(j)
        em = rowq_ref[2, rows, :]
        s_old = rowq_ref[4, rows, :]
        s_loc = rowq_ref[5, rows, :]
        for h in range(H):
            sl = slice(h * DH, (h + 1) * DH)
            vaug = jnp.concatenate([vt_ref[cj, sl, :], ones], axis=0)
            st = state_ref[h]
            r = _dot(jnp.concatenate([vaug, st.astype(BF16)], axis=1), sq_s[j * H + h])
            inv = 1.0 / jnp.maximum(jnp.abs(r[DH:DH + 1, :]), em[h:h + 1, :])
            hf_ref[hbase + cj, sl, :] = r[:DH, :] * inv
            state_ref[h] = (jnp.broadcast_to(s_old[h:h + 1, :], (NA, DH)) * st
                            + jnp.broadcast_to(s_loc[h:h + 1, :], (NA, DH)) * upd_s[j * H + h])

    @pl.when(jnp.logical_not(fwd))
    def _():
        for c in range(nch):
            for h in range(H):
                sl = slice(h * DH, (h + 1) * DH)
                tot = hf_ref[blk * nch + c, sl, :] + hf_ref[c_total + c, sl, :]
                ms = jnp.mean(tot * tot, axis=0, keepdims=True)
                hn = (tot * lax.rsqrt(ms + EPS) * gnb_ref[sl, :]).T
                tok = slice(c * L, (c + 1) * L)
                y_ref[tok, sl] = (jax.nn.sigmoid(o_ref[tok, sl]) * hn).astype(BF16)


def _mlstm(qt4, k3, vt4, o3, rowq, colq, gnb):
    b, t, w = k3.shape
    tt = ML_TT
    nblk = t // tt
    nch = tt // ML_L
    blk_of = lambda d, i: i + d * (nblk - 1 - 2 * i)
    tile = pl.BlockSpec((None, tt, w), lambda bi, d, i: (bi, blk_of(d, i), 0))
    ft_spec = pl.BlockSpec((None, nch, w, ML_L), lambda bi, d, i: (bi, blk_of(d, i), 0, 0))
    rq_spec = pl.BlockSpec((None, None, ML_NROWQ, nch * SUBLANES, ML_L),
                           lambda bi, d, i: (bi, d, 0, blk_of(d, i), 0))
    cq_spec = pl.BlockSpec((None, None, nch, ML_L, LANES), lambda bi, d, i: (bi, d, blk_of(d, i), 0, 0))
    gn_spec = pl.BlockSpec(gnb.shape, lambda bi, d, i: (0, 0))
    y_spec = pl.BlockSpec((None, tt, w), lambda bi, d, i: (bi, d * blk_of(d, i) + (1 - d) * (nblk - 1), 0))
    return pl.pallas_call(
        functools.partial(_mlstm_kernel, nblk=nblk),
        grid=(b, 2, nblk),
        in_specs=[ft_spec, tile, ft_spec, tile, rq_spec, cq_spec, gn_spec],
        out_specs=y_spec,
        out_shape=jax.ShapeDtypeStruct((b, t, w), BF16),
        scratch_shapes=[
            pltpu.VMEM((t // ML_L + nch, w, ML_L), F32),
            pltpu.VMEM((ML_HEADS, ML_DH + ML_NONES, ML_DH), F32),
            pltpu.VMEM((nch * ML_HEADS, ML_L + ML_DH, ML_L), BF16),
            pltpu.VMEM((nch * ML_HEADS, ML_DH + ML_NONES, ML_DH), F32),
        ],
        compiler_params=pltpu.CompilerParams(
            dimension_semantics=("arbitrary", "arbitrary", "arbitrary"), vmem_limit_bytes=VMEM_LIMIT),
        name="mlstm",
    )(qt4, k3, vt4, o3, rowq, colq, gnb)


def _na_kernel(q_ref, k_ref, v_ref, bias_ref, out_ref, *, rows):
    r0 = pl.program_id(1) * NA_R
    nkeys = NA_KH * GRID_W
    gw = NA_PACK * NA_DH
    row_blk = lax.broadcasted_iota(jnp.int32, (NA_PACK * GRID_W, gw), 0) // GRID_W
    lane_blk = lax.broadcasted_iota(jnp.int32, (NA_PACK * GRID_W, gw), 1) // NA_DH
    diag = row_blk == lane_blk
    out_blk = lax.broadcasted_iota(jnp.int32, (GRID_W, gw), 1) // NA_DH

    def row_body(ri, carry):
        r = r0 + ri
        rs = jnp.clip(r - NA_KH // 2, 0, rows - NA_KH)
        delta = r - rs
        koff = pl.multiple_of(rs * GRID_W, GRID_W)
        qoff = pl.multiple_of(ri * GRID_W, GRID_W)
        outs = []
        for g in range(NA_HEADS // NA_PACK):
            sl = slice(g * gw, (g + 1) * gw)
            q4 = q_ref[pl.ds(qoff, GRID_W), sl]
            qbd = jnp.where(diag, jnp.concatenate([q4] * NA_PACK, axis=0), jnp.zeros((), BF16))
            s = _dot_nt(qbd, k_ref[pl.ds(koff, nkeys), sl]) + bias_ref[delta, g]
            m = jnp.max(s, axis=-1, keepdims=True)
            p = jnp.exp(s - m)
            l = jnp.sum(p, axis=-1, keepdims=True)
            o = _dot(p.astype(BF16), v_ref[pl.ds(koff, nkeys), sl]) * (1.0 / l)
            og = o[(NA_PACK - 1) * GRID_W:, :]
            for h in range(NA_PACK - 2, -1, -1):
                og = jnp.where(out_blk == h, o[h * GRID_W:(h + 1) * GRID_W, :], og)
            outs.append(og)
        out_ref[pl.ds(qoff, GRID_W), :] = jnp.concatenate(outs, axis=-1).astype(BF16)
        return carry

    lax.fori_loop(0, NA_R, row_body, 0, unroll=2)


def _na(q3, k3, v3, bias):
    b, t, w = q3.shape
    rows = t // GRID_W
    tq = NA_R * GRID_W
    qtile = pl.BlockSpec((None, tq, w), lambda bi, i: (bi, i, 0))
    seq = pl.BlockSpec((None, t, w), lambda bi, i: (bi, 0, 0), pipeline_mode=pl.Buffered(1))
    bias_spec = pl.BlockSpec(bias.shape, lambda bi, i: (0, 0, 0, 0), pipeline_mode=pl.Buffered(1))
    return pl.pallas_call(
        functools.partial(_na_kernel, rows=rows),
        grid=(b, rows // NA_R),
        in_specs=[qtile, seq, seq, bias_spec],
        out_specs=qtile,
        out_shape=jax.ShapeDtypeStruct((b, t, w), BF16),
        compiler_params=pltpu.CompilerParams(
            dimension_semantics=("parallel", "parallel"), vmem_limit_bytes=VMEM_LIMIT),
        name="natten",
    )(q3, k3, v3, bias)


def _na_bias_table(rpb):
    c = np.arange(GRID_W)
    cs = np.clip(c - NA_KW // 2, 0, GRID_W - NA_KW)
    cc = np.arange(GRID_W)
    valid = (cc[None, :] >= cs[:, None]) & (cc[None, :] < cs[:, None] + NA_KW)
    nh, ndr, ndc = rpb.shape
    lead = GRID_W - NA_KW
    w = jnp.pad(rpb.astype(F32), ((0, 0), (0, 0), (lead, 2 * GRID_W - lead - ndc)))
    skew = jnp.broadcast_to(w[:, :, None, :], (nh, ndr, GRID_W, 2 * GRID_W)).reshape(nh, ndr, -1)
    skew = skew[:, :, :GRID_W * (2 * GRID_W - 1)].reshape(nh, ndr, GRID_W, 2 * GRID_W - 1)
    toep = jnp.where(valid[None, None], skew[..., GRID_W - 1:], NEG).transpose(0, 2, 1, 3)
    tabs = [toep[:, :, NA_KH - 1 - dl:2 * NA_KH - 1 - dl].reshape(nh, GRID_W, -1)
            for dl in range(NA_KH)]
    return jnp.stack(tabs).reshape(NA_KH, NA_HEADS // NA_PACK, NA_PACK * GRID_W, NA_KH * GRID_W)


def _outproj_ffn2_kernel(x1_ref, yml_ref, yna_ref, woml_ref, wona_ref, g2_ref, wg_ref, wu_ref, wd_ref,
                         gf_ref, out_ref):
    x2 = x1_ref[...] + _dot(yml_ref[...], woml_ref[...]) + _dot(yna_ref[...], wona_ref[...])
    h = _rms(x2, g2_ref[...]).astype(BF16)
    x3 = x2 + _swiglu_half(h, wg_ref, wu_ref, wd_ref)
    out_ref[...] = _rms(x3, gf_ref[...])


def _outproj_ffn2(x1, yml, yna, woml, wona, g2, wg, wu, wd, gf):
    n = x1.shape[0]
    tm = FFN_TM
    tok = lambda w: pl.BlockSpec((tm, w), lambda i: (i, 0))
    return pl.pallas_call(
        _outproj_ffn2_kernel,
        grid=(n // tm,),
        in_specs=[tok(D_MODEL), tok(ML_WIDTH), tok(NA_WIDTH)] + [
            _const_spec(a.shape) for a in (woml, wona, g2, wg, wu, wd, gf)],
        out_specs=tok(D_MODEL),
        out_shape=jax.ShapeDtypeStruct((n, D_MODEL), F32),
        compiler_params=pltpu.CompilerParams(
            dimension_semantics=("parallel",), vmem_limit_bytes=VMEM_LIMIT),
        name="outproj_ffn2",
    )(x1, yml, yna, woml, wona, g2, wg, wu, wd, gf)


def _layer(x, norm_ffn1, w1_gate, w1_up, w1_down, norm_mix, w_in, b_gates, conv_w, conv_b, w_q_ml,
           w_k_ml, gn_ml, gq_na, gk_na, rpb, w_out, norm_ffn2, w2_gate, w2_up, w2_down, norm_final):
    b, t, dm = x.shape
    n = b * t
    W = ML_WIDTH
    row = lambda a: a.reshape(1, -1).astype(F32)
    g0 = 3 * W
    wc = w_in[:, :W].astype(BF16)
    wvt = w_in[:, W:2 * W].T.astype(BF16)
    wo = w_in[:, 2 * W:g0].astype(BF16)
    wna = w_in[:, g0 + N_GATES:].astype(BF16)
    wgate = w_in[:, g0:g0 + N_GATES]
    H = ML_HEADS
    order = np.concatenate([dd * 2 * H + np.r_[0:2 * H, H:2 * H, 0:H] for dd in range(2)])
    wgr = wgate.T[order].astype(BF16)
    bgr = b_gates[order].reshape(-1, 1).astype(F32)
    ones = jnp.asarray(np.kron(np.eye(NA_HEADS), np.ones((NA_DH, NA_DH))), BF16)
    gq = jnp.tile(gq_na.astype(F32), NA_HEADS).reshape(1, -1)
    gk = jnp.tile(gk_na.astype(F32), NA_HEADS).reshape(1, -1)

    x1, c, vt_ml, o, ga, gb, qn, kn, vn = _ffn1_inproj(
        x.reshape(n, dm), row(norm_ffn1), w1_gate.astype(BF16), w1_up.astype(BF16), w1_down.astype(BF16),
        row(norm_mix), wc, wvt, wo, wna, wgr, bgr, ones, gq, gk)

    seq = lambda a: a.reshape(b, t, a.shape[-1])
    qt_ml, k_ml = _conv_qk(seq(c), conv_w.astype(F32), row(conv_b),
                           w_q_ml.transpose(0, 2, 1).astype(BF16), w_k_ml.astype(BF16))
    rowq, colq = _mlstm_gates(ga.reshape(2, -1, ML_L), gb.reshape(2, -1, ML_L), b, t)
    gnb = jnp.broadcast_to(gn_ml.astype(F32).reshape(W, 1), (W, ML_L))
    y_ml = _mlstm(qt_ml, k_ml, vt_ml.reshape(b, t // ML_L, W, ML_L), seq(o), rowq, colq, gnb)
    y_na = _na(seq(qn), seq(kn), seq(vn), _na_bias_table(rpb))

    out = _outproj_ffn2(x1, y_ml.reshape(n, W), y_na.reshape(n, NA_WIDTH),
                        w_out[:W].astype(BF16), w_out[W:].astype(BF16), row(norm_ffn2),
                        w2_gate.astype(BF16), w2_up.astype(BF16), w2_down.astype(BF16), row(norm_final))
    return out.reshape(b, t, dm)


def kernel(x, norm_ffn1, w1_gate, w1_up, w1_down, norm_mix, w_in, b_gates, conv_w, conv_b, w_q_ml, w_k_ml,
           gn_ml, gq_na, gk_na, rpb, w_out, norm_ffn2, w2_gate, w2_up, w2_down, norm_final):
    depth = norm_ffn1.shape[0]
    for l in range(depth):
        x = _layer(x, norm_ffn1[l], w1_gate[l], w1_up[l], w1_down[l], norm_mix[l], w_in[l], b_gates[l],
                   conv_w[l], conv_b[l], w_q_ml[l], w_k_ml[l], gn_ml[l], gq_na[l], gk_na[l], rpb[l],
                   w_out[l], norm_ffn2[l], w2_gate[l], w2_up[l], w2_down[l], norm_final[l])
    return x
```

```python
import functools

import jax
import jax.numpy as jnp
import numpy as np
from jax import lax
from jax.experimental import pallas as pl
from jax.experimental.pallas import tpu as pltpu

F32 = jnp.float32
BF16 = jnp.bfloat16

D_MODEL = 1024
D_FF = 2816
GRID_W = 64
ML_HEADS = 4
ML_DH = 128
ML_WIDTH = ML_HEADS * ML_DH
CONV_W = 5
NA_HEADS = 8
NA_DH = 64
NA_WIDTH = NA_HEADS * NA_DH
NA_KH = 8
NA_KW = 16
N_GATES = 4 * ML_HEADS
EPS = 1e-6
NEG = -1e30
LOG2E = 1.4426950408889634

SUBLANES = 8
LANES = 128

FFN_TM = 512
FFN_CK = 1408
ML_L = 128
ML_TT = 512
ML_NROWQ = 6
ML_NONES = 16
CONV_TT = 512
NA_R = 8
NA_PACK = 4
VMEM_LIMIT = 56 * 1024 * 1024


def _dot(a, b):
    return jnp.dot(a, b, preferred_element_type=F32)


def _dot_nt(a, b):
    return lax.dot_general(a, b, (((1,), (1,)), ((), ())), preferred_element_type=F32)


def _dot_tn(a, b):
    return lax.dot_general(a, b, (((0,), (0,)), ((), ())), preferred_element_type=F32)


def _dot_exact(a, b):
    return jnp.dot(a, b, preferred_element_type=F32, precision=lax.Precision.HIGHEST)


def _rms(x, g):
    ms = jnp.mean(x * x, axis=-1, keepdims=True)
    return x * lax.rsqrt(ms + EPS) * g


def _log_sigmoid(x):
    return jnp.minimum(x, 0.0) - jnp.log1p(jnp.exp(-jnp.abs(x)))


def _swiglu_half(h, wg_ref, wu_ref, wd_ref):
    acc = None
    for j in range(D_FF // FFN_CK):
        sl = slice(j * FFN_CK, (j + 1) * FFN_CK)
        g = _dot(h, wg_ref[:, sl])
        u = _dot(h, wu_ref[:, sl])
        a = (g * jax.nn.sigmoid(g) * u).astype(BF16)
        part = _dot(a, wd_ref[sl, :])
        acc = part if acc is None else acc + part
    return 0.5 * acc


def _group_mean_sq(x, ones_ref):
    xx = x * x
    return _dot(xx.astype(BF16), ones_ref[...]) * (1.0 / NA_DH)


def _ffn1_inproj_kernel(x_ref, g1_ref, wg_ref, wu_ref, wd_ref, gm_ref, wc_ref, wvt_ref, wo_ref, wna_ref,
                        wgr_ref, bgr_ref, ones_ref, gq_ref, gk_ref,
                        x1_ref, c_ref, vt_ref, o_ref, ga_ref, gb_ref, qn_ref, kn_ref, vn_ref):
    x = x_ref[...]
    h = _rms(x, g1_ref[...]).astype(BF16)
    x1 = x + _swiglu_half(h, wg_ref, wu_ref, wd_ref)
    x1_ref[...] = x1
    h2 = _rms(x1, gm_ref[...]).astype(BF16)
    W = ML_WIDTH
    c_ref[...] = _dot(h2, wc_ref[...])
    o_ref[...] = _dot(h2, wo_ref[...])
    vt = _dot_nt(wvt_ref[...], h2).astype(BF16)
    for cidx in range(FFN_TM // ML_L):
        vt_ref[cidx] = vt[:, cidx * ML_L:(cidx + 1) * ML_L]
    p = _dot(h2, wna_ref[...])
    q = p[:, 0:W]
    k = p[:, W:2 * W]
    vn_ref[...] = p[:, 2 * W:3 * W].astype(BF16)
    qn = q * lax.rsqrt(_group_mean_sq(q, ones_ref) + EPS) * gq_ref[...] * (NA_DH ** -0.5 * LOG2E)
    kn = k * lax.rsqrt(_group_mean_sq(k, ones_ref) + EPS) * gk_ref[...]
    qn_ref[...] = qn.astype(BF16)
    kn_ref[...] = kn.astype(BF16)
    gr = _dot_nt(wgr_ref[...], h2) + bgr_ref[...]
    row = lax.broadcasted_iota(jnp.int32, gr.shape, 0) % (4 * ML_HEADS)
    gr = jnp.where((row >= ML_HEADS) & (row < 3 * ML_HEADS), _log_sigmoid(gr), gr)
    for dd in range(2):
        for cidx in range(FFN_TM // ML_L):
            lanes = slice(cidx * ML_L, (cidx + 1) * ML_L)
            ga_ref[dd, cidx] = gr[dd * 16:dd * 16 + 8, lanes]
            gb_ref[dd, cidx] = gr[dd * 16 + 8:dd * 16 + 16, lanes]


def _const_spec(shape):
    nd = len(shape)
    return pl.BlockSpec(shape, lambda *_: (0,) * nd, pipeline_mode=pl.Buffered(1))


def _ffn1_inproj(x2d, g1, wg, wu, wd, gm, wc, wvt, wo, wna, wgr, bgr, ones, gq, gk):
    n = x2d.shape[0]
    tm = FFN_TM
    W = ML_WIDTH
    tok = lambda w: pl.BlockSpec((tm, w), lambda i: (i, 0))
    out_shape = (
        jax.ShapeDtypeStruct((n, D_MODEL), F32),
        jax.ShapeDtypeStruct((n, W), F32),
        jax.ShapeDtypeStruct((n // ML_L, W, ML_L), BF16),
        jax.ShapeDtypeStruct((n, W), F32),
        jax.ShapeDtypeStruct((2, n // ML_L, SUBLANES, ML_L), F32),
        jax.ShapeDtypeStruct((2, n // ML_L, SUBLANES, ML_L), F32),
        jax.ShapeDtypeStruct((n, W), BF16),
        jax.ShapeDtypeStruct((n, W), BF16),
        jax.ShapeDtypeStruct((n, W), BF16),
    )
    out_specs = (
        tok(D_MODEL), tok(W), pl.BlockSpec((tm // ML_L, W, ML_L), lambda i: (i, 0, 0)), tok(W),
        pl.BlockSpec((2, tm // ML_L, SUBLANES, ML_L), lambda i: (0, i, 0, 0)),
        pl.BlockSpec((2, tm // ML_L, SUBLANES, ML_L), lambda i: (0, i, 0, 0)),
        tok(W), tok(W), tok(W),
    )
    in_specs = [tok(D_MODEL)] + [_const_spec(a.shape) for a in
                                 (g1, wg, wu, wd, gm, wc, wvt, wo, wna, wgr, bgr, ones, gq, gk)]
    return pl.pallas_call(
        _ffn1_inproj_kernel,
        grid=(n // tm,),
        in_specs=in_specs,
        out_specs=out_specs,
        out_shape=out_shape,
        compiler_params=pltpu.CompilerParams(
            dimension_semantics=("parallel",), vmem_limit_bytes=VMEM_LIMIT),
        name="ffn1_inproj",
    )(x2d, g1, wg, wu, wd, gm, wc, wvt, wo, wna, wgr, bgr, ones, gq, gk)


def _conv_qk_kernel(c_ref, prev_ref, next_ref, cw_ref, cb_ref, wqt_ref, wk_ref, qt_ref, k_ref, pad_ref):
    i = pl.program_id(1)
    nb = pl.num_programs(1)
    tt = CONV_TT
    half = CONV_W // 2
    pad_ref[0:SUBLANES, :] = jnp.where(i > 0, prev_ref[...], 0.0)
    pad_ref[SUBLANES:SUBLANES + tt, :] = c_ref[...]
    pad_ref[SUBLANES + tt:2 * SUBLANES + tt, :] = jnp.where(i < nb - 1, next_ref[...], 0.0)
    y = cb_ref[...]
    for kk in range(CONV_W):
        start = SUBLANES + kk - half
        y = y + pad_ref[start:start + tt, :] * cw_ref[kk:kk + 1, :]
    u = y * jax.nn.sigmoid(y)
    for h in range(ML_HEADS):
        sl = slice(h * ML_DH, (h + 1) * ML_DH)
        uh = u[:, sl].astype(BF16)
        k_ref[:, sl] = (_dot(uh, wk_ref[h]) * (ML_DH ** -0.5)).astype(BF16)
        qt = _dot_nt(wqt_ref[h], uh).astype(BF16)
        for cidx in range(tt // ML_L):
            qt_ref[cidx, sl, :] = qt[:, cidx * ML_L:(cidx + 1) * ML_L]


def _conv_qk(c3, cw, cb, wqt, wk):
    b, t, w = c3.shape
    tt = CONV_TT
    nb = t // tt
    per = tt // SUBLANES
    last = t // SUBLANES - 1
    tile = pl.BlockSpec((None, tt, w), lambda bi, i: (bi, i, 0))
    prev = pl.BlockSpec((None, SUBLANES, w), lambda bi, i: (bi, jnp.maximum(i * per - 1, 0), 0))
    nxt = pl.BlockSpec((None, SUBLANES, w), lambda bi, i: (bi, jnp.minimum((i + 1) * per, last), 0))
    return pl.pallas_call(
        _conv_qk_kernel,
        grid=(b, nb),
        in_specs=[tile, prev, nxt] + [
            pl.BlockSpec(a.shape, functools.partial(lambda nd, bi, i: (0,) * nd, a.ndim))
            for a in (cw, cb, wqt, wk)],
        out_specs=(pl.BlockSpec((None, tt // ML_L, w, ML_L), lambda bi, i: (bi, i, 0, 0)), tile),
        out_shape=(jax.ShapeDtypeStruct((b, t // ML_L, w, ML_L), BF16),
                   jax.ShapeDtypeStruct((b, t, w), BF16)),
        scratch_shapes=[pltpu.VMEM((tt + 2 * SUBLANES, w), F32)],
        compiler_params=pltpu.CompilerParams(dimension_semantics=("parallel", "parallel")),
        name="conv_qk",
    )(c3, c3, c3, cw, cb, wqt, wk)


def _lane_scan(x, pos, d, combine, ident):
    L = x.shape[-1]
    k = 1
    while k < L:
        shifted = pltpu.roll(x, jnp.where(d == 0, k, L - k), axis=1)
        x = combine(x, jnp.where(pos >= k, shifted, ident))
        k *= 2
    return x


def _mlstm_gates_kernel(ga_ref, gb_ref, rowq_ref, colq_ref, b_s, g_s, mloc_s, rmax_s, w_s, vec_s):
    d = pl.program_id(1)
    R = SUBLANES
    L = ML_L
    nc = ga_ref.shape[0] // R
    xa = ga_ref[...]
    xb = gb_ref[...]
    lane = lax.broadcasted_iota(jnp.int32, xa.shape, 1)
    pos = lane + d * (L - 1 - 2 * lane)
    b = _lane_scan(xb, pos, d, jnp.add, 0.0)
    g = jnp.broadcast_to(jnp.sum(xb, axis=1, keepdims=True), xb.shape)
    w = xa - b
    a = g + w
    m_loc = jnp.broadcast_to(jnp.max(a, axis=1, keepdims=True), a.shape)
    rowq_ref[3] = jnp.exp(a - m_loc)
    b_s[...] = b
    g_s[...] = g
    mloc_s[...] = m_loc
    rmax_s[...] = _lane_scan(w, pos, d, jnp.maximum, NEG)
    w_s[...] = w
    vec_s[...] = jnp.zeros_like(vec_s)

    def chunk(k, m0):
        ck = k + d * (nc - 1 - 2 * k)
        rows = pl.ds(pl.multiple_of(ck * R, R), R)
        gk = g_s[rows, :]
        mlk = mloc_s[rows, :]
        mm = jnp.maximum(m0, rmax_s[rows, :])
        m_new = jnp.maximum(gk + m0, mlk)
        rowq_ref[0, rows, :] = mm
        rowq_ref[1, rows, :] = jnp.exp(m0 - mm)
        rowq_ref[2, rows, :] = jnp.exp(-(b_s[rows, :] + mm))
        rowq_ref[4, rows, :] = jnp.exp(gk + m0 - m_new)
        rowq_ref[5, rows, :] = jnp.exp(mlk - m_new)
        vec_s[ck, 0:R, :] = w_s[rows, :]
        colq_ref[ck] = vec_s[ck].T
        return m_new

    lax.fori_loop(0, nc, chunk, jnp.zeros((R, L), F32), unroll=8)


def _mlstm_gates(ga, gb, b, t):
    nc = t // ML_L
    R = SUBLANES
    in_spec = pl.BlockSpec((None, nc * R, ML_L), lambda bi, d: (d, bi, 0))
    return pl.pallas_call(
        _mlstm_gates_kernel,
        grid=(b, 2),
        in_specs=[in_spec, in_spec],
        out_specs=(pl.BlockSpec((None, None, ML_NROWQ, nc * R, ML_L), lambda bi, d: (bi, d, 0, 0, 0)),
                   pl.BlockSpec((None, None, nc, ML_L, LANES), lambda bi, d: (bi, d, 0, 0, 0))),
        out_shape=(jax.ShapeDtypeStruct((b, 2, ML_NROWQ, nc * R, ML_L), F32),
                   jax.ShapeDtypeStruct((b, 2, nc, ML_L, LANES), F32)),
        scratch_shapes=[pltpu.VMEM((nc * R, ML_L), F32)] * 5 + [pltpu.VMEM((nc, LANES, ML_L), F32)],
        compiler_params=pltpu.CompilerParams(dimension_semantics=("parallel", "parallel")),
        name="mlstm_gates",
    )(ga, gb)


def _mlstm_kernel(qt_ref, k_ref, vt_ref, o_ref, rowq_ref, colq_ref, gnb_ref, y_ref,
                  hf_ref, state_ref, sq_s, upd_s, *, nblk):
    d = pl.program_id(1)
    i = pl.program_id(2)
    blk = i + d * (nblk - 1 - 2 * i)
    L = ML_L
    nch = ML_TT // L
    H = ML_HEADS
    DH = ML_DH
    R = SUBLANES

    @pl.when(i == 0)
    def _():
        state_ref[...] = jnp.zeros_like(state_ref)

    s_idx = lax.broadcasted_iota(jnp.int32, (L, L), 0)
    t_idx = lax.broadcasted_iota(jnp.int32, (L, L), 1)
    fwd = d == 0
    mask = (t_idx - s_idx) * (1 - 2 * d) >= 0
    ones = jnp.ones((ML_NONES, L), BF16)
    NA = DH + ML_NONES
    c_total = hf_ref.shape[0] - nch
    hbase = jnp.where(fwd, blk * nch, c_total)

    def chunk_of(j):
        cj = j + d * (nch - 1 - 2 * j)
        return cj, pl.ds(pl.multiple_of(cj * R, R), R)

    for j in range(nch):
        cj, rows = chunk_of(j)
        off = pl.multiple_of(cj * L, L)
        mm = rowq_ref[0, rows, :]
        sc = rowq_ref[1, rows, :]
        w_end = rowq_ref[3, rows, :]
        cols = colq_ref[cj]
        for h in range(H):
            sl = slice(h * DH, (h + 1) * DH)
            k = k_ref[pl.ds(off, L), sl]
            qt = qt_ref[cj, sl, :]
            vaug = jnp.concatenate([vt_ref[cj, sl, :], ones], axis=0)
            e = jnp.exp(jnp.where(mask, cols[:, h:h + 1] - mm[h:h + 1, :], NEG))
            sq_s[j * H + h, 0:L, :] = (_dot(k, qt) * e).astype(BF16)
            sq_s[j * H + h, L:L + DH, :] = (qt.astype(F32) * sc[h:h + 1, :]).astype(BF16)
            vw = (vaug.astype(F32) * w_end[h:h + 1, :]).astype(BF16)
            upd_s[j * H + h] = _dot(vw, k)

    for j in range(nch):
        cj, rows = chunk_of(j)
        em = rowq_ref[2, rows, :]
        s_old = rowq_ref[4, rows, :]
        s_loc = rowq_ref[5, rows, :]
        for h in range(H):
            sl = slice(h * DH, (h + 1) * DH)
            vaug = jnp.concatenate([vt_ref[cj, sl, :], ones], axis=0)
            st = state_ref[h]
            r = _dot(jnp.concatenate([vaug, st.astype(BF16)], axis=1), sq_s[j * H + h])
            inv = 1.0 / jnp.maximum(jnp.abs(r[DH:DH + 1, :]), em[h:h + 1, :])
            hf_ref[hbase + cj, sl, :] = r[:DH, :] * inv
            state_ref[h] = (jnp.broadcast_to(s_old[h:h + 1, :], (NA, DH)) * st
                            + jnp.broadcast_to(s_loc[h:h + 1, :], (NA, DH)) * upd_s[j * H + h])

    @pl.when(jnp.logical_not(fwd))
    def _():
        for c in range(nch):
            for h in range(H):
                sl = slice(h * DH, (h + 1) * DH)
                tot = hf_ref[blk * nch + c, sl, :] + hf_ref[c_total + c, sl, :]
                ms = jnp.mean(tot * tot, axis=0, keepdims=True)
                hn = (tot * lax.rsqrt(ms + EPS) * gnb_ref[sl, :]).T
                tok = slice(c * L, (c + 1) * L)
                y_ref[tok, sl] = (jax.nn.sigmoid(o_ref[tok, sl]) * hn).astype(BF16)


def _mlstm(qt4, k3, vt4, o3, rowq, colq, gnb):
    b, t, w = k3.shape
    tt = ML_TT
    nblk = t // tt
    nch = tt // ML_L
    blk_of = lambda d, i: i + d * (nblk - 1 - 2 * i)
    tile = pl.BlockSpec((None, tt, w), lambda bi, d, i: (bi, blk_of(d, i), 0))
    ft_spec = pl.BlockSpec((None, nch, w, ML_L), lambda bi, d, i: (bi, blk_of(d, i), 0, 0))
    rq_spec = pl.BlockSpec((None, None, ML_NROWQ, nch * SUBLANES, ML_L),
                           lambda bi, d, i: (bi, d, 0, blk_of(d, i), 0))
    cq_spec = pl.BlockSpec((None, None, nch, ML_L, LANES), lambda bi, d, i: (bi, d, blk_of(d, i), 0, 0))
    gn_spec = pl.BlockSpec(gnb.shape, lambda bi, d, i: (0, 0))
    y_spec = pl.BlockSpec((None, tt, w), lambda bi, d, i: (bi, d * blk_of(d, i) + (1 - d) * (nblk - 1), 0))
    return pl.pallas_call(
        functools.partial(_mlstm_kernel, nblk=nblk),
        grid=(b, 2, nblk),
        in_specs=[ft_spec, tile, ft_spec, tile, rq_spec, cq_spec, gn_spec],
        out_specs=y_spec,
        out_shape=jax.ShapeDtypeStruct((b, t, w), BF16),
        scratch_shapes=[
            pltpu.VMEM((t // ML_L + nch, w, ML_L), F32),
            pltpu.VMEM((ML_HEADS, ML_DH + ML_NONES, ML_DH), F32),
            pltpu.VMEM((nch * ML_HEADS, ML_L + ML_DH, ML_L), BF16),
            pltpu.VMEM((nch * ML_HEADS, ML_DH + ML_NONES, ML_DH), F32),
        ],
        compiler_params=pltpu.CompilerParams(
            dimension_semantics=("arbitrary", "arbitrary", "arbitrary"), vmem_limit_bytes=VMEM_LIMIT),
        name="mlstm",
    )(qt4, k3, vt4, o3, rowq, colq, gnb)


def _na_kernel(q_ref, k_ref, v_ref, bias_ref, out_ref, *, rows):
    r0 = pl.program_id(1) * NA_R
    nkeys = NA_KH * GRID_W
    gw = NA_PACK * NA_DH
    row_blk = lax.broadcasted_iota(jnp.int32, (NA_PACK * GRID_W, gw), 0) // GRID_W
    lane_blk = lax.broadcasted_iota(jnp.int32, (NA_PACK * GRID_W, gw), 1) // NA_DH
    diag = row_blk == lane_blk
    out_blk = lax.broadcasted_iota(jnp.int32, (GRID_W, gw), 1) // NA_DH

    ng = NA_HEADS // NA_PACK

    def window(ri):
        r = r0 + ri
        rs = jnp.clip(r - NA_KH // 2, 0, rows - NA_KH)
        return pl.ds(pl.multiple_of(rs * GRID_W, GRID_W), nkeys), r - rs

    def row_body(ri, carry):
        keys, delta = window(ri)
        qoff = pl.multiple_of(ri * GRID_W, GRID_W)
        outs = []
        for g in range(ng):
            sl = slice(g * gw, (g + 1) * gw)
            q4 = q_ref[pl.ds(qoff, GRID_W), sl]
            qbd = jnp.where(diag, jnp.concatenate([q4] * NA_PACK, axis=0), jnp.zeros((), BF16))
            s = _dot_nt(qbd, k_ref[keys, sl]) + bias_ref[delta, g]
            p = jnp.exp2(s - jnp.max(s, axis=-1, keepdims=True))
            linv = 1.0 / jnp.sum(p, axis=-1, keepdims=True)
            o = _dot(p.astype(BF16), v_ref[keys, sl]) * linv
            og = o[(NA_PACK - 1) * GRID_W:, :]
            for h in range(NA_PACK - 2, -1, -1):
                og = jnp.where(out_blk == h, o[h * GRID_W:(h + 1) * GRID_W, :], og)
            outs.append(og)
        out_ref[pl.ds(qoff, GRID_W), :] = jnp.concatenate(outs, axis=-1).astype(BF16)
        return carry

    lax.fori_loop(0, NA_R, row_body, 0, unroll=True)


def _na(q3, k3, v3, bias):
    b, t, w = q3.shape
    rows = t // GRID_W
    tq = NA_R * GRID_W
    qtile = pl.BlockSpec((None, tq, w), lambda bi, i: (bi, i, 0))
    seq = pl.BlockSpec((None, t, w), lambda bi, i: (bi, 0, 0), pipeline_mode=pl.Buffered(1))
    bias_spec = pl.BlockSpec(bias.shape, lambda bi, i: (0, 0, 0, 0), pipeline_mode=pl.Buffered(1))
    return pl.pallas_call(
        functools.partial(_na_kernel, rows=rows),
        grid=(b, rows // NA_R),
        in_specs=[qtile, seq, seq, bias_spec],
        out_specs=qtile,
        out_shape=jax.ShapeDtypeStruct((b, t, w), BF16),
        compiler_params=pltpu.CompilerParams(
            dimension_semantics=("parallel", "parallel"), vmem_limit_bytes=VMEM_LIMIT),
        name="natten",
    )(q3, k3, v3, bias)


def _na_bias_table(rpb):
    c = np.arange(GRID_W)
    cs = np.clip(c - NA_KW // 2, 0, GRID_W - NA_KW)
    cc = np.arange(GRID_W)
    valid = (cc[None, :] >= cs[:, None]) & (cc[None, :] < cs[:, None] + NA_KW)
    nh, ndr, ndc = rpb.shape
    lead = GRID_W - NA_KW
    w = jnp.pad(rpb.astype(F32), ((0, 0), (0, 0), (lead, 2 * GRID_W - lead - ndc)))
    skew = jnp.broadcast_to(w[:, :, None, :], (nh, ndr, GRID_W, 2 * GRID_W)).reshape(nh, ndr, -1)
    skew = skew[:, :, :GRID_W * (2 * GRID_W - 1)].reshape(nh, ndr, GRID_W, 2 * GRID_W - 1)
    toep = jnp.where(valid[None, None], skew[..., GRID_W - 1:] * LOG2E, NEG).transpose(0, 2, 1, 3)
    tabs = [toep[:, :, NA_KH - 1 - dl:2 * NA_KH - 1 - dl].reshape(nh, GRID_W, -1)
            for dl in range(NA_KH)]
    return jnp.stack(tabs).reshape(NA_KH, NA_HEADS // NA_PACK, NA_PACK * GRID_W, NA_KH * GRID_W)


def _outproj_ffn2_kernel(x1_ref, yml_ref, yna_ref, woml_ref, wona_ref, g2_ref, wg_ref, wu_ref, wd_ref,
                         gf_ref, out_ref):
    x2 = x1_ref[...] + _dot(yml_ref[...], woml_ref[...]) + _dot(yna_ref[...], wona_ref[...])
    h = _rms(x2, g2_ref[...]).astype(BF16)
    x3 = x2 + _swiglu_half(h, wg_ref, wu_ref, wd_ref)
    out_ref[...] = _rms(x3, gf_ref[...])


def _outproj_ffn2(x1, yml, yna, woml, wona, g2, wg, wu, wd, gf):
    n = x1.shape[0]
    tm = FFN_TM
    tok = lambda w: pl.BlockSpec((tm, w), lambda i: (i, 0))
    return pl.pallas_call(
        _outproj_ffn2_kernel,
        grid=(n // tm,),
        in_specs=[tok(D_MODEL), tok(ML_WIDTH), tok(NA_WIDTH)] + [
            _const_spec(a.shape) for a in (woml, wona, g2, wg, wu, wd, gf)],
        out_specs=tok(D_MODEL),
        out_shape=jax.ShapeDtypeStruct((n, D_MODEL), F32),
        compiler_params=pltpu.CompilerParams(
            dimension_semantics=("parallel",), vmem_limit_bytes=VMEM_LIMIT),
        name="outproj_ffn2",
    )(x1, yml, yna, woml, wona, g2, wg, wu, wd, gf)


def _layer(x, norm_ffn1, w1_gate, w1_up, w1_down, norm_mix, w_in, b_gates, conv_w, conv_b, w_q_ml,
           w_k_ml, gn_ml, gq_na, gk_na, rpb, w_out, norm_ffn2, w2_gate, w2_up, w2_down, norm_final):
    b, t, dm = x.shape
    n = b * t
    W = ML_WIDTH
    row = lambda a: a.reshape(1, -1).astype(F32)
    g0 = 3 * W
    wc = w_in[:, :W].astype(BF16)
    wvt = w_in[:, W:2 * W].T.astype(BF16)
    wo = w_in[:, 2 * W:g0].astype(BF16)
    wna = w_in[:, g0 + N_GATES:].astype(BF16)
    wgate = w_in[:, g0:g0 + N_GATES]
    H = ML_HEADS
    order = np.concatenate([dd * 2 * H + np.r_[0:2 * H, H:2 * H, 0:H] for dd in range(2)])
    wgr = wgate.T[order].astype(BF16)
    bgr = b_gates[order].reshape(-1, 1).astype(F32)
    ones = jnp.asarray(np.kron(np.eye(NA_HEADS), np.ones((NA_DH, NA_DH))), BF16)
    gq = jnp.tile(gq_na.astype(F32), NA_HEADS).reshape(1, -1)
    gk = jnp.tile(gk_na.astype(F32), NA_HEADS).reshape(1, -1)

    x1, c, vt_ml, o, ga, gb, qn, kn, vn = _ffn1_inproj(
        x.reshape(n, dm), row(norm_ffn1), w1_gate.astype(BF16), w1_up.astype(BF16), w1_down.astype(BF16),
        row(norm_mix), wc, wvt, wo, wna, wgr, bgr, ones, gq, gk)

    seq = lambda a: a.reshape(b, t, a.shape[-1])
    qt_ml, k_ml = _conv_qk(seq(c), conv_w.astype(F32), row(conv_b),
                           w_q_ml.transpose(0, 2, 1).astype(BF16), w_k_ml.astype(BF16))
    rowq, colq = _mlstm_gates(ga.reshape(2, -1, ML_L), gb.reshape(2, -1, ML_L), b, t)
    gnb = jnp.broadcast_to(gn_ml.astype(F32).reshape(W, 1), (W, ML_L))
    y_ml = _mlstm(qt_ml, k_ml, vt_ml.reshape(b, t // ML_L, W, ML_L), seq(o), rowq, colq, gnb)
    y_na = _na(seq(qn), seq(kn), seq(vn), _na_bias_table(rpb))

    out = _outproj_ffn2(x1, y_ml.reshape(n, W), y_na.reshape(n, NA_WIDTH),
                        w_out[:W].astype(BF16), w_out[W:].astype(BF16), row(norm_ffn2),
                        w2_gate.astype(BF16), w2_up.astype(BF16), w2_down.astype(BF16), row(norm_final))
    return out.reshape(b, t, dm)


def kernel(x, norm_ffn1, w1_gate, w1_up, w1_down, norm_mix, w_in, b_gates, conv_w, conv_b, w_q_ml, w_k_ml,
           gn_ml, gq_na, gk_na, rpb, w_out, norm_ffn2, w2_gate, w2_up, w2_down, norm_final):
    depth = norm_ffn1.shape[0]
    for l in range(depth):
        x = _layer(x, norm_ffn1[l], w1_gate[l], w1_up[l], w1_down[l], norm_mix[l], w_in[l], b_gates[l],
                   conv_w[l], conv_b[l], w_q_ml[l], w_k_ml[l], gn_ml[l], gq_na[l], gk_na[l], rpb[l],
                   w_out[l], norm_ffn2[l], w2_gate[l], w2_up[l], w2_down[l], norm_final[l])
    return x
```

```python
import functools

import jax
import jax.numpy as jnp
import numpy as np
from jax import lax
from jax.experimental import pallas as pl
from jax.experimental.pallas import tpu as pltpu

F32 = jnp.float32
BF16 = jnp.bfloat16

D_MODEL = 1024
D_FF = 2816
GRID_W = 64
ML_HEADS = 4
ML_DH = 128
ML_WIDTH = ML_HEADS * ML_DH
CONV_W = 5
NA_HEADS = 8
NA_DH = 64
NA_WIDTH = NA_HEADS * NA_DH
NA_KH = 8
NA_KW = 16
N_GATES = 4 * ML_HEADS
EPS = 1e-6
NEG = -1e30
LOG2E = 1.4426950408889634

SUBLANES = 8
LANES = 128

FFN_TM = 512
FFN_CK = 2816
ML_L = 128
ML_TT = 512
ML_NROWQ = 6
ML_NONES = 16
CONV_TT = 512
NA_R = 8
NA_PACK = 4
VMEM_LIMIT = 56 * 1024 * 1024


def _dot(a, b):
    return jnp.dot(a, b, preferred_element_type=F32)


def _dot_nt(a, b):
    return lax.dot_general(a, b, (((1,), (1,)), ((), ())), preferred_element_type=F32)


def _dot_tn(a, b):
    return lax.dot_general(a, b, (((0,), (0,)), ((), ())), preferred_element_type=F32)


def _dot_exact(a, b):
    return jnp.dot(a, b, preferred_element_type=F32, precision=lax.Precision.HIGHEST)


def _rms(x, g):
    ms = jnp.mean(x * x, axis=-1, keepdims=True)
    return x * lax.rsqrt(ms + EPS) * g


def _log_sigmoid(x):
    return jnp.minimum(x, 0.0) - jnp.log1p(jnp.exp(-jnp.abs(x)))


def _swiglu_half(h, wg_ref, wu_ref, wd_ref):
    acc = None
    for j in range(D_FF // FFN_CK):
        sl = slice(j * FFN_CK, (j + 1) * FFN_CK)
        g = _dot(h, wg_ref[:, sl])
        u = _dot(h, wu_ref[:, sl])
        a = (g * jax.nn.sigmoid(g) * u).astype(BF16)
        part = _dot(a, wd_ref[sl, :])
        acc = part if acc is None else acc + part
    return 0.5 * acc


def _group_mean_sq(x, ones_ref):
    xx = x * x
    return _dot(xx.astype(BF16), ones_ref[...]) * (1.0 / NA_DH)


def _ffn1_inproj_kernel(x_ref, g1_ref, wg_ref, wu_ref, wd_ref, gm_ref, wc_ref, wvt_ref, wo_ref, wna_ref,
                        wgr_ref, bgr_ref, ones_ref, gq_ref, gk_ref,
                        x1_ref, c_ref, vt_ref, o_ref, ga_ref, gb_ref, qn_ref, kn_ref, vn_ref):
    x = x_ref[...]
    h = _rms(x, g1_ref[...]).astype(BF16)
    x1 = x + _swiglu_half(h, wg_ref, wu_ref, wd_ref)
    x1_ref[...] = x1
    h2 = _rms(x1, gm_ref[...]).astype(BF16)
    W = ML_WIDTH
    c_ref[...] = _dot(h2, wc_ref[...])
    o_ref[...] = _dot(h2, wo_ref[...])
    vt = _dot_nt(wvt_ref[...], h2).astype(BF16)
    for cidx in range(FFN_TM // ML_L):
        vt_ref[cidx] = vt[:, cidx * ML_L:(cidx + 1) * ML_L]
    p = _dot(h2, wna_ref[...])
    q = p[:, 0:W]
    k = p[:, W:2 * W]
    vn_ref[...] = p[:, 2 * W:3 * W].astype(BF16)
    qn = q * lax.rsqrt(_group_mean_sq(q, ones_ref) + EPS) * gq_ref[...] * (NA_DH ** -0.5 * LOG2E)
    kn = k * lax.rsqrt(_group_mean_sq(k, ones_ref) + EPS) * gk_ref[...]
    qn_ref[...] = qn.astype(BF16)
    kn_ref[...] = kn.astype(BF16)
    gr = _dot_nt(wgr_ref[...], h2) + bgr_ref[...]
    row = lax.broadcasted_iota(jnp.int32, gr.shape, 0) % (4 * ML_HEADS)
    gr = jnp.where((row >= ML_HEADS) & (row < 3 * ML_HEADS), _log_sigmoid(gr), gr)
    for dd in range(2):
        for cidx in range(FFN_TM // ML_L):
            lanes = slice(cidx * ML_L, (cidx + 1) * ML_L)
            ga_ref[dd, cidx] = gr[dd * 16:dd * 16 + 8, lanes]
            gb_ref[dd, cidx] = gr[dd * 16 + 8:dd * 16 + 16, lanes]


def _const_spec(shape):
    nd = len(shape)
    return pl.BlockSpec(shape, lambda *_: (0,) * nd, pipeline_mode=pl.Buffered(1))


def _ffn1_inproj(x2d, g1, wg, wu, wd, gm, wc, wvt, wo, wna, wgr, bgr, ones, gq, gk):
    n = x2d.shape[0]
    tm = FFN_TM
    W = ML_WIDTH
    tok = lambda w: pl.BlockSpec((tm, w), lambda i: (i, 0))
    out_shape = (
        jax.ShapeDtypeStruct((n, D_MODEL), F32),
        jax.ShapeDtypeStruct((n, W), F32),
        jax.ShapeDtypeStruct((n // ML_L, W, ML_L), BF16),
        jax.ShapeDtypeStruct((n, W), F32),
        jax.ShapeDtypeStruct((2, n // ML_L, SUBLANES, ML_L), F32),
        jax.ShapeDtypeStruct((2, n // ML_L, SUBLANES, ML_L), F32),
        jax.ShapeDtypeStruct((n, W), BF16),
        jax.ShapeDtypeStruct((n, W), BF16),
        jax.ShapeDtypeStruct((n, W), BF16),
    )
    out_specs = (
        tok(D_MODEL), tok(W), pl.BlockSpec((tm // ML_L, W, ML_L), lambda i: (i, 0, 0)), tok(W),
        pl.BlockSpec((2, tm // ML_L, SUBLANES, ML_L), lambda i: (0, i, 0, 0)),
        pl.BlockSpec((2, tm // ML_L, SUBLANES, ML_L), lambda i: (0, i, 0, 0)),
        tok(W), tok(W), tok(W),
    )
    in_specs = [tok(D_MODEL)] + [_const_spec(a.shape) for a in
                                 (g1, wg, wu, wd, gm, wc, wvt, wo, wna, wgr, bgr, ones, gq, gk)]
    return pl.pallas_call(
        _ffn1_inproj_kernel,
        grid=(n // tm,),
        in_specs=in_specs,
        out_specs=out_specs,
        out_shape=out_shape,
        compiler_params=pltpu.CompilerParams(
            dimension_semantics=("parallel",), vmem_limit_bytes=VMEM_LIMIT),
        name="ffn1_inproj",
    )(x2d, g1, wg, wu, wd, gm, wc, wvt, wo, wna, wgr, bgr, ones, gq, gk)


def _conv_qk_kernel(c_ref, prev_ref, next_ref, cw_ref, cb_ref, wqt_ref, wk_ref, qt_ref, k_ref, pad_ref):
    i = pl.program_id(1)
    nb = pl.num_programs(1)
    tt = CONV_TT
    half = CONV_W // 2
    pad_ref[0:SUBLANES, :] = jnp.where(i > 0, prev_ref[...], 0.0)
    pad_ref[SUBLANES:SUBLANES + tt, :] = c_ref[...]
    pad_ref[SUBLANES + tt:2 * SUBLANES + tt, :] = jnp.where(i < nb - 1, next_ref[...], 0.0)
    y = cb_ref[...]
    for kk in range(CONV_W):
        start = SUBLANES + kk - half
        y = y + pad_ref[start:start + tt, :] * cw_ref[kk:kk + 1, :]
    u = y * jax.nn.sigmoid(y)
    for h in range(ML_HEADS):
        sl = slice(h * ML_DH, (h + 1) * ML_DH)
        uh = u[:, sl].astype(BF16)
        k_ref[:, sl] = (_dot(uh, wk_ref[h]) * (ML_DH ** -0.5)).astype(BF16)
        qt = _dot_nt(wqt_ref[h], uh).astype(BF16)
        for cidx in range(tt // ML_L):
            qt_ref[cidx, sl, :] = qt[:, cidx * ML_L:(cidx + 1) * ML_L]


def _conv_qk(c3, cw, cb, wqt, wk):
    b, t, w = c3.shape
    tt = CONV_TT
    nb = t // tt
    per = tt // SUBLANES
    last = t // SUBLANES - 1
    tile = pl.BlockSpec((None, tt, w), lambda bi, i: (bi, i, 0))
    prev = pl.BlockSpec((None, SUBLANES, w), lambda bi, i: (bi, jnp.maximum(i * per - 1, 0), 0))
    nxt = pl.BlockSpec((None, SUBLANES, w), lambda bi, i: (bi, jnp.minimum((i + 1) * per, last), 0))
    return pl.pallas_call(
        _conv_qk_kernel,
        grid=(b, nb),
        in_specs=[tile, prev, nxt] + [
            pl.BlockSpec(a.shape, functools.partial(lambda nd, bi, i: (0,) * nd, a.ndim))
            for a in (cw, cb, wqt, wk)],
        out_specs=(pl.BlockSpec((None, tt // ML_L, w, ML_L), lambda bi, i: (bi, i, 0, 0)), tile),
        out_shape=(jax.ShapeDtypeStruct((b, t // ML_L, w, ML_L), BF16),
                   jax.ShapeDtypeStruct((b, t, w), BF16)),
        scratch_shapes=[pltpu.VMEM((tt + 2 * SUBLANES, w), F32)],
        compiler_params=pltpu.CompilerParams(dimension_semantics=("parallel", "parallel")),
        name="conv_qk",
    )(c3, c3, c3, cw, cb, wqt, wk)


def _lane_scan(x, pos, d, combine, ident):
    L = x.shape[-1]
    k = 1
    while k < L:
        shifted = pltpu.roll(x, jnp.where(d == 0, k, L - k), axis=1)
        x = combine(x, jnp.where(pos >= k, shifted, ident))
        k *= 2
    return x


def _mlstm_gates_kernel(ga_ref, gb_ref, rowq_ref, colq_ref, b_s, g_s, mloc_s, rmax_s, w_s, vec_s):
    d = pl.program_id(1)
    R = SUBLANES
    L = ML_L
    nc = ga_ref.shape[0] // R
    xa = ga_ref[...]
    xb = gb_ref[...]
    lane = lax.broadcasted_iota(jnp.int32, xa.shape, 1)
    pos = lane + d * (L - 1 - 2 * lane)
    b = _lane_scan(xb, pos, d, jnp.add, 0.0)
    g = jnp.broadcast_to(jnp.sum(xb, axis=1, keepdims=True), xb.shape)
    w = xa - b
    a = g + w
    m_loc = jnp.broadcast_to(jnp.max(a, axis=1, keepdims=True), a.shape)
    rowq_ref[3] = jnp.exp(a - m_loc)
    b_s[...] = b
    g_s[...] = g
    mloc_s[...] = m_loc
    rmax_s[...] = _lane_scan(w, pos, d, jnp.maximum, NEG)
    w_s[...] = w
    vec_s[...] = jnp.zeros_like(vec_s)

    def chunk(k, m0):
        ck = k + d * (nc - 1 - 2 * k)
        rows = pl.ds(pl.multiple_of(ck * R, R), R)
        gk = g_s[rows, :]
        mlk = mloc_s[rows, :]
        mm = jnp.maximum(m0, rmax_s[rows, :])
        m_new = jnp.maximum(gk + m0, mlk)
        rowq_ref[0, rows, :] = mm
        rowq_ref[1, rows, :] = jnp.exp(m0 - mm)
        rowq_ref[2, rows, :] = jnp.exp(-(b_s[rows, :] + mm))
        rowq_ref[4, rows, :] = jnp.exp(gk + m0 - m_new)
        rowq_ref[5, rows, :] = jnp.exp(mlk - m_new)
        vec_s[ck, 0:R, :] = w_s[rows, :]
        colq_ref[ck] = vec_s[ck].T
        return m_new

    lax.fori_loop(0, nc, chunk, jnp.zeros((R, L), F32), unroll=8)


def _mlstm_gates(ga, gb, b, t):
    nc = t // ML_L
    R = SUBLANES
    in_spec = pl.BlockSpec((None, nc * R, ML_L), lambda bi, d: (d, bi, 0))
    return pl.pallas_call(
        _mlstm_gates_kernel,
        grid=(b, 2),
        in_specs=[in_spec, in_spec],
        out_specs=(pl.BlockSpec((None, None, ML_NROWQ, nc * R, ML_L), lambda bi, d: (bi, d, 0, 0, 0)),
                   pl.BlockSpec((None, None, nc, ML_L, LANES), lambda bi, d: (bi, d, 0, 0, 0))),
        out_shape=(jax.ShapeDtypeStruct((b, 2, ML_NROWQ, nc * R, ML_L), F32),
                   jax.ShapeDtypeStruct((b, 2, nc, ML_L, LANES), F32)),
        scratch_shapes=[pltpu.VMEM((nc * R, ML_L), F32)] * 5 + [pltpu.VMEM((nc, LANES, ML_L), F32)],
        compiler_params=pltpu.CompilerParams(dimension_semantics=("parallel", "parallel")),
        name="mlstm_gates",
    )(ga, gb)


def _mlstm_kernel(qt_ref, k_ref, vt_ref, o_ref, rowq_ref, colq_ref, gnb_ref, y_ref,
                  hf_ref, state_ref, sq_s, upd_s, *, nblk):
    d = pl.program_id(1)
    i = pl.program_id(2)
    blk = i + d * (nblk - 1 - 2 * i)
    L = ML_L
    nch = ML_TT // L
    H = ML_HEADS
    DH = ML_DH
    R = SUBLANES

    @pl.when(i == 0)
    def _():
        state_ref[...] = jnp.zeros_like(state_ref)

    s_idx = lax.broadcasted_iota(jnp.int32, (L, L), 0)
    t_idx = lax.broadcasted_iota(jnp.int32, (L, L), 1)
    fwd = d == 0
    mask = (t_idx - s_idx) * (1 - 2 * d) >= 0
    ones = jnp.ones((ML_NONES, L), BF16)
    NA = DH + ML_NONES
    c_total = hf_ref.shape[0] - nch
    hbase = jnp.where(fwd, blk * nch, c_total)

    def chunk_of(j):
        cj = j + d * (nch - 1 - 2 * j)
        return cj, pl.ds(pl.multiple_of(cj * R, R), R)

    for j in range(nch):
        cj, rows = chunk_of(j)
        off = pl.multiple_of(cj * L, L)
        mm = rowq_ref[0, rows, :]
        sc = rowq_ref[1, rows, :]
        w_end = rowq_ref[3, rows, :]
        cols = colq_ref[cj]
        for h in range(H):
            sl = slice(h * DH, (h + 1) * DH)
            k = k_ref[pl.ds(off, L), sl]
            qt = qt_ref[cj, sl, :]
            vaug = jnp.concatenate([vt_ref[cj, sl, :], ones], axis=0)
            e = jnp.exp(jnp.where(mask, cols[:, h:h + 1] - mm[h:h + 1, :], NEG))
            sq_s[j * H + h, 0:L, :] = (_dot(k, qt) * e).astype(BF16)
            sq_s[j * H + h, L:L + DH, :] = (qt.astype(F32) * sc[h:h + 1, :]).astype(BF16)
            vw = (vaug.astype(F32) * w_end[h:h + 1, :]).astype(BF16)
            upd_s[j * H + h] = _dot(vw, k)

    for j in range(nch):
        cj, rows = chunk_of(j)
        em = rowq_ref[2, rows, :]
        s_old = rowq_ref[4, rows, :]
        s_loc = rowq_ref[5, rows, :]
        for h in range(H):
            sl = slice(h * DH, (h + 1) * DH)
            vaug = jnp.concatenate([vt_ref[cj, sl, :], ones], axis=0)
            st = state_ref[h]
            r = _dot(jnp.concatenate([vaug, st.astype(BF16)], axis=1), sq_s[j * H + h])
            inv = 1.0 / jnp.maximum(jnp.abs(r[DH:DH + 1, :]), em[h:h + 1, :])
            hf_ref[hbase + cj, sl, :] = r[:DH, :] * inv
            state_ref[h] = (jnp.broadcast_to(s_old[h:h + 1, :], (NA, DH)) * st
                            + jnp.broadcast_to(s_loc[h:h + 1, :], (NA, DH)) * upd_s[j * H + h])

    @pl.when(jnp.logical_not(fwd))
    def _():
        for c in range(nch):
            for h in range(H):
                sl = slice(h * DH, (h + 1) * DH)
                tot = hf_ref[blk * nch + c, sl, :] + hf_ref[c_total + c, sl, :]
                ms = jnp.mean(tot * tot, axis=0, keepdims=True)
                hn = (tot * lax.rsqrt(ms + EPS) * gnb_ref[sl, :]).T
                tok = slice(c * L, (c + 1) * L)
                y_ref[tok, sl] = (jax.nn.sigmoid(o_ref[tok, sl]) * hn).astype(BF16)


def _mlstm(qt4, k3, vt4, o3, rowq, colq, gnb):
    b, t, w = k3.shape
    tt = ML_TT
    nblk = t // tt
    nch = tt // ML_L
    blk_of = lambda d, i: i + d * (nblk - 1 - 2 * i)
    tile = pl.BlockSpec((None, tt, w), lambda bi, d, i: (bi, blk_of(d, i), 0))
    ft_spec = pl.BlockSpec((None, nch, w, ML_L), lambda bi, d, i: (bi, blk_of(d, i), 0, 0))
    rq_spec = pl.BlockSpec((None, None, ML_NROWQ, nch * SUBLANES, ML_L),
                           lambda bi, d, i: (bi, d, 0, blk_of(d, i), 0))
    cq_spec = pl.BlockSpec((None, None, nch, ML_L, LANES), lambda bi, d, i: (bi, d, blk_of(d, i), 0, 0))
    gn_spec = pl.BlockSpec(gnb.shape, lambda bi, d, i: (0, 0))
    y_spec = pl.BlockSpec((None, tt, w), lambda bi, d, i: (bi, d * blk_of(d, i) + (1 - d) * (nblk - 1), 0))
    return pl.pallas_call(
        functools.partial(_mlstm_kernel, nblk=nblk),
        grid=(b, 2, nblk),
        in_specs=[ft_spec, tile, ft_spec, tile, rq_spec, cq_spec, gn_spec],
        out_specs=y_spec,
        out_shape=jax.ShapeDtypeStruct((b, t, w), BF16),
        scratch_shapes=[
            pltpu.VMEM((t // ML_L + nch, w, ML_L), F32),
            pltpu.VMEM((ML_HEADS, ML_DH + ML_NONES, ML_DH), F32),
            pltpu.VMEM((nch * ML_HEADS, ML_L + ML_DH, ML_L), BF16),
            pltpu.VMEM((nch * ML_HEADS, ML_DH + ML_NONES, ML_DH), F32),
        ],
        compiler_params=pltpu.CompilerParams(
            dimension_semantics=("arbitrary", "arbitrary", "arbitrary"), vmem_limit_bytes=VMEM_LIMIT),
        name="mlstm",
    )(qt4, k3, vt4, o3, rowq, colq, gnb)


def _na_kernel(q_ref, k_ref, v_ref, toep_ref, out_ref, bias_ref, *, rows):
    r0 = pl.program_id(1) * NA_R
    nkeys = NA_KH * GRID_W
    gw = NA_PACK * NA_DH

    @pl.when(pl.program_id(1) == 0)
    def _():
        for delta in range(NA_KH):
            lo = (NA_KH - 1 - delta) * GRID_W
            for h in range(NA_HEADS):
                g, hh = divmod(h, NA_PACK)
                bias_ref[delta, g, hh * GRID_W:(hh + 1) * GRID_W, :] = toep_ref[h, :, lo:lo + nkeys]

    row_blk = lax.broadcasted_iota(jnp.int32, (NA_PACK * GRID_W, gw), 0) // GRID_W
    lane_blk = lax.broadcasted_iota(jnp.int32, (NA_PACK * GRID_W, gw), 1) // NA_DH
    diag = row_blk == lane_blk
    out_blk = lax.broadcasted_iota(jnp.int32, (GRID_W, gw), 1) // NA_DH

    ng = NA_HEADS // NA_PACK

    def window(ri):
        r = r0 + ri
        rs = jnp.clip(r - NA_KH // 2, 0, rows - NA_KH)
        return pl.ds(pl.multiple_of(rs * GRID_W, GRID_W), nkeys), r - rs

    def row_body(ri, carry):
        keys, delta = window(ri)
        qoff = pl.multiple_of(ri * GRID_W, GRID_W)
        outs = []
        for g in range(ng):
            sl = slice(g * gw, (g + 1) * gw)
            q4 = q_ref[pl.ds(qoff, GRID_W), sl]
            qbd = jnp.where(diag, jnp.concatenate([q4] * NA_PACK, axis=0), jnp.zeros((), BF16))
            s = _dot_nt(qbd, k_ref[keys, sl]) + bias_ref[delta, g]
            p = jnp.exp2(s - jnp.max(s, axis=-1, keepdims=True))
            linv = 1.0 / jnp.sum(p, axis=-1, keepdims=True)
            o = _dot(p.astype(BF16), v_ref[keys, sl]) * linv
            og = o[(NA_PACK - 1) * GRID_W:, :]
            for h in range(NA_PACK - 2, -1, -1):
                og = jnp.where(out_blk == h, o[h * GRID_W:(h + 1) * GRID_W, :], og)
            outs.append(og)
        out_ref[pl.ds(qoff, GRID_W), :] = jnp.concatenate(outs, axis=-1).astype(BF16)
        return carry

    lax.fori_loop(0, NA_R, row_body, 0, unroll=True)


def _na(q3, k3, v3, toep):
    b, t, w = q3.shape
    rows = t // GRID_W
    tq = NA_R * GRID_W
    qtile = pl.BlockSpec((None, tq, w), lambda bi, i: (bi, i, 0))
    seq = pl.BlockSpec((None, t, w), lambda bi, i: (bi, 0, 0), pipeline_mode=pl.Buffered(1))
    toep_spec = pl.BlockSpec(toep.shape, lambda bi, i: (0, 0, 0), pipeline_mode=pl.Buffered(1))
    return pl.pallas_call(
        functools.partial(_na_kernel, rows=rows),
        grid=(b, rows // NA_R),
        in_specs=[qtile, seq, seq, toep_spec],
        out_specs=qtile,
        out_shape=jax.ShapeDtypeStruct((b, t, w), BF16),
        scratch_shapes=[pltpu.VMEM((NA_KH, NA_HEADS // NA_PACK, NA_PACK * GRID_W, NA_KH * GRID_W), F32)],
        compiler_params=pltpu.CompilerParams(
            dimension_semantics=("arbitrary", "arbitrary"), vmem_limit_bytes=VMEM_LIMIT),
        name="natten",
    )(q3, k3, v3, toep)


def _na_bias_table(rpb):
    c = np.arange(GRID_W)
    cs = np.clip(c - NA_KW // 2, 0, GRID_W - NA_KW)
    cc = np.arange(GRID_W)
    valid = (cc[None, :] >= cs[:, None]) & (cc[None, :] < cs[:, None] + NA_KW)
    nh, ndr, ndc = rpb.shape
    lead = GRID_W - NA_KW
    w = jnp.pad(rpb.astype(F32), ((0, 0), (0, 0), (lead, 2 * GRID_W - lead - ndc)))
    skew = jnp.broadcast_to(w[:, :, None, :], (nh, ndr, GRID_W, 2 * GRID_W)).reshape(nh, ndr, -1)
    skew = skew[:, :, :GRID_W * (2 * GRID_W - 1)].reshape(nh, ndr, GRID_W, 2 * GRID_W - 1)
    toep = jnp.where(valid[None, None], skew[..., GRID_W - 1:] * LOG2E, NEG).transpose(0, 2, 1, 3)
    return toep.reshape(nh, GRID_W, ndr * GRID_W)


def _outproj_ffn2_kernel(x1_ref, yml_ref, yna_ref, woml_ref, wona_ref, g2_ref, wg_ref, wu_ref, wd_ref,
                         gf_ref, out_ref):
    x2 = x1_ref[...] + _dot(yml_ref[...], woml_ref[...]) + _dot(yna_ref[...], wona_ref[...])
    h = _rms(x2, g2_ref[...]).astype(BF16)
    x3 = x2 + _swiglu_half(h, wg_ref, wu_ref, wd_ref)
    out_ref[...] = _rms(x3, gf_ref[...])


def _outproj_ffn2(x1, yml, yna, woml, wona, g2, wg, wu, wd, gf):
    n = x1.shape[0]
    tm = FFN_TM
    tok = lambda w: pl.BlockSpec((tm, w), lambda i: (i, 0))
    return pl.pallas_call(
        _outproj_ffn2_kernel,
        grid=(n // tm,),
        in_specs=[tok(D_MODEL), tok(ML_WIDTH), tok(NA_WIDTH)] + [
            _const_spec(a.shape) for a in (woml, wona, g2, wg, wu, wd, gf)],
        out_specs=tok(D_MODEL),
        out_shape=jax.ShapeDtypeStruct((n, D_MODEL), F32),
        compiler_params=pltpu.CompilerParams(
            dimension_semantics=("parallel",), vmem_limit_bytes=VMEM_LIMIT),
        name="outproj_ffn2",
    )(x1, yml, yna, woml, wona, g2, wg, wu, wd, gf)


def _layer(x, norm_ffn1, w1_gate, w1_up, w1_down, norm_mix, w_in, b_gates, conv_w, conv_b, w_q_ml,
           w_k_ml, gn_ml, gq_na, gk_na, rpb, w_out, norm_ffn2, w2_gate, w2_up, w2_down, norm_final):
    b, t, dm = x.shape
    n = b * t
    W = ML_WIDTH
    row = lambda a: a.reshape(1, -1).astype(F32)
    g0 = 3 * W
    wc = w_in[:, :W].astype(BF16)
    wvt = w_in[:, W:2 * W].T.astype(BF16)
    wo = w_in[:, 2 * W:g0].astype(BF16)
    wna = w_in[:, g0 + N_GATES:].astype(BF16)
    wgate = w_in[:, g0:g0 + N_GATES]
    H = ML_HEADS
    order = np.concatenate([dd * 2 * H + np.r_[0:2 * H, H:2 * H, 0:H] for dd in range(2)])
    wgr = wgate.T[order].astype(BF16)
    bgr = b_gates[order].reshape(-1, 1).astype(F32)
    ones = jnp.asarray(np.kron(np.eye(NA_HEADS), np.ones((NA_DH, NA_DH))), BF16)
    gq = jnp.tile(gq_na.astype(F32), NA_HEADS).reshape(1, -1)
    gk = jnp.tile(gk_na.astype(F32), NA_HEADS).reshape(1, -1)

    x1, c, vt_ml, o, ga, gb, qn, kn, vn = _ffn1_inproj(
        x.reshape(n, dm), row(norm_ffn1), w1_gate.astype(BF16), w1_up.astype(BF16), w1_down.astype(BF16),
        row(norm_mix), wc, wvt, wo, wna, wgr, bgr, ones, gq, gk)

    seq = lambda a: a.reshape(b, t, a.shape[-1])
    qt_ml, k_ml = _conv_qk(seq(c), conv_w.astype(F32), row(conv_b),
                           w_q_ml.transpose(0, 2, 1).astype(BF16), w_k_ml.astype(BF16))
    rowq, colq = _mlstm_gates(ga.reshape(2, -1, ML_L), gb.reshape(2, -1, ML_L), b, t)
    gnb = jnp.broadcast_to(gn_ml.astype(F32).reshape(W, 1), (W, ML_L))
    y_ml = _mlstm(qt_ml, k_ml, vt_ml.reshape(b, t // ML_L, W, ML_L), seq(o), rowq, colq, gnb)
    y_na = _na(seq(qn), seq(kn), seq(vn), _na_bias_table(rpb))

    out = _outproj_ffn2(x1, y_ml.reshape(n, W), y_na.reshape(n, NA_WIDTH),
                        w_out[:W].astype(BF16), w_out[W:].astype(BF16), row(norm_ffn2),
                        w2_gate.astype(BF16), w2_up.astype(BF16), w2_down.astype(BF16), row(norm_final))
    return out.reshape(b, t, dm)


def kernel(x, norm_ffn1, w1_gate, w1_up, w1_down, norm_mix, w_in, b_gates, conv_w, conv_b, w_q_ml, w_k_ml,
           gn_ml, gq_na, gk_na, rpb, w_out, norm_ffn2, w2_gate, w2_up, w2_down, norm_final):
    depth = norm_ffn1.shape[0]
    for l in range(depth):
        x = _layer(x, norm_ffn1[l], w1_gate[l], w1_up[l], w1_down[l], norm_mix[l], w_in[l], b_gates[l],
                   conv_w[l], conv_b[l], w_q_ml[l], w_k_ml[l], gn_ml[l], gq_na[l], gk_na[l], rpb[l],
                   w_out[l], norm_ffn2[l], w2_gate[l], w2_up[l], w2_down[l], norm_final[l])
    return x
```

```python
import functools

import jax
import jax.numpy as jnp
import numpy as np
from jax import lax
from jax.experimental import pallas as pl
from jax.experimental.pallas import tpu as pltpu

F32 = jnp.float32
BF16 = jnp.bfloat16

D_MODEL = 1024
D_FF = 2816
GRID_W = 64
ML_HEADS = 4
ML_DH = 128
ML_WIDTH = ML_HEADS * ML_DH
CONV_W = 5
NA_HEADS = 8
NA_DH = 64
NA_WIDTH = NA_HEADS * NA_DH
NA_KH = 8
NA_KW = 16
N_GATES = 4 * ML_HEADS
EPS = 1e-6
NEG = -1e30
LOG2E = 1.4426950408889634

SUBLANES = 8
LANES = 128

FFN_TM = 512
FFN_CK = 2816
ML_L = 128
ML_TT = 512
ML_NROWQ = 6
ML_NONES = 16
CONV_TT = 512
NA_R = 16
NA_PACK = 4
VMEM_LIMIT = 56 * 1024 * 1024


def _dot(a, b):
    return jnp.dot(a, b, preferred_element_type=F32)


def _dot_nt(a, b):
    return lax.dot_general(a, b, (((1,), (1,)), ((), ())), preferred_element_type=F32)


def _dot_tn(a, b):
    return lax.dot_general(a, b, (((0,), (0,)), ((), ())), preferred_element_type=F32)


def _dot_exact(a, b):
    return jnp.dot(a, b, preferred_element_type=F32, precision=lax.Precision.HIGHEST)


def _rms(x, g):
    ms = jnp.mean(x * x, axis=-1, keepdims=True)
    return x * lax.rsqrt(ms + EPS) * g


def _log_sigmoid(x):
    return jnp.minimum(x, 0.0) - jnp.log1p(jnp.exp(-jnp.abs(x)))


def _swiglu_half(h, wg_ref, wu_ref, wd_ref):
    acc = None
    for j in range(D_FF // FFN_CK):
        sl = slice(j * FFN_CK, (j + 1) * FFN_CK)
        g = _dot(h, wg_ref[:, sl])
        u = _dot(h, wu_ref[:, sl])
        a = (g * jax.nn.sigmoid(g) * u).astype(BF16)
        part = _dot(a, wd_ref[sl, :])
        acc = part if acc is None else acc + part
    return 0.5 * acc


def _group_mean_sq(x, ones_ref):
    xx = x * x
    return _dot(xx.astype(BF16), ones_ref[...]) * (1.0 / NA_DH)


def _ffn1_inproj_kernel(x_ref, g1_ref, wg_ref, wu_ref, wd_ref, gm_ref, wc_ref, wvg_ref, wo_ref, wna_ref,
                        bgr_ref, ones_ref, gq_ref, gk_ref,
                        x1_ref, c_ref, vt_ref, o_ref, ga_ref, gb_ref, qn_ref, kn_ref, vn_ref):
    x = x_ref[...]
    h = _rms(x, g1_ref[...]).astype(BF16)
    x1 = x + _swiglu_half(h, wg_ref, wu_ref, wd_ref)
    x1_ref[...] = x1
    h2 = _rms(x1, gm_ref[...]).astype(BF16)
    W = ML_WIDTH
    c_ref[...] = _dot(h2, wc_ref[...])
    o_ref[...] = _dot(h2, wo_ref[...])
    vg = _dot_nt(wvg_ref[...], h2)
    vt = vg[:W].astype(BF16)
    for cidx in range(FFN_TM // ML_L):
        vt_ref[cidx] = vt[:, cidx * ML_L:(cidx + 1) * ML_L]
    p = _dot(h2, wna_ref[...])
    q = p[:, 0:W]
    k = p[:, W:2 * W]
    vn_ref[...] = p[:, 2 * W:3 * W].astype(BF16)
    qn = q * lax.rsqrt(_group_mean_sq(q, ones_ref) + EPS) * gq_ref[...] * (NA_DH ** -0.5 * LOG2E)
    kn = k * lax.rsqrt(_group_mean_sq(k, ones_ref) + EPS) * gk_ref[...]
    qn_ref[...] = qn.astype(BF16)
    kn_ref[...] = kn.astype(BF16)
    gr = vg[W:] + bgr_ref[...]
    row = lax.broadcasted_iota(jnp.int32, gr.shape, 0) % (4 * ML_HEADS)
    gr = jnp.where((row >= ML_HEADS) & (row < 3 * ML_HEADS), _log_sigmoid(gr), gr)
    for dd in range(2):
        for cidx in range(FFN_TM // ML_L):
            lanes = slice(cidx * ML_L, (cidx + 1) * ML_L)
            ga_ref[dd, cidx] = gr[dd * 16:dd * 16 + 8, lanes]
            gb_ref[dd, cidx] = gr[dd * 16 + 8:dd * 16 + 16, lanes]


def _const_spec(shape):
    nd = len(shape)
    return pl.BlockSpec(shape, lambda *_: (0,) * nd, pipeline_mode=pl.Buffered(1))


def _ffn1_inproj(x2d, g1, wg, wu, wd, gm, wc, wvg, wo, wna, bgr, ones, gq, gk):
    n = x2d.shape[0]
    tm = FFN_TM
    W = ML_WIDTH
    tok = lambda w: pl.BlockSpec((tm, w), lambda i: (i, 0))
    out_shape = (
        jax.ShapeDtypeStruct((n, D_MODEL), F32),
        jax.ShapeDtypeStruct((n, W), F32),
        jax.ShapeDtypeStruct((n // ML_L, W, ML_L), BF16),
        jax.ShapeDtypeStruct((n, W), F32),
        jax.ShapeDtypeStruct((2, n // ML_L, SUBLANES, ML_L), F32),
        jax.ShapeDtypeStruct((2, n // ML_L, SUBLANES, ML_L), F32),
        jax.ShapeDtypeStruct((n, W), BF16),
        jax.ShapeDtypeStruct((n, W), BF16),
        jax.ShapeDtypeStruct((n, W), BF16),
    )
    out_specs = (
        tok(D_MODEL), tok(W), pl.BlockSpec((tm // ML_L, W, ML_L), lambda i: (i, 0, 0)), tok(W),
        pl.BlockSpec((2, tm // ML_L, SUBLANES, ML_L), lambda i: (0, i, 0, 0)),
        pl.BlockSpec((2, tm // ML_L, SUBLANES, ML_L), lambda i: (0, i, 0, 0)),
        tok(W), tok(W), tok(W),
    )
    in_specs = [tok(D_MODEL)] + [_const_spec(a.shape) for a in
                                 (g1, wg, wu, wd, gm, wc, wvg, wo, wna, bgr, ones, gq, gk)]
    return pl.pallas_call(
        _ffn1_inproj_kernel,
        grid=(n // tm,),
        in_specs=in_specs,
        out_specs=out_specs,
        out_shape=out_shape,
        compiler_params=pltpu.CompilerParams(
            dimension_semantics=("parallel",), vmem_limit_bytes=VMEM_LIMIT),
        name="ffn1_inproj",
    )(x2d, g1, wg, wu, wd, gm, wc, wvg, wo, wna, bgr, ones, gq, gk)


def _conv_qk_kernel(c_ref, prev_ref, next_ref, cw_ref, cb_ref, wqt_ref, wk_ref, qt_ref, k_ref, pad_ref):
    i = pl.program_id(1)
    nb = pl.num_programs(1)
    tt = CONV_TT
    half = CONV_W // 2
    pad_ref[0:SUBLANES, :] = jnp.where(i > 0, prev_ref[...], 0.0)
    pad_ref[SUBLANES:SUBLANES + tt, :] = c_ref[...]
    pad_ref[SUBLANES + tt:2 * SUBLANES + tt, :] = jnp.where(i < nb - 1, next_ref[...], 0.0)
    y = cb_ref[...]
    for kk in range(CONV_W):
        start = SUBLANES + kk - half
        y = y + pad_ref[start:start + tt, :] * cw_ref[kk:kk + 1, :]
    u = y * jax.nn.sigmoid(y)
    for h in range(ML_HEADS):
        sl = slice(h * ML_DH, (h + 1) * ML_DH)
        uh = u[:, sl].astype(BF16)
        k_ref[:, sl] = (_dot(uh, wk_ref[h]) * (ML_DH ** -0.5)).astype(BF16)
        qt = _dot_nt(wqt_ref[h], uh).astype(BF16)
        for cidx in range(tt // ML_L):
            qt_ref[cidx, sl, :] = qt[:, cidx * ML_L:(cidx + 1) * ML_L]


def _conv_qk(c3, cw, cb, wqt, wk):
    b, t, w = c3.shape
    tt = CONV_TT
    nb = t // tt
    per = tt // SUBLANES
    last = t // SUBLANES - 1
    tile = pl.BlockSpec((None, tt, w), lambda bi, i: (bi, i, 0))
    prev = pl.BlockSpec((None, SUBLANES, w), lambda bi, i: (bi, jnp.maximum(i * per - 1, 0), 0))
    nxt = pl.BlockSpec((None, SUBLANES, w), lambda bi, i: (bi, jnp.minimum((i + 1) * per, last), 0))
    return pl.pallas_call(
        _conv_qk_kernel,
        grid=(b, nb),
        in_specs=[tile, prev, nxt] + [
            pl.BlockSpec(a.shape, functools.partial(lambda nd, bi, i: (0,) * nd, a.ndim))
            for a in (cw, cb, wqt, wk)],
        out_specs=(pl.BlockSpec((None, tt // ML_L, w, ML_L), lambda bi, i: (bi, i, 0, 0)), tile),
        out_shape=(jax.ShapeDtypeStruct((b, t // ML_L, w, ML_L), BF16),
                   jax.ShapeDtypeStruct((b, t, w), BF16)),
        scratch_shapes=[pltpu.VMEM((tt + 2 * SUBLANES, w), F32)],
        compiler_params=pltpu.CompilerParams(dimension_semantics=("parallel", "parallel")),
        name="conv_qk",
    )(c3, c3, c3, cw, cb, wqt, wk)


def _lane_scan(x, pos, d, combine, ident):
    L = x.shape[-1]
    k = 1
    while k < L:
        shifted = pltpu.roll(x, jnp.where(d == 0, k, L - k), axis=1)
        x = combine(x, jnp.where(pos >= k, shifted, ident))
        k *= 2
    return x


def _mlstm_gates_kernel(ga_ref, gb_ref, rowq_ref, colq_ref, b_s, g_s, mloc_s, rmax_s, w_s, vec_s):
    d = pl.program_id(1)
    R = SUBLANES
    L = ML_L
    nc = ga_ref.shape[0] // R
    xa = ga_ref[...]
    xb = gb_ref[...]
    lane = lax.broadcasted_iota(jnp.int32, xa.shape, 1)
    pos = lane + d * (L - 1 - 2 * lane)
    b = _lane_scan(xb, pos, d, jnp.add, 0.0)
    g = jnp.broadcast_to(jnp.sum(xb, axis=1, keepdims=True), xb.shape)
    w = xa - b
    a = g + w
    m_loc = jnp.broadcast_to(jnp.max(a, axis=1, keepdims=True), a.shape)
    rowq_ref[3] = jnp.exp(a - m_loc)
    b_s[...] = b
    g_s[...] = g
    mloc_s[...] = m_loc
    rmax_s[...] = _lane_scan(w, pos, d, jnp.maximum, NEG)
    w_s[...] = w
    vec_s[...] = jnp.zeros_like(vec_s)

    def chunk(k, m0):
        ck = k + d * (nc - 1 - 2 * k)
        rows = pl.ds(pl.multiple_of(ck * R, R), R)
        gk = g_s[rows, :]
        mlk = mloc_s[rows, :]
        mm = jnp.maximum(m0, rmax_s[rows, :])
        m_new = jnp.maximum(gk + m0, mlk)
        rowq_ref[0, rows, :] = mm
        rowq_ref[1, rows, :] = jnp.exp(m0 - mm)
        rowq_ref[2, rows, :] = jnp.exp(-(b_s[rows, :] + mm))
        rowq_ref[4, rows, :] = jnp.exp(gk + m0 - m_new)
        rowq_ref[5, rows, :] = jnp.exp(mlk - m_new)
        vec_s[ck, 0:R, :] = w_s[rows, :]
        colq_ref[ck] = vec_s[ck].T
        return m_new

    lax.fori_loop(0, nc, chunk, jnp.zeros((R, L), F32), unroll=8)


def _mlstm_gates(ga, gb, b, t):
    nc = t // ML_L
    R = SUBLANES
    in_spec = pl.BlockSpec((None, nc * R, ML_L), lambda bi, d: (d, bi, 0))
    return pl.pallas_call(
        _mlstm_gates_kernel,
        grid=(b, 2),
        in_specs=[in_spec, in_spec],
        out_specs=(pl.BlockSpec((None, None, ML_NROWQ, nc * R, ML_L), lambda bi, d: (bi, d, 0, 0, 0)),
                   pl.BlockSpec((None, None, nc, ML_L, LANES), lambda bi, d: (bi, d, 0, 0, 0))),
        out_shape=(jax.ShapeDtypeStruct((b, 2, ML_NROWQ, nc * R, ML_L), F32),
                   jax.ShapeDtypeStruct((b, 2, nc, ML_L, LANES), F32)),
        scratch_shapes=[pltpu.VMEM((nc * R, ML_L), F32)] * 5 + [pltpu.VMEM((nc, LANES, ML_L), F32)],
        compiler_params=pltpu.CompilerParams(dimension_semantics=("parallel", "parallel")),
        name="mlstm_gates",
    )(ga, gb)


def _mlstm_kernel(qt_ref, k_ref, vt_ref, o_ref, rowq_ref, colq_ref, gnb_ref, y_ref,
                  hf_ref, state_ref, sq_s, upd_s, *, nblk):
    d = pl.program_id(1)
    i = pl.program_id(2)
    blk = i + d * (nblk - 1 - 2 * i)
    L = ML_L
    nch = ML_TT // L
    H = ML_HEADS
    DH = ML_DH
    R = SUBLANES

    @pl.when(i == 0)
    def _():
        state_ref[...] = jnp.zeros_like(state_ref)

    s_idx = lax.broadcasted_iota(jnp.int32, (L, L), 0)
    t_idx = lax.broadcasted_iota(jnp.int32, (L, L), 1)
    fwd = d == 0
    mask = (t_idx - s_idx) * (1 - 2 * d) >= 0
    ones = jnp.ones((ML_NONES, L), BF16)
    NA = DH + ML_NONES
    c_total = hf_ref.shape[0] - nch
    hbase = jnp.where(fwd, blk * nch, c_total)

    def chunk_of(j):
        cj = j + d * (nch - 1 - 2 * j)
        return cj, pl.ds(pl.multiple_of(cj * R, R), R)

    for j in range(nch):
        cj, rows = chunk_of(j)
        off = pl.multiple_of(cj * L, L)
        mm = rowq_ref[0, rows, :]
        sc = rowq_ref[1, rows, :]
        w_end = rowq_ref[3, rows, :]
        cols = colq_ref[cj]
        for h in range(H):
            sl = slice(h * DH, (h + 1) * DH)
            k = k_ref[pl.ds(off, L), sl]
            qt = qt_ref[cj, sl, :]
            vaug = jnp.concatenate([vt_ref[cj, sl, :], ones], axis=0)
            e = jnp.exp(jnp.where(mask, cols[:, h:h + 1] - mm[h:h + 1, :], NEG))
            sq_s[j * H + h, 0:L, :] = (_dot(k, qt) * e).astype(BF16)
            sq_s[j * H + h, L:L + DH, :] = (qt.astype(F32) * sc[h:h + 1, :]).astype(BF16)
            vw = (vaug.astype(F32) * w_end[h:h + 1, :]).astype(BF16)
            upd_s[j * H + h] = _dot(vw, k)

    for j in range(nch):
        cj, rows = chunk_of(j)
        em = rowq_ref[2, rows, :]
        s_old = rowq_ref[4, rows, :]
        s_loc = rowq_ref[5, rows, :]
        for h in range(H):
            sl = slice(h * DH, (h + 1) * DH)
            vaug = jnp.concatenate([vt_ref[cj, sl, :], ones], axis=0)
            st = state_ref[h]
            r = _dot(jnp.concatenate([vaug, st.astype(BF16)], axis=1), sq_s[j * H + h])
            inv = 1.0 / jnp.maximum(jnp.abs(r[DH:DH + 1, :]), em[h:h + 1, :])
            hf_ref[hbase + cj, sl, :] = r[:DH, :] * inv
            state_ref[h] = (jnp.broadcast_to(s_old[h:h + 1, :], (NA, DH)) * st
                            + jnp.broadcast_to(s_loc[h:h + 1, :], (NA, DH)) * upd_s[j * H + h])

    @pl.when(jnp.logical_not(fwd))
    def _():
        for c in range(nch):
            for h in range(H):
                sl = slice(h * DH, (h + 1) * DH)
                tot = hf_ref[blk * nch + c, sl, :] + hf_ref[c_total + c, sl, :]
                ms = jnp.mean(tot * tot, axis=0, keepdims=True)
                hn = (tot * lax.rsqrt(ms + EPS) * gnb_ref[sl, :]).T
                tok = slice(c * L, (c + 1) * L)
                y_ref[tok, sl] = (jax.nn.sigmoid(o_ref[tok, sl]) * hn).astype(BF16)


def _mlstm(qt4, k3, vt4, o3, rowq, colq, gnb):
    b, t, w = k3.shape
    tt = ML_TT
    nblk = t // tt
    nch = tt // ML_L
    blk_of = lambda d, i: i + d * (nblk - 1 - 2 * i)
    tile = pl.BlockSpec((None, tt, w), lambda bi, d, i: (bi, blk_of(d, i), 0))
    ft_spec = pl.BlockSpec((None, nch, w, ML_L), lambda bi, d, i: (bi, blk_of(d, i), 0, 0))
    rq_spec = pl.BlockSpec((None, None, ML_NROWQ, nch * SUBLANES, ML_L),
                           lambda bi, d, i: (bi, d, 0, blk_of(d, i), 0))
    cq_spec = pl.BlockSpec((None, None, nch, ML_L, LANES), lambda bi, d, i: (bi, d, blk_of(d, i), 0, 0))
    gn_spec = pl.BlockSpec(gnb.shape, lambda bi, d, i: (0, 0))
    y_spec = pl.BlockSpec((None, tt, w), lambda bi, d, i: (bi, d * blk_of(d, i) + (1 - d) * (nblk - 1), 0))
    return pl.pallas_call(
        functools.partial(_mlstm_kernel, nblk=nblk),
        grid=(b, 2, nblk),
        in_specs=[ft_spec, tile, ft_spec, tile, rq_spec, cq_spec, gn_spec],
        out_specs=y_spec,
        out_shape=jax.ShapeDtypeStruct((b, t, w), BF16),
        scratch_shapes=[
            pltpu.VMEM((t // ML_L + nch, w, ML_L), F32),
            pltpu.VMEM((ML_HEADS, ML_DH + ML_NONES, ML_DH), F32),
            pltpu.VMEM((nch * ML_HEADS, ML_L + ML_DH, ML_L), BF16),
            pltpu.VMEM((nch * ML_HEADS, ML_DH + ML_NONES, ML_DH), F32),
        ],
        compiler_params=pltpu.CompilerParams(
            dimension_semantics=("arbitrary", "arbitrary", "arbitrary"), vmem_limit_bytes=VMEM_LIMIT),
        name="mlstm",
    )(qt4, k3, vt4, o3, rowq, colq, gnb)


def _na_kernel(q_ref, k_ref, v_ref, toep_ref, out_ref, bias_ref, *, rows):
    r0 = pl.program_id(1) * NA_R
    nkeys = NA_KH * GRID_W
    gw = NA_PACK * NA_DH

    @pl.when(pl.program_id(1) == 0)
    def _():
        for delta in range(NA_KH):
            lo = (NA_KH - 1 - delta) * GRID_W
            for h in range(NA_HEADS):
                g, hh = divmod(h, NA_PACK)
                bias_ref[delta, g, hh * GRID_W:(hh + 1) * GRID_W, :] = toep_ref[h, :, lo:lo + nkeys]

    row_blk = lax.broadcasted_iota(jnp.int32, (NA_PACK * GRID_W, gw), 0) // GRID_W
    lane_blk = lax.broadcasted_iota(jnp.int32, (NA_PACK * GRID_W, gw), 1) // NA_DH
    diag = row_blk == lane_blk
    out_blk = lax.broadcasted_iota(jnp.int32, (GRID_W, gw), 1) // NA_DH

    ng = NA_HEADS // NA_PACK

    def window(ri):
        r = r0 + ri
        rs = jnp.clip(r - NA_KH // 2, 0, rows - NA_KH)
        return pl.ds(pl.multiple_of(rs * GRID_W, GRID_W), nkeys), r - rs

    def row_body(ri, carry):
        keys, delta = window(ri)
        qoff = pl.multiple_of(ri * GRID_W, GRID_W)
        outs = []
        for g in range(ng):
            sl = slice(g * gw, (g + 1) * gw)
            q4 = q_ref[pl.ds(qoff, GRID_W), sl]
            qbd = jnp.where(diag, jnp.concatenate([q4] * NA_PACK, axis=0), jnp.zeros((), BF16))
            s = _dot_nt(qbd, k_ref[keys, sl]) + bias_ref[delta, g]
            p = jnp.exp2(s - jnp.max(s, axis=-1, keepdims=True))
            linv = 1.0 / jnp.sum(p, axis=-1, keepdims=True)
            o = _dot(p.astype(BF16), v_ref[keys, sl]) * linv
            og = o[(NA_PACK - 1) * GRID_W:, :]
            for h in range(NA_PACK - 2, -1, -1):
                og = jnp.where(out_blk == h, o[h * GRID_W:(h + 1) * GRID_W, :], og)
            outs.append(og)
        out_ref[pl.ds(qoff, GRID_W), :] = jnp.concatenate(outs, axis=-1).astype(BF16)
        return carry

    lax.fori_loop(0, NA_R, row_body, 0, unroll=True)


def _na(q3, k3, v3, toep):
    b, t, w = q3.shape
    rows = t // GRID_W
    tq = NA_R * GRID_W
    qtile = pl.BlockSpec((None, tq, w), lambda bi, i: (bi, i, 0))
    seq = pl.BlockSpec((None, t, w), lambda bi, i: (bi, 0, 0), pipeline_mode=pl.Buffered(1))
    toep_spec = pl.BlockSpec(toep.shape, lambda bi, i: (0, 0, 0), pipeline_mode=pl.Buffered(1))
    return pl.pallas_call(
        functools.partial(_na_kernel, rows=rows),
        grid=(b, rows // NA_R),
        in_specs=[qtile, seq, seq, toep_spec],
        out_specs=qtile,
        out_shape=jax.ShapeDtypeStruct((b, t, w), BF16),
        scratch_shapes=[pltpu.VMEM((NA_KH, NA_HEADS // NA_PACK, NA_PACK * GRID_W, NA_KH * GRID_W), F32)],
        compiler_params=pltpu.CompilerParams(
            dimension_semantics=("arbitrary", "arbitrary"), vmem_limit_bytes=VMEM_LIMIT),
        name="natten",
    )(q3, k3, v3, toep)


def _na_bias_table(rpb):
    c = np.arange(GRID_W)
    cs = np.clip(c - NA_KW // 2, 0, GRID_W - NA_KW)
    cc = np.arange(GRID_W)
    valid = (cc[None, :] >= cs[:, None]) & (cc[None, :] < cs[:, None] + NA_KW)
    nh, ndr, ndc = rpb.shape
    lead = GRID_W - NA_KW
    w = jnp.pad(rpb.astype(F32), ((0, 0), (0, 0), (lead, 2 * GRID_W - lead - ndc)))
    skew = jnp.broadcast_to(w[:, :, None, :], (nh, ndr, GRID_W, 2 * GRID_W)).reshape(nh, ndr, -1)
    skew = skew[:, :, :GRID_W * (2 * GRID_W - 1)].reshape(nh, ndr, GRID_W, 2 * GRID_W - 1)
    toep = jnp.where(valid[None, None], skew[..., GRID_W - 1:] * LOG2E, NEG).transpose(0, 2, 1, 3)
    return toep.reshape(nh, GRID_W, ndr * GRID_W)


def _outproj_ffn2_kernel(x1_ref, yml_ref, yna_ref, wout_ref, g2_ref, wg_ref, wu_ref, wd_ref,
                         gf_ref, out_ref):
    x2 = x1_ref[...] + _dot(jnp.concatenate([yml_ref[...], yna_ref[...]], axis=1), wout_ref[...])
    h = _rms(x2, g2_ref[...]).astype(BF16)
    x3 = x2 + _swiglu_half(h, wg_ref, wu_ref, wd_ref)
    out_ref[...] = _rms(x3, gf_ref[...])


def _outproj_ffn2(x1, yml, yna, wout, g2, wg, wu, wd, gf):
    n = x1.shape[0]
    tm = FFN_TM
    tok = lambda w: pl.BlockSpec((tm, w), lambda i: (i, 0))
    return pl.pallas_call(
        _outproj_ffn2_kernel,
        grid=(n // tm,),
        in_specs=[tok(D_MODEL), tok(ML_WIDTH), tok(NA_WIDTH)] + [
            _const_spec(a.shape) for a in (wout, g2, wg, wu, wd, gf)],
        out_specs=tok(D_MODEL),
        out_shape=jax.ShapeDtypeStruct((n, D_MODEL), F32),
        compiler_params=pltpu.CompilerParams(
            dimension_semantics=("parallel",), vmem_limit_bytes=VMEM_LIMIT),
        name="outproj_ffn2",
    )(x1, yml, yna, wout, g2, wg, wu, wd, gf)


def _layer(x, norm_ffn1, w1_gate, w1_up, w1_down, norm_mix, w_in, b_gates, conv_w, conv_b, w_q_ml,
           w_k_ml, gn_ml, gq_na, gk_na, rpb, w_out, norm_ffn2, w2_gate, w2_up, w2_down, norm_final):
    b, t, dm = x.shape
    n = b * t
    W = ML_WIDTH
    row = lambda a: a.reshape(1, -1).astype(F32)
    g0 = 3 * W
    wc = w_in[:, :W].astype(BF16)
    wo = w_in[:, 2 * W:g0].astype(BF16)
    wna = w_in[:, g0 + N_GATES:].astype(BF16)
    wgate = w_in[:, g0:g0 + N_GATES]
    H = ML_HEADS
    order = np.concatenate([dd * 2 * H + np.r_[0:2 * H, H:2 * H, 0:H] for dd in range(2)])
    wvg = jnp.concatenate([w_in[:, W:2 * W].T, wgate.T[order]], axis=0).astype(BF16)
    bgr = b_gates[order].reshape(-1, 1).astype(F32)
    ones = jnp.asarray(np.kron(np.eye(NA_HEADS), np.ones((NA_DH, NA_DH))), BF16)
    gq = jnp.tile(gq_na.astype(F32), NA_HEADS).reshape(1, -1)
    gk = jnp.tile(gk_na.astype(F32), NA_HEADS).reshape(1, -1)

    x1, c, vt_ml, o, ga, gb, qn, kn, vn = _ffn1_inproj(
        x.reshape(n, dm), row(norm_ffn1), w1_gate.astype(BF16), w1_up.astype(BF16), w1_down.astype(BF16),
        row(norm_mix), wc, wvg, wo, wna, bgr, ones, gq, gk)

    seq = lambda a: a.reshape(b, t, a.shape[-1])
    qt_ml, k_ml = _conv_qk(seq(c), conv_w.astype(F32), row(conv_b),
                           w_q_ml.transpose(0, 2, 1).astype(BF16), w_k_ml.astype(BF16))
    rowq, colq = _mlstm_gates(ga.reshape(2, -1, ML_L), gb.reshape(2, -1, ML_L), b, t)
    gnb = jnp.broadcast_to(gn_ml.astype(F32).reshape(W, 1), (W, ML_L))
    y_ml = _mlstm(qt_ml, k_ml, vt_ml.reshape(b, t // ML_L, W, ML_L), seq(o), rowq, colq, gnb)
    y_na = _na(seq(qn), seq(kn), seq(vn), _na_bias_table(rpb))

    out = _outproj_ffn2(x1, y_ml.reshape(n, W), y_na.reshape(n, NA_WIDTH),
                        w_out.astype(BF16), row(norm_ffn2),
                        w2_gate.astype(BF16), w2_up.astype(BF16), w2_down.astype(BF16), row(norm_final))
    return out.reshape(b, t, dm)


def kernel(x, norm_ffn1, w1_gate, w1_up, w1_down, norm_mix, w_in, b_gates, conv_w, conv_b, w_q_ml, w_k_ml,
           gn_ml, gq_na, gk_na, rpb, w_out, norm_ffn2, w2_gate, w2_up, w2_down, norm_final):
    depth = norm_ffn1.shape[0]
    for l in range(depth):
        x = _layer(x, norm_ffn1[l], w1_gate[l], w1_up[l], w1_down[l], norm_mix[l], w_in[l], b_gates[l],
                   conv_w[l], conv_b[l], w_q_ml[l], w_k_ml[l], gn_ml[l], gq_na[l], gk_na[l], rpb[l],
                   w_out[l], norm_ffn2[l], w2_gate[l], w2_up[l], w2_down[l], norm_final[l])
    return x
```

```python
import functools

import jax
import jax.numpy as jnp
import numpy as np
from jax import lax
from jax.experimental import pallas as pl
from jax.experimental.pallas import tpu as pltpu

F32 = jnp.float32
BF16 = jnp.bfloat16

D_MODEL = 1024
D_FF = 2816
GRID_W = 64
ML_HEADS = 4
ML_DH = 128
ML_WIDTH = ML_HEADS * ML_DH
CONV_W = 5
NA_HEADS = 8
NA_DH = 64
NA_WIDTH = NA_HEADS * NA_DH
NA_KH = 8
NA_KW = 16
N_GATES = 4 * ML_HEADS
EPS = 1e-6
NEG = -1e30
LOG2E = 1.4426950408889634

SUBLANES = 8
LANES = 128

FFN_TM = 512
FFN_CK = 2816
ML_L = 128
ML_TT = 1024
ML_NROWQ = 6
ML_NONES = 16
CONV_TT = 512
NA_R = 16
NA_PACK = 4
VMEM_LIMIT = 56 * 1024 * 1024


def _dot(a, b):
    return jnp.dot(a, b, preferred_element_type=F32)


def _dot_nt(a, b):
    return lax.dot_general(a, b, (((1,), (1,)), ((), ())), preferred_element_type=F32)


def _dot_tn(a, b):
    return lax.dot_general(a, b, (((0,), (0,)), ((), ())), preferred_element_type=F32)


def _dot_exact(a, b):
    return jnp.dot(a, b, preferred_element_type=F32, precision=lax.Precision.HIGHEST)


def _rms(x, g):
    ms = jnp.mean(x * x, axis=-1, keepdims=True)
    return x * lax.rsqrt(ms + EPS) * g


def _log_sigmoid(x):
    return jnp.minimum(x, 0.0) - jnp.log1p(jnp.exp(-jnp.abs(x)))


def _swiglu_half(h, wg_ref, wu_ref, wd_ref):
    acc = None
    for j in range(D_FF // FFN_CK):
        sl = slice(j * FFN_CK, (j + 1) * FFN_CK)
        g = _dot(h, wg_ref[:, sl])
        u = _dot(h, wu_ref[:, sl])
        a = (g * jax.nn.sigmoid(g) * u).astype(BF16)
        part = _dot(a, wd_ref[sl, :])
        acc = part if acc is None else acc + part
    return 0.5 * acc


def _group_mean_sq(x, ones_ref):
    xx = x * x
    return _dot(xx.astype(BF16), ones_ref[...]) * (1.0 / NA_DH)


def _ffn1_inproj_kernel(x_ref, g1_ref, wg_ref, wu_ref, wd_ref, gm_ref, wc_ref, wvg_ref, wo_ref, wna_ref,
                        bgr_ref, ones_ref, gq_ref, gk_ref,
                        x1_ref, c_ref, vt_ref, o_ref, ga_ref, gb_ref, qn_ref, kn_ref, vn_ref):
    x = x_ref[...]
    h = _rms(x, g1_ref[...]).astype(BF16)
    x1 = x + _swiglu_half(h, wg_ref, wu_ref, wd_ref)
    x1_ref[...] = x1
    h2 = _rms(x1, gm_ref[...]).astype(BF16)
    W = ML_WIDTH
    c_ref[...] = _dot(h2, wc_ref[...])
    o_ref[...] = _dot(h2, wo_ref[...])
    vg = _dot_nt(wvg_ref[...], h2)
    vt = vg[:W].astype(BF16)
    for cidx in range(FFN_TM // ML_L):
        vt_ref[cidx] = vt[:, cidx * ML_L:(cidx + 1) * ML_L]
    p = _dot(h2, wna_ref[...])
    q = p[:, 0:W]
    k = p[:, W:2 * W]
    vn_ref[...] = p[:, 2 * W:3 * W].astype(BF16)
    qn = q * lax.rsqrt(_group_mean_sq(q, ones_ref) + EPS) * gq_ref[...] * (NA_DH ** -0.5 * LOG2E)
    kn = k * lax.rsqrt(_group_mean_sq(k, ones_ref) + EPS) * gk_ref[...]
    qn_ref[...] = qn.astype(BF16)
    kn_ref[...] = kn.astype(BF16)
    gr = vg[W:] + bgr_ref[...]
    row = lax.broadcasted_iota(jnp.int32, gr.shape, 0) % (4 * ML_HEADS)
    gr = jnp.where((row >= ML_HEADS) & (row < 3 * ML_HEADS), _log_sigmoid(gr), gr)
    for dd in range(2):
        for cidx in range(FFN_TM // ML_L):
            lanes = slice(cidx * ML_L, (cidx + 1) * ML_L)
            ga_ref[dd, cidx] = gr[dd * 16:dd * 16 + 8, lanes]
            gb_ref[dd, cidx] = gr[dd * 16 + 8:dd * 16 + 16, lanes]


def _const_spec(shape):
    nd = len(shape)
    return pl.BlockSpec(shape, lambda *_: (0,) * nd, pipeline_mode=pl.Buffered(1))


def _ffn1_inproj(x2d, g1, wg, wu, wd, gm, wc, wvg, wo, wna, bgr, ones, gq, gk):
    n = x2d.shape[0]
    tm = FFN_TM
    W = ML_WIDTH
    tok = lambda w: pl.BlockSpec((tm, w), lambda i: (i, 0))
    out_shape = (
        jax.ShapeDtypeStruct((n, D_MODEL), F32),
        jax.ShapeDtypeStruct((n, W), F32),
        jax.ShapeDtypeStruct((n // ML_L, W, ML_L), BF16),
        jax.ShapeDtypeStruct((n, W), F32),
        jax.ShapeDtypeStruct((2, n // ML_L, SUBLANES, ML_L), F32),
        jax.ShapeDtypeStruct((2, n // ML_L, SUBLANES, ML_L), F32),
        jax.ShapeDtypeStruct((n, W), BF16),
        jax.ShapeDtypeStruct((n, W), BF16),
        jax.ShapeDtypeStruct((n, W), BF16),
    )
    out_specs = (
        tok(D_MODEL), tok(W), pl.BlockSpec((tm // ML_L, W, ML_L), lambda i: (i, 0, 0)), tok(W),
        pl.BlockSpec((2, tm // ML_L, SUBLANES, ML_L), lambda i: (0, i, 0, 0)),
        pl.BlockSpec((2, tm // ML_L, SUBLANES, ML_L), lambda i: (0, i, 0, 0)),
        tok(W), tok(W), tok(W),
    )
    in_specs = [tok(D_MODEL)] + [_const_spec(a.shape) for a in
                                 (g1, wg, wu, wd, gm, wc, wvg, wo, wna, bgr, ones, gq, gk)]
    return pl.pallas_call(
        _ffn1_inproj_kernel,
        grid=(n // tm,),
        in_specs=in_specs,
        out_specs=out_specs,
        out_shape=out_shape,
        compiler_params=pltpu.CompilerParams(
            dimension_semantics=("parallel",), vmem_limit_bytes=VMEM_LIMIT),
        name="ffn1_inproj",
    )(x2d, g1, wg, wu, wd, gm, wc, wvg, wo, wna, bgr, ones, gq, gk)


def _conv_qk_kernel(c_ref, prev_ref, next_ref, cw_ref, cb_ref, wqt_ref, wk_ref, qt_ref, k_ref, pad_ref):
    i = pl.program_id(1)
    nb = pl.num_programs(1)
    tt = CONV_TT
    half = CONV_W // 2
    pad_ref[0:SUBLANES, :] = jnp.where(i > 0, prev_ref[...], 0.0)
    pad_ref[SUBLANES:SUBLANES + tt, :] = c_ref[...]
    pad_ref[SUBLANES + tt:2 * SUBLANES + tt, :] = jnp.where(i < nb - 1, next_ref[...], 0.0)
    y = cb_ref[...]
    for kk in range(CONV_W):
        start = SUBLANES + kk - half
        y = y + pad_ref[start:start + tt, :] * cw_ref[kk:kk + 1, :]
    u = y * jax.nn.sigmoid(y)
    for h in range(ML_HEADS):
        sl = slice(h * ML_DH, (h + 1) * ML_DH)
        uh = u[:, sl].astype(BF16)
        k_ref[:, sl] = (_dot(uh, wk_ref[h]) * (ML_DH ** -0.5)).astype(BF16)
        qt = _dot_nt(wqt_ref[h], uh).astype(BF16)
        for cidx in range(tt // ML_L):
            qt_ref[cidx, sl, :] = qt[:, cidx * ML_L:(cidx + 1) * ML_L]


def _conv_qk(c3, cw, cb, wqt, wk):
    b, t, w = c3.shape
    tt = CONV_TT
    nb = t // tt
    per = tt // SUBLANES
    last = t // SUBLANES - 1
    tile = pl.BlockSpec((None, tt, w), lambda bi, i: (bi, i, 0))
    prev = pl.BlockSpec((None, SUBLANES, w), lambda bi, i: (bi, jnp.maximum(i * per - 1, 0), 0))
    nxt = pl.BlockSpec((None, SUBLANES, w), lambda bi, i: (bi, jnp.minimum((i + 1) * per, last), 0))
    return pl.pallas_call(
        _conv_qk_kernel,
        grid=(b, nb),
        in_specs=[tile, prev, nxt] + [
            pl.BlockSpec(a.shape, functools.partial(lambda nd, bi, i: (0,) * nd, a.ndim))
            for a in (cw, cb, wqt, wk)],
        out_specs=(pl.BlockSpec((None, tt // ML_L, w, ML_L), lambda bi, i: (bi, i, 0, 0)), tile),
        out_shape=(jax.ShapeDtypeStruct((b, t // ML_L, w, ML_L), BF16),
                   jax.ShapeDtypeStruct((b, t, w), BF16)),
        scratch_shapes=[pltpu.VMEM((tt + 2 * SUBLANES, w), F32)],
        compiler_params=pltpu.CompilerParams(dimension_semantics=("parallel", "parallel")),
        name="conv_qk",
    )(c3, c3, c3, cw, cb, wqt, wk)


def _lane_scan(x, pos, d, combine, ident):
    L = x.shape[-1]
    k = 1
    while k < L:
        shifted = pltpu.roll(x, jnp.where(d == 0, k, L - k), axis=1)
        x = combine(x, jnp.where(pos >= k, shifted, ident))
        k *= 2
    return x


def _mlstm_gates_kernel(ga_ref, gb_ref, rowq_ref, colq_ref, b_s, g_s, mloc_s, rmax_s, w_s, vec_s):
    d = pl.program_id(1)
    R = SUBLANES
    L = ML_L
    nc = ga_ref.shape[0] // R
    xa = ga_ref[...]
    xb = gb_ref[...]
    lane = lax.broadcasted_iota(jnp.int32, xa.shape, 1)
    pos = lane + d * (L - 1 - 2 * lane)
    b = _lane_scan(xb, pos, d, jnp.add, 0.0)
    g = jnp.broadcast_to(jnp.sum(xb, axis=1, keepdims=True), xb.shape)
    w = xa - b
    a = g + w
    m_loc = jnp.broadcast_to(jnp.max(a, axis=1, keepdims=True), a.shape)
    rowq_ref[3] = jnp.exp(a - m_loc)
    b_s[...] = b
    g_s[...] = g
    mloc_s[...] = m_loc
    rmax_s[...] = _lane_scan(w, pos, d, jnp.maximum, NEG)
    w_s[...] = w
    vec_s[...] = jnp.zeros_like(vec_s)

    def chunk(k, m0):
        ck = k + d * (nc - 1 - 2 * k)
        rows = pl.ds(pl.multiple_of(ck * R, R), R)
        gk = g_s[rows, :]
        mlk = mloc_s[rows, :]
        mm = jnp.maximum(m0, rmax_s[rows, :])
        m_new = jnp.maximum(gk + m0, mlk)
        rowq_ref[0, rows, :] = mm
        rowq_ref[1, rows, :] = jnp.exp(m0 - mm)
        rowq_ref[2, rows, :] = jnp.exp(-(b_s[rows, :] + mm))
        rowq_ref[4, rows, :] = jnp.exp(gk + m0 - m_new)
        rowq_ref[5, rows, :] = jnp.exp(mlk - m_new)
        vec_s[ck, 0:R, :] = w_s[rows, :]
        colq_ref[ck] = vec_s[ck].T
        return m_new

    lax.fori_loop(0, nc, chunk, jnp.zeros((R, L), F32), unroll=8)


def _mlstm_gates(ga, gb, b, t):
    nc = t // ML_L
    R = SUBLANES
    in_spec = pl.BlockSpec((None, nc * R, ML_L), lambda bi, d: (d, bi, 0))
    return pl.pallas_call(
        _mlstm_gates_kernel,
        grid=(b, 2),
        in_specs=[in_spec, in_spec],
        out_specs=(pl.BlockSpec((None, None, ML_NROWQ, nc * R, ML_L), lambda bi, d: (bi, d, 0, 0, 0)),
                   pl.BlockSpec((None, None, nc, ML_L, LANES), lambda bi, d: (bi, d, 0, 0, 0))),
        out_shape=(jax.ShapeDtypeStruct((b, 2, ML_NROWQ, nc * R, ML_L), F32),
                   jax.ShapeDtypeStruct((b, 2, nc, ML_L, LANES), F32)),
        scratch_shapes=[pltpu.VMEM((nc * R, ML_L), F32)] * 5 + [pltpu.VMEM((nc, LANES, ML_L), F32)],
        compiler_params=pltpu.CompilerParams(dimension_semantics=("parallel", "parallel")),
        name="mlstm_gates",
    )(ga, gb)


def _mlstm_kernel(qt_ref, k_ref, vt_ref, o_ref, rowq_ref, colq_ref, gnb_ref, y_ref,
                  hf_ref, state_ref, sq_s, upd_s, *, nblk):
    d = pl.program_id(1)
    i = pl.program_id(2)
    blk = i + d * (nblk - 1 - 2 * i)
    L = ML_L
    nch = ML_TT // L
    H = ML_HEADS
    DH = ML_DH
    R = SUBLANES

    @pl.when(i == 0)
    def _():
        state_ref[...] = jnp.zeros_like(state_ref)

    s_idx = lax.broadcasted_iota(jnp.int32, (L, L), 0)
    t_idx = lax.broadcasted_iota(jnp.int32, (L, L), 1)
    fwd = d == 0
    mask = (t_idx - s_idx) * (1 - 2 * d) >= 0
    ones = jnp.ones((ML_NONES, L), BF16)
    NA = DH + ML_NONES
    c_total = hf_ref.shape[0] - nch
    hbase = jnp.where(fwd, blk * nch, c_total)

    def chunk_of(j):
        cj = j + d * (nch - 1 - 2 * j)
        return cj, pl.ds(pl.multiple_of(cj * R, R), R)

    for j in range(nch):
        cj, rows = chunk_of(j)
        off = pl.multiple_of(cj * L, L)
        mm = rowq_ref[0, rows, :]
        sc = rowq_ref[1, rows, :]
        w_end = rowq_ref[3, rows, :]
        cols = colq_ref[cj]
        for h in range(H):
            sl = slice(h * DH, (h + 1) * DH)
            k = k_ref[pl.ds(off, L), sl]
            qt = qt_ref[cj, sl, :]
            vaug = jnp.concatenate([vt_ref[cj, sl, :], ones], axis=0)
            e = jnp.exp(jnp.where(mask, cols[:, h:h + 1] - mm[h:h + 1, :], NEG))
            sq_s[j * H + h, 0:L, :] = (_dot(k, qt) * e).astype(BF16)
            sq_s[j * H + h, L:L + DH, :] = (qt.astype(F32) * sc[h:h + 1, :]).astype(BF16)
            vw = (vaug.astype(F32) * w_end[h:h + 1, :]).astype(BF16)
            upd_s[j * H + h] = _dot(vw, k)

    for j in range(nch):
        cj, rows = chunk_of(j)
        em = rowq_ref[2, rows, :]
        s_old = rowq_ref[4, rows, :]
        s_loc = rowq_ref[5, rows, :]
        for h in range(H):
            sl = slice(h * DH, (h + 1) * DH)
            vaug = jnp.concatenate([vt_ref[cj, sl, :], ones], axis=0)
            st = state_ref[h]
            r = _dot(jnp.concatenate([vaug, st.astype(BF16)], axis=1), sq_s[j * H + h])
            inv = 1.0 / jnp.maximum(jnp.abs(r[DH:DH + 1, :]), em[h:h + 1, :])
            hf_ref[hbase + cj, sl, :] = r[:DH, :] * inv
            state_ref[h] = (jnp.broadcast_to(s_old[h:h + 1, :], (NA, DH)) * st
                            + jnp.broadcast_to(s_loc[h:h + 1, :], (NA, DH)) * upd_s[j * H + h])

    @pl.when(jnp.logical_not(fwd))
    def _():
        for c in range(nch):
            for h in range(H):
                sl = slice(h * DH, (h + 1) * DH)
                tot = hf_ref[blk * nch + c, sl, :] + hf_ref[c_total + c, sl, :]
                ms = jnp.mean(tot * tot, axis=0, keepdims=True)
                hn = (tot * lax.rsqrt(ms + EPS) * gnb_ref[sl, :]).T
                tok = slice(c * L, (c + 1) * L)
                y_ref[tok, sl] = (jax.nn.sigmoid(o_ref[tok, sl]) * hn).astype(BF16)


def _mlstm(qt4, k3, vt4, o3, rowq, colq, gnb):
    b, t, w = k3.shape
    tt = ML_TT
    nblk = t // tt
    nch = tt // ML_L
    blk_of = lambda d, i: i + d * (nblk - 1 - 2 * i)
    tile = pl.BlockSpec((None, tt, w), lambda bi, d, i: (bi, blk_of(d, i), 0))
    ft_spec = pl.BlockSpec((None, nch, w, ML_L), lambda bi, d, i: (bi, blk_of(d, i), 0, 0))
    rq_spec = pl.BlockSpec((None, None, ML_NROWQ, nch * SUBLANES, ML_L),
                           lambda bi, d, i: (bi, d, 0, blk_of(d, i), 0))
    cq_spec = pl.BlockSpec((None, None, nch, ML_L, LANES), lambda bi, d, i: (bi, d, blk_of(d, i), 0, 0))
    gn_spec = pl.BlockSpec(gnb.shape, lambda bi, d, i: (0, 0))
    y_spec = pl.BlockSpec((None, tt, w), lambda bi, d, i: (bi, d * blk_of(d, i) + (1 - d) * (nblk - 1), 0))
    return pl.pallas_call(
        functools.partial(_mlstm_kernel, nblk=nblk),
        grid=(b, 2, nblk),
        in_specs=[ft_spec, tile, ft_spec, tile, rq_spec, cq_spec, gn_spec],
        out_specs=y_spec,
        out_shape=jax.ShapeDtypeStruct((b, t, w), BF16),
        scratch_shapes=[
            pltpu.VMEM((t // ML_L + nch, w, ML_L), F32),
            pltpu.VMEM((ML_HEADS, ML_DH + ML_NONES, ML_DH), F32),
            pltpu.VMEM((nch * ML_HEADS, ML_L + ML_DH, ML_L), BF16),
            pltpu.VMEM((nch * ML_HEADS, ML_DH + ML_NONES, ML_DH), F32),
        ],
        compiler_params=pltpu.CompilerParams(
            dimension_semantics=("arbitrary", "arbitrary", "arbitrary"), vmem_limit_bytes=VMEM_LIMIT),
        name="mlstm",
    )(qt4, k3, vt4, o3, rowq, colq, gnb)


def _na_kernel(q_ref, k_ref, v_ref, toep_ref, out_ref, bias_ref, *, rows):
    r0 = pl.program_id(1) * NA_R
    nkeys = NA_KH * GRID_W
    gw = NA_PACK * NA_DH

    @pl.when(pl.program_id(1) == 0)
    def _():
        for delta in range(NA_KH):
            lo = (NA_KH - 1 - delta) * GRID_W
            for h in range(NA_HEADS):
                g, hh = divmod(h, NA_PACK)
                bias_ref[delta, g, hh * GRID_W:(hh + 1) * GRID_W, :] = toep_ref[h, :, lo:lo + nkeys]

    row_blk = lax.broadcasted_iota(jnp.int32, (NA_PACK * GRID_W, gw), 0) // GRID_W
    lane_blk = lax.broadcasted_iota(jnp.int32, (NA_PACK * GRID_W, gw), 1) // NA_DH
    diag = row_blk == lane_blk
    out_blk = lax.broadcasted_iota(jnp.int32, (GRID_W, gw), 1) // NA_DH

    ng = NA_HEADS // NA_PACK

    def window(ri):
        r = r0 + ri
        rs = jnp.clip(r - NA_KH // 2, 0, rows - NA_KH)
        return pl.ds(pl.multiple_of(rs * GRID_W, GRID_W), nkeys), r - rs

    def row_body(ri, carry):
        keys, delta = window(ri)
        qoff = pl.multiple_of(ri * GRID_W, GRID_W)
        outs = []
        for g in range(ng):
            sl = slice(g * gw, (g + 1) * gw)
            q4 = q_ref[pl.ds(qoff, GRID_W), sl]
            qbd = jnp.where(diag, jnp.concatenate([q4] * NA_PACK, axis=0), jnp.zeros((), BF16))
            s = _dot_nt(qbd, k_ref[keys, sl]) + bias_ref[delta, g]
            p = jnp.exp2(s - jnp.max(s, axis=-1, keepdims=True))
            linv = 1.0 / jnp.sum(p, axis=-1, keepdims=True)
            o = _dot(p.astype(BF16), v_ref[keys, sl]) * linv
            og = o[(NA_PACK - 1) * GRID_W:, :]
            for h in range(NA_PACK - 2, -1, -1):
                og = jnp.where(out_blk == h, o[h * GRID_W:(h + 1) * GRID_W, :], og)
            outs.append(og)
        out_ref[pl.ds(qoff, GRID_W), :] = jnp.concatenate(outs, axis=-1).astype(BF16)
        return carry

    lax.fori_loop(0, NA_R, row_body, 0, unroll=True)


def _na(q3, k3, v3, toep):
    b, t, w = q3.shape
    rows = t // GRID_W
    tq = NA_R * GRID_W
    qtile = pl.BlockSpec((None, tq, w), lambda bi, i: (bi, i, 0))
    seq = pl.BlockSpec((None, t, w), lambda bi, i: (bi, 0, 0))
    toep_spec = pl.BlockSpec(toep.shape, lambda bi, i: (0, 0, 0), pipeline_mode=pl.Buffered(1))
    return pl.pallas_call(
        functools.partial(_na_kernel, rows=rows),
        grid=(b, rows // NA_R),
        in_specs=[qtile, seq, seq, toep_spec],
        out_specs=qtile,
        out_shape=jax.ShapeDtypeStruct((b, t, w), BF16),
        scratch_shapes=[pltpu.VMEM((NA_KH, NA_HEADS // NA_PACK, NA_PACK * GRID_W, NA_KH * GRID_W), F32)],
        compiler_params=pltpu.CompilerParams(
            dimension_semantics=("arbitrary", "arbitrary"), vmem_limit_bytes=VMEM_LIMIT),
        name="natten",
    )(q3, k3, v3, toep)


def _na_bias_table(rpb):
    c = np.arange(GRID_W)
    cs = np.clip(c - NA_KW // 2, 0, GRID_W - NA_KW)
    cc = np.arange(GRID_W)
    valid = (cc[None, :] >= cs[:, None]) & (cc[None, :] < cs[:, None] + NA_KW)
    nh, ndr, ndc = rpb.shape
    lead = GRID_W - NA_KW
    w = jnp.pad(rpb.astype(F32), ((0, 0), (0, 0), (lead, 2 * GRID_W - lead - ndc)))
    skew = jnp.broadcast_to(w[:, :, None, :], (nh, ndr, GRID_W, 2 * GRID_W)).reshape(nh, ndr, -1)
    skew = skew[:, :, :GRID_W * (2 * GRID_W - 1)].reshape(nh, ndr, GRID_W, 2 * GRID_W - 1)
    toep = jnp.where(valid[None, None], skew[..., GRID_W - 1:] * LOG2E, NEG).transpose(0, 2, 1, 3)
    return toep.reshape(nh, GRID_W, ndr * GRID_W)


def _outproj_ffn2_kernel(x1_ref, yml_ref, yna_ref, wout_ref, g2_ref, wg_ref, wu_ref, wd_ref,
                         gf_ref, out_ref):
    x2 = x1_ref[...] + _dot(jnp.concatenate([yml_ref[...], yna_ref[...]], axis=1), wout_ref[...])
    h = _rms(x2, g2_ref[...]).astype(BF16)
    x3 = x2 + _swiglu_half(h, wg_ref, wu_ref, wd_ref)
    out_ref[...] = _rms(x3, gf_ref[...])


def _outproj_ffn2(x1, yml, yna, wout, g2, wg, wu, wd, gf):
    n = x1.shape[0]
    tm = FFN_TM
    tok = lambda w: pl.BlockSpec((tm, w), lambda i: (i, 0))
    return pl.pallas_call(
        _outproj_ffn2_kernel,
        grid=(n // tm,),
        in_specs=[tok(D_MODEL), tok(ML_WIDTH), tok(NA_WIDTH)] + [
            _const_spec(a.shape) for a in (wout, g2, wg, wu, wd, gf)],
        out_specs=tok(D_MODEL),
        out_shape=jax.ShapeDtypeStruct((n, D_MODEL), F32),
        compiler_params=pltpu.CompilerParams(
            dimension_semantics=("parallel",), vmem_limit_bytes=VMEM_LIMIT),
        name="outproj_ffn2",
    )(x1, yml, yna, wout, g2, wg, wu, wd, gf)


def _layer(x, norm_ffn1, w1_gate, w1_up, w1_down, norm_mix, w_in, b_gates, conv_w, conv_b, w_q_ml,
           w_k_ml, gn_ml, gq_na, gk_na, rpb, w_out, norm_ffn2, w2_gate, w2_up, w2_down, norm_final):
    b, t, dm = x.shape
    n = b * t
    W = ML_WIDTH
    row = lambda a: a.reshape(1, -1).astype(F32)
    g0 = 3 * W
    wc = w_in[:, :W].astype(BF16)
    wo = w_in[:, 2 * W:g0].astype(BF16)
    wna = w_in[:, g0 + N_GATES:].astype(BF16)
    wgate = w_in[:, g0:g0 + N_GATES]
    H = ML_HEADS
    order = np.concatenate([dd * 2 * H + np.r_[0:2 * H, H:2 * H, 0:H] for dd in range(2)])
    wvg = jnp.concatenate([w_in[:, W:2 * W].T, wgate.T[order]], axis=0).astype(BF16)
    bgr = b_gates[order].reshape(-1, 1).astype(F32)
    ones = jnp.asarray(np.kron(np.eye(NA_HEADS), np.ones((NA_DH, NA_DH))), BF16)
    gq = jnp.tile(gq_na.astype(F32), NA_HEADS).reshape(1, -1)
    gk = jnp.tile(gk_na.astype(F32), NA_HEADS).reshape(1, -1)

    x1, c, vt_ml, o, ga, gb, qn, kn, vn = _ffn1_inproj(
        x.reshape(n, dm), row(norm_ffn1), w1_gate.astype(BF16), w1_up.astype(BF16), w1_down.astype(BF16),
        row(norm_mix), wc, wvg, wo, wna, bgr, ones, gq, gk)

    seq = lambda a: a.reshape(b, t, a.shape[-1])
    qt_ml, k_ml = _conv_qk(seq(c), conv_w.astype(F32), row(conv_b),
                           w_q_ml.transpose(0, 2, 1).astype(BF16), w_k_ml.astype(BF16))
    rowq, colq = _mlstm_gates(ga.reshape(2, -1, ML_L), gb.reshape(2, -1, ML_L), b, t)
    gnb = jnp.broadcast_to(gn_ml.astype(F32).reshape(W, 1), (W, ML_L))
    y_ml = _mlstm(qt_ml, k_ml, vt_ml.reshape(b, t // ML_L, W, ML_L), seq(o), rowq, colq, gnb)
    y_na = _na(seq(qn), seq(kn), seq(vn), _na_bias_table(rpb))

    out = _outproj_ffn2(x1, y_ml.reshape(n, W), y_na.reshape(n, NA_WIDTH),
                        w_out.astype(BF16), row(norm_ffn2),
                        w2_gate.astype(BF16), w2_up.astype(BF16), w2_down.astype(BF16), row(norm_final))
    return out.reshape(b, t, dm)


def kernel(x, norm_ffn1, w1_gate, w1_up, w1_down, norm_mix, w_in, b_gates, conv_w, conv_b, w_q_ml, w_k_ml,
           gn_ml, gq_na, gk_na, rpb, w_out, norm_ffn2, w2_gate, w2_up, w2_down, norm_final):
    depth = norm_ffn1.shape[0]
    for l in range(depth):
        x = _layer(x, norm_ffn1[l], w1_gate[l], w1_up[l], w1_down[l], norm_mix[l], w_in[l], b_gates[l],
                   conv_w[l], conv_b[l], w_q_ml[l], w_k_ml[l], gn_ml[l], gq_na[l], gk_na[l], rpb[l],
                   w_out[l], norm_ffn2[l], w2_gate[l], w2_up[l], w2_down[l], norm_final[l])
    return x
```

```python
import functools

import jax
import jax.numpy as jnp
import numpy as np
from jax import lax
from jax.experimental import pallas as pl
from jax.experimental.pallas import tpu as pltpu

F32 = jnp.float32
BF16 = jnp.bfloat16

D_MODEL = 1024
D_FF = 2816
GRID_W = 64
ML_HEADS = 4
ML_DH = 128
ML_WIDTH = ML_HEADS * ML_DH
CONV_W = 5
NA_HEADS = 8
NA_DH = 64
NA_WIDTH = NA_HEADS * NA_DH
NA_KH = 8
NA_KW = 16
N_GATES = 4 * ML_HEADS
EPS = 1e-6
NEG = -1e30
LOG2E = 1.4426950408889634

SUBLANES = 8
LANES = 128

FFN_TM = 512
FFN_CK = 2816
ML_L = 128
ML_TT = 1024
ML_NROWQ = 6
ML_NONES = 16
CONV_TT = 1024
NA_R = 16
NA_PACK = 4
VMEM_LIMIT = 56 * 1024 * 1024


def _dot(a, b):
    return jnp.dot(a, b, preferred_element_type=F32)


def _dot_nt(a, b):
    return lax.dot_general(a, b, (((1,), (1,)), ((), ())), preferred_element_type=F32)


def _dot_tn(a, b):
    return lax.dot_general(a, b, (((0,), (0,)), ((), ())), preferred_element_type=F32)


def _dot_exact(a, b):
    return jnp.dot(a, b, preferred_element_type=F32, precision=lax.Precision.HIGHEST)


def _rms(x, g):
    ms = jnp.mean(x * x, axis=-1, keepdims=True)
    return x * lax.rsqrt(ms + EPS) * g


def _log_sigmoid(x):
    return jnp.minimum(x, 0.0) - jnp.log1p(jnp.exp(-jnp.abs(x)))


def _swiglu_half(h, wg_ref, wu_ref, wd_ref):
    acc = None
    for j in range(D_FF // FFN_CK):
        sl = slice(j * FFN_CK, (j + 1) * FFN_CK)
        g = _dot(h, wg_ref[:, sl])
        u = _dot(h, wu_ref[:, sl])
        a = (g * jax.nn.sigmoid(g) * u).astype(BF16)
        part = _dot(a, wd_ref[sl, :])
        acc = part if acc is None else acc + part
    return 0.5 * acc


def _group_mean_sq(x, ones_ref):
    xx = x * x
    return _dot(xx.astype(BF16), ones_ref[...]) * (1.0 / NA_DH)


def _ffn1_inproj_kernel(x_ref, g1_ref, wg_ref, wu_ref, wd_ref, gm_ref, wc_ref, wvg_ref, wo_ref, wna_ref,
                        bgr_ref, ones_ref, gq_ref, gk_ref,
                        x1_ref, c_ref, vt_ref, o_ref, ga_ref, gb_ref, qn_ref, kn_ref, vn_ref):
    x = x_ref[...]
    h = _rms(x, g1_ref[...]).astype(BF16)
    x1 = x + _swiglu_half(h, wg_ref, wu_ref, wd_ref)
    x1_ref[...] = x1
    h2 = _rms(x1, gm_ref[...]).astype(BF16)
    W = ML_WIDTH
    c_ref[...] = _dot(h2, wc_ref[...])
    o_ref[...] = jax.nn.sigmoid(_dot(h2, wo_ref[...])).astype(BF16)
    vg = _dot_nt(wvg_ref[...], h2)
    vt = vg[:W].astype(BF16)
    for cidx in range(FFN_TM // ML_L):
        vt_ref[cidx] = vt[:, cidx * ML_L:(cidx + 1) * ML_L]
    p = _dot(h2, wna_ref[...])
    q = p[:, 0:W]
    k = p[:, W:2 * W]
    vn_ref[...] = p[:, 2 * W:3 * W].astype(BF16)
    qn = q * lax.rsqrt(_group_mean_sq(q, ones_ref) + EPS) * gq_ref[...] * (NA_DH ** -0.5 * LOG2E)
    kn = k * lax.rsqrt(_group_mean_sq(k, ones_ref) + EPS) * gk_ref[...]
    qn_ref[...] = qn.astype(BF16)
    kn_ref[...] = kn.astype(BF16)
    gr = vg[W:] + bgr_ref[...]
    row = lax.broadcasted_iota(jnp.int32, gr.shape, 0) % (4 * ML_HEADS)
    gr = jnp.where((row >= ML_HEADS) & (row < 3 * ML_HEADS), _log_sigmoid(gr), gr)
    for dd in range(2):
        for cidx in range(FFN_TM // ML_L):
            lanes = slice(cidx * ML_L, (cidx + 1) * ML_L)
            ga_ref[dd, cidx] = gr[dd * 16:dd * 16 + 8, lanes]
            gb_ref[dd, cidx] = gr[dd * 16 + 8:dd * 16 + 16, lanes]


def _const_spec(shape):
    nd = len(shape)
    return pl.BlockSpec(shape, lambda *_: (0,) * nd, pipeline_mode=pl.Buffered(1))


def _ffn1_inproj(x2d, g1, wg, wu, wd, gm, wc, wvg, wo, wna, bgr, ones, gq, gk):
    n = x2d.shape[0]
    tm = FFN_TM
    W = ML_WIDTH
    tok = lambda w: pl.BlockSpec((tm, w), lambda i: (i, 0))
    out_shape = (
        jax.ShapeDtypeStruct((n, D_MODEL), F32),
        jax.ShapeDtypeStruct((n, W), F32),
        jax.ShapeDtypeStruct((n // ML_L, W, ML_L), BF16),
        jax.ShapeDtypeStruct((n, W), BF16),
        jax.ShapeDtypeStruct((2, n // ML_L, SUBLANES, ML_L), F32),
        jax.ShapeDtypeStruct((2, n // ML_L, SUBLANES, ML_L), F32),
        jax.ShapeDtypeStruct((n, W), BF16),
        jax.ShapeDtypeStruct((n, W), BF16),
        jax.ShapeDtypeStruct((n, W), BF16),
    )
    out_specs = (
        tok(D_MODEL), tok(W), pl.BlockSpec((tm // ML_L, W, ML_L), lambda i: (i, 0, 0)), tok(W),
        pl.BlockSpec((2, tm // ML_L, SUBLANES, ML_L), lambda i: (0, i, 0, 0)),
        pl.BlockSpec((2, tm // ML_L, SUBLANES, ML_L), lambda i: (0, i, 0, 0)),
        tok(W), tok(W), tok(W),
    )
    in_specs = [tok(D_MODEL)] + [_const_spec(a.shape) for a in
                                 (g1, wg, wu, wd, gm, wc, wvg, wo, wna, bgr, ones, gq, gk)]
    return pl.pallas_call(
        _ffn1_inproj_kernel,
        grid=(n // tm,),
        in_specs=in_specs,
        out_specs=out_specs,
        out_shape=out_shape,
        compiler_params=pltpu.CompilerParams(
            dimension_semantics=("parallel",), vmem_limit_bytes=VMEM_LIMIT),
        name="ffn1_inproj",
    )(x2d, g1, wg, wu, wd, gm, wc, wvg, wo, wna, bgr, ones, gq, gk)


def _conv_qk_kernel(c_ref, prev_ref, next_ref, cw_ref, cb_ref, wqt_ref, wk_ref, qt_ref, k_ref, pad_ref):
    i = pl.program_id(1)
    nb = pl.num_programs(1)
    tt = CONV_TT
    half = CONV_W // 2
    pad_ref[0:SUBLANES, :] = jnp.where(i > 0, prev_ref[...], 0.0)
    pad_ref[SUBLANES:SUBLANES + tt, :] = c_ref[...]
    pad_ref[SUBLANES + tt:2 * SUBLANES + tt, :] = jnp.where(i < nb - 1, next_ref[...], 0.0)
    y = cb_ref[...]
    for kk in range(CONV_W):
        start = SUBLANES + kk - half
        y = y + pad_ref[start:start + tt, :] * cw_ref[kk:kk + 1, :]
    u = y * jax.nn.sigmoid(y)
    for h in range(ML_HEADS):
        sl = slice(h * ML_DH, (h + 1) * ML_DH)
        uh = u[:, sl].astype(BF16)
        k_ref[:, sl] = (_dot(uh, wk_ref[h]) * (ML_DH ** -0.5)).astype(BF16)
        qt = _dot_nt(wqt_ref[h], uh).astype(BF16)
        for cidx in range(tt // ML_L):
            qt_ref[cidx, sl, :] = qt[:, cidx * ML_L:(cidx + 1) * ML_L]


def _conv_qk(c3, cw, cb, wqt, wk):
    b, t, w = c3.shape
    tt = CONV_TT
    nb = t // tt
    per = tt // SUBLANES
    last = t // SUBLANES - 1
    tile = pl.BlockSpec((None, tt, w), lambda bi, i: (bi, i, 0))
    prev = pl.BlockSpec((None, SUBLANES, w), lambda bi, i: (bi, jnp.maximum(i * per - 1, 0), 0))
    nxt = pl.BlockSpec((None, SUBLANES, w), lambda bi, i: (bi, jnp.minimum((i + 1) * per, last), 0))
    return pl.pallas_call(
        _conv_qk_kernel,
        grid=(b, nb),
        in_specs=[tile, prev, nxt] + [
            pl.BlockSpec(a.shape, functools.partial(lambda nd, bi, i: (0,) * nd, a.ndim))
            for a in (cw, cb, wqt, wk)],
        out_specs=(pl.BlockSpec((None, tt // ML_L, w, ML_L), lambda bi, i: (bi, i, 0, 0)), tile),
        out_shape=(jax.ShapeDtypeStruct((b, t // ML_L, w, ML_L), BF16),
                   jax.ShapeDtypeStruct((b, t, w), BF16)),
        scratch_shapes=[pltpu.VMEM((tt + 2 * SUBLANES, w), F32)],
        compiler_params=pltpu.CompilerParams(dimension_semantics=("parallel", "parallel")),
        name="conv_qk",
    )(c3, c3, c3, cw, cb, wqt, wk)


def _lane_scan(x, pos, d, combine, ident):
    L = x.shape[-1]
    k = 1
    while k < L:
        shifted = pltpu.roll(x, jnp.where(d == 0, k, L - k), axis=1)
        x = combine(x, jnp.where(pos >= k, shifted, ident))
        k *= 2
    return x


def _mlstm_gates_kernel(ga_ref, gb_ref, rowq_ref, colq_ref, b_s, g_s, mloc_s, rmax_s, w_s, vec_s):
    d = pl.program_id(1)
    R = SUBLANES
    L = ML_L
    nc = ga_ref.shape[0] // R
    xa = ga_ref[...]
    xb = gb_ref[...]
    lane = lax.broadcasted_iota(jnp.int32, xa.shape, 1)
    pos = lane + d * (L - 1 - 2 * lane)
    b = _lane_scan(xb, pos, d, jnp.add, 0.0)
    g = jnp.broadcast_to(jnp.sum(xb, axis=1, keepdims=True), xb.shape)
    w = xa - b
    a = g + w
    m_loc = jnp.broadcast_to(jnp.max(a, axis=1, keepdims=True), a.shape)
    rowq_ref[3] = jnp.exp(a - m_loc)
    b_s[...] = b
    g_s[...] = g
    mloc_s[...] = m_loc
    rmax_s[...] = _lane_scan(w, pos, d, jnp.maximum, NEG)
    w_s[...] = w
    vec_s[...] = jnp.zeros_like(vec_s)

    def chunk(k, m0):
        ck = k + d * (nc - 1 - 2 * k)
        rows = pl.ds(pl.multiple_of(ck * R, R), R)
        gk = g_s[rows, :]
        mlk = mloc_s[rows, :]
        mm = jnp.maximum(m0, rmax_s[rows, :])
        m_new = jnp.maximum(gk + m0, mlk)
        rowq_ref[0, rows, :] = mm * LOG2E
        rowq_ref[1, rows, :] = jnp.exp(m0 - mm)
        rowq_ref[2, rows, :] = jnp.exp(-(b_s[rows, :] + mm))
        rowq_ref[4, rows, :] = jnp.exp(gk + m0 - m_new)
        rowq_ref[5, rows, :] = jnp.exp(mlk - m_new)
        vec_s[ck, 0:R, :] = w_s[rows, :] * LOG2E
        colq_ref[ck] = vec_s[ck].T
        return m_new

    lax.fori_loop(0, nc, chunk, jnp.zeros((R, L), F32), unroll=8)


def _mlstm_gates(ga, gb, b, t):
    nc = t // ML_L
    R = SUBLANES
    in_spec = pl.BlockSpec((None, nc * R, ML_L), lambda bi, d: (d, bi, 0))
    return pl.pallas_call(
        _mlstm_gates_kernel,
        grid=(b, 2),
        in_specs=[in_spec, in_spec],
        out_specs=(pl.BlockSpec((None, None, ML_NROWQ, nc * R, ML_L), lambda bi, d: (bi, d, 0, 0, 0)),
                   pl.BlockSpec((None, None, nc, ML_L, LANES), lambda bi, d: (bi, d, 0, 0, 0))),
        out_shape=(jax.ShapeDtypeStruct((b, 2, ML_NROWQ, nc * R, ML_L), F32),
                   jax.ShapeDtypeStruct((b, 2, nc, ML_L, LANES), F32)),
        scratch_shapes=[pltpu.VMEM((nc * R, ML_L), F32)] * 5 + [pltpu.VMEM((nc, LANES, ML_L), F32)],
        compiler_params=pltpu.CompilerParams(dimension_semantics=("parallel", "parallel")),
        name="mlstm_gates",
    )(ga, gb)


def _mlstm_kernel(qt_ref, k_ref, vt_ref, o_ref, rowq_ref, colq_ref, gnb_ref, y_ref,
                  hf_ref, state_ref, sq_s, upd_s, *, nblk):
    d = pl.program_id(1)
    i = pl.program_id(2)
    blk = i + d * (nblk - 1 - 2 * i)
    L = ML_L
    nch = ML_TT // L
    H = ML_HEADS
    DH = ML_DH
    R = SUBLANES

    @pl.when(i == 0)
    def _():
        state_ref[...] = jnp.zeros_like(state_ref)

    s_idx = lax.broadcasted_iota(jnp.int32, (L, L), 0)
    t_idx = lax.broadcasted_iota(jnp.int32, (L, L), 1)
    fwd = d == 0
    mask = (t_idx - s_idx) * (1 - 2 * d) >= 0
    ones = jnp.ones((ML_NONES, L), BF16)
    NA = DH + ML_NONES
    c_total = hf_ref.shape[0] - nch
    hbase = jnp.where(fwd, blk * nch, c_total)

    def chunk_of(j):
        cj = j + d * (nch - 1 - 2 * j)
        return cj, pl.ds(pl.multiple_of(cj * R, R), R)

    for j in range(nch):
        cj, rows = chunk_of(j)
        off = pl.multiple_of(cj * L, L)
        mm = rowq_ref[0, rows, :]
        sc = rowq_ref[1, rows, :]
        w_end = rowq_ref[3, rows, :]
        cols = colq_ref[cj]
        for h in range(0, H, 2):
            ka = k_ref[pl.ds(off, L), h * DH:(h + 1) * DH]
            kb = k_ref[pl.ds(off, L), (h + 1) * DH:(h + 2) * DH]
            qta = qt_ref[cj, h * DH:(h + 1) * DH, :]
            qtb = qt_ref[cj, (h + 1) * DH:(h + 2) * DH, :]
            zero = jnp.zeros((DH, L), BF16)
            st2 = _dot(jnp.concatenate([ka, kb], axis=1),
                       jnp.concatenate([jnp.concatenate([qta, zero], axis=1),
                                        jnp.concatenate([zero, qtb], axis=1)], axis=0))
            vws = []
            for hh, qt in ((h, qta), (h + 1, qtb)):
                sl = slice(hh * DH, (hh + 1) * DH)
                lanes = slice((hh - h) * L, (hh - h + 1) * L)
                vaug = jnp.concatenate([vt_ref[cj, sl, :], ones], axis=0)
                e = jnp.exp2(jnp.where(mask, cols[:, hh:hh + 1] - mm[hh:hh + 1, :], NEG))
                sq_s[j * H + hh, 0:L, :] = (st2[:, lanes] * e).astype(BF16)
                sq_s[j * H + hh, L:L + DH, :] = qt * sc[hh:hh + 1, :].astype(BF16)
                vws.append(vaug * w_end[hh:hh + 1, :].astype(BF16))
            upd2 = _dot(jnp.concatenate(vws, axis=1),
                        jnp.concatenate([jnp.concatenate([ka, zero], axis=1),
                                         jnp.concatenate([zero, kb], axis=1)], axis=0))
            upd_s[j * H + h] = upd2[:, :DH]
            upd_s[j * H + h + 1] = upd2[:, DH:]

    for j in range(nch):
        cj, rows = chunk_of(j)
        em = rowq_ref[2, rows, :]
        s_old = rowq_ref[4, rows, :]
        s_loc = rowq_ref[5, rows, :]
        for h in range(H):
            sl = slice(h * DH, (h + 1) * DH)
            vaug = jnp.concatenate([vt_ref[cj, sl, :], ones], axis=0)
            st = state_ref[h]
            r = _dot(jnp.concatenate([vaug, st.astype(BF16)], axis=1), sq_s[j * H + h])
            inv = 1.0 / jnp.maximum(jnp.abs(r[DH:DH + 1, :]), em[h:h + 1, :])
            hf_ref[hbase + cj, sl, :] = r[:DH, :] * inv
            state_ref[h] = (jnp.broadcast_to(s_old[h:h + 1, :], (NA, DH)) * st
                            + jnp.broadcast_to(s_loc[h:h + 1, :], (NA, DH)) * upd_s[j * H + h])

    @pl.when(jnp.logical_not(fwd))
    def _():
        for c in range(nch):
            for h in range(H):
                sl = slice(h * DH, (h + 1) * DH)
                tot = hf_ref[blk * nch + c, sl, :] + hf_ref[c_total + c, sl, :]
                ms = jnp.mean(tot * tot, axis=0, keepdims=True)
                hn = (tot * lax.rsqrt(ms + EPS) * gnb_ref[sl, :]).T
                tok = slice(c * L, (c + 1) * L)
                y_ref[tok, sl] = (o_ref[tok, sl].astype(F32) * hn).astype(BF16)


def _mlstm(qt4, k3, vt4, o3, rowq, colq, gnb):
    b, t, w = k3.shape
    tt = ML_TT
    nblk = t // tt
    nch = tt // ML_L
    blk_of = lambda d, i: i + d * (nblk - 1 - 2 * i)
    tile = pl.BlockSpec((None, tt, w), lambda bi, d, i: (bi, blk_of(d, i), 0))
    ft_spec = pl.BlockSpec((None, nch, w, ML_L), lambda bi, d, i: (bi, blk_of(d, i), 0, 0))
    rq_spec = pl.BlockSpec((None, None, ML_NROWQ, nch * SUBLANES, ML_L),
                           lambda bi, d, i: (bi, d, 0, blk_of(d, i), 0))
    cq_spec = pl.BlockSpec((None, None, nch, ML_L, LANES), lambda bi, d, i: (bi, d, blk_of(d, i), 0, 0))
    gn_spec = pl.BlockSpec(gnb.shape, lambda bi, d, i: (0, 0))
    y_spec = pl.BlockSpec((None, tt, w), lambda bi, d, i: (bi, d * blk_of(d, i) + (1 - d) * (nblk - 1), 0))
    return pl.pallas_call(
        functools.partial(_mlstm_kernel, nblk=nblk),
        grid=(b, 2, nblk),
        in_specs=[ft_spec, tile, ft_spec, tile, rq_spec, cq_spec, gn_spec],
        out_specs=y_spec,
        out_shape=jax.ShapeDtypeStruct((b, t, w), BF16),
        scratch_shapes=[
            pltpu.VMEM((t // ML_L + nch, w, ML_L), F32),
            pltpu.VMEM((ML_HEADS, ML_DH + ML_NONES, ML_DH), F32),
            pltpu.VMEM((nch * ML_HEADS, ML_L + ML_DH, ML_L), BF16),
            pltpu.VMEM((nch * ML_HEADS, ML_DH + ML_NONES, ML_DH), F32),
        ],
        compiler_params=pltpu.CompilerParams(
            dimension_semantics=("arbitrary", "arbitrary", "arbitrary"), vmem_limit_bytes=VMEM_LIMIT),
        name="mlstm",
    )(qt4, k3, vt4, o3, rowq, colq, gnb)


def _na_kernel(q_ref, k_ref, v_ref, toep_ref, out_ref, bias_ref, *, rows):
    r0 = pl.program_id(1) * NA_R
    nkeys = NA_KH * GRID_W
    gw = NA_PACK * NA_DH

    @pl.when(pl.program_id(1) == 0)
    def _():
        for delta in range(NA_KH):
            lo = (NA_KH - 1 - delta) * GRID_W
            for h in range(NA_HEADS):
                g, hh = divmod(h, NA_PACK)
                bias_ref[delta, g, hh * GRID_W:(hh + 1) * GRID_W, :] = toep_ref[h, :, lo:lo + nkeys]

    row_blk = lax.broadcasted_iota(jnp.int32, (NA_PACK * GRID_W, gw), 0) // GRID_W
    lane_blk = lax.broadcasted_iota(jnp.int32, (NA_PACK * GRID_W, gw), 1) // NA_DH
    diag = row_blk == lane_blk
    out_blk = lax.broadcasted_iota(jnp.int32, (GRID_W, gw), 1) // NA_DH

    ng = NA_HEADS // NA_PACK

    def window(ri):
        r = r0 + ri
        rs = jnp.clip(r - NA_KH // 2, 0, rows - NA_KH)
        return pl.ds(pl.multiple_of(rs * GRID_W, GRID_W), nkeys), r - rs

    def row_body(ri, carry):
        keys, delta = window(ri)
        qoff = pl.multiple_of(ri * GRID_W, GRID_W)
        outs = []
        for g in range(ng):
            sl = slice(g * gw, (g + 1) * gw)
            q4 = q_ref[pl.ds(qoff, GRID_W), sl]
            qbd = jnp.where(diag, jnp.concatenate([q4] * NA_PACK, axis=0), jnp.zeros((), BF16))
            s = _dot_nt(qbd, k_ref[keys, sl]) + bias_ref[delta, g]
            p = jnp.exp2(s - jnp.max(s, axis=-1, keepdims=True))
            linv = 1.0 / jnp.sum(p, axis=-1, keepdims=True)
            o = _dot(p.astype(BF16), v_ref[keys, sl]) * linv
            og = o[(NA_PACK - 1) * GRID_W:, :]
            for h in range(NA_PACK - 2, -1, -1):
                og = jnp.where(out_blk == h, o[h * GRID_W:(h + 1) * GRID_W, :], og)
            outs.append(og)
        out_ref[pl.ds(qoff, GRID_W), :] = jnp.concatenate(outs, axis=-1).astype(BF16)
        return carry

    lax.fori_loop(0, NA_R, row_body, 0, unroll=True)


def _na(q3, k3, v3, toep):
    b, t, w = q3.shape
    rows = t // GRID_W
    tq = NA_R * GRID_W
    qtile = pl.BlockSpec((None, tq, w), lambda bi, i: (bi, i, 0))
    seq = pl.BlockSpec((None, t, w), lambda bi, i: (bi, 0, 0))
    toep_spec = pl.BlockSpec(toep.shape, lambda bi, i: (0, 0, 0), pipeline_mode=pl.Buffered(1))
    return pl.pallas_call(
        functools.partial(_na_kernel, rows=rows),
        grid=(b, rows // NA_R),
        in_specs=[qtile, seq, seq, toep_spec],
        out_specs=qtile,
        out_shape=jax.ShapeDtypeStruct((b, t, w), BF16),
        scratch_shapes=[pltpu.VMEM((NA_KH, NA_HEADS // NA_PACK, NA_PACK * GRID_W, NA_KH * GRID_W), F32)],
        compiler_params=pltpu.CompilerParams(
            dimension_semantics=("arbitrary", "arbitrary"), vmem_limit_bytes=VMEM_LIMIT),
        name="natten",
    )(q3, k3, v3, toep)


def _na_bias_table(rpb):
    c = np.arange(GRID_W)
    cs = np.clip(c - NA_KW // 2, 0, GRID_W - NA_KW)
    cc = np.arange(GRID_W)
    valid = (cc[None, :] >= cs[:, None]) & (cc[None, :] < cs[:, None] + NA_KW)
    nh, ndr, ndc = rpb.shape
    lead = GRID_W - NA_KW
    w = jnp.pad(rpb.astype(F32), ((0, 0), (0, 0), (lead, 2 * GRID_W - lead - ndc)))
    skew = jnp.broadcast_to(w[:, :, None, :], (nh, ndr, GRID_W, 2 * GRID_W)).reshape(nh, ndr, -1)
    skew = skew[:, :, :GRID_W * (2 * GRID_W - 1)].reshape(nh, ndr, GRID_W, 2 * GRID_W - 1)
    toep = jnp.where(valid[None, None], skew[..., GRID_W - 1:] * LOG2E, NEG).transpose(0, 2, 1, 3)
    return toep.reshape(nh, GRID_W, ndr * GRID_W)


def _outproj_ffn2_kernel(x1_ref, yml_ref, yna_ref, wout_ref, g2_ref, wg_ref, wu_ref, wd_ref,
                         gf_ref, out_ref):
    x2 = x1_ref[...] + _dot(jnp.concatenate([yml_ref[...], yna_ref[...]], axis=1), wout_ref[...])
    h = _rms(x2, g2_ref[...]).astype(BF16)
    x3 = x2 + _swiglu_half(h, wg_ref, wu_ref, wd_ref)
    out_ref[...] = _rms(x3, gf_ref[...])


def _outproj_ffn2(x1, yml, yna, wout, g2, wg, wu, wd, gf):
    n = x1.shape[0]
    tm = FFN_TM
    tok = lambda w: pl.BlockSpec((tm, w), lambda i: (i, 0))
    return pl.pallas_call(
        _outproj_ffn2_kernel,
        grid=(n // tm,),
        in_specs=[tok(D_MODEL), tok(ML_WIDTH), tok(NA_WIDTH)] + [
            _const_spec(a.shape) for a in (wout, g2, wg, wu, wd, gf)],
        out_specs=tok(D_MODEL),
        out_shape=jax.ShapeDtypeStruct((n, D_MODEL), F32),
        compiler_params=pltpu.CompilerParams(
            dimension_semantics=("parallel",), vmem_limit_bytes=VMEM_LIMIT),
        name="outproj_ffn2",
    )(x1, yml, yna, wout, g2, wg, wu, wd, gf)


def _layer(x, norm_ffn1, w1_gate, w1_up, w1_down, norm_mix, w_in, b_gates, conv_w, conv_b, w_q_ml,
           w_k_ml, gn_ml, gq_na, gk_na, rpb, w_out, norm_ffn2, w2_gate, w2_up, w2_down, norm_final):
    b, t, dm = x.shape
    n = b * t
    W = ML_WIDTH
    row = lambda a: a.reshape(1, -1).astype(F32)
    g0 = 3 * W
    wc = w_in[:, :W].astype(BF16)
    wo = w_in[:, 2 * W:g0].astype(BF16)
    wna = w_in[:, g0 + N_GATES:].astype(BF16)
    wgate = w_in[:, g0:g0 + N_GATES]
    H = ML_HEADS
    order = np.concatenate([dd * 2 * H + np.r_[0:2 * H, H:2 * H, 0:H] for dd in range(2)])
    wvg = jnp.concatenate([w_in[:, W:2 * W].T, wgate.T[order]], axis=0).astype(BF16)
    bgr = b_gates[order].reshape(-1, 1).astype(F32)
    ones = jnp.asarray(np.kron(np.eye(NA_HEADS), np.ones((NA_DH, NA_DH))), BF16)
    gq = jnp.tile(gq_na.astype(F32), NA_HEADS).reshape(1, -1)
    gk = jnp.tile(gk_na.astype(F32), NA_HEADS).reshape(1, -1)

    x1, c, vt_ml, o, ga, gb, qn, kn, vn = _ffn1_inproj(
        x.reshape(n, dm), row(norm_ffn1), w1_gate.astype(BF16), w1_up.astype(BF16), w1_down.astype(BF16),
        row(norm_mix), wc, wvg, wo, wna, bgr, ones, gq, gk)

    seq = lambda a: a.reshape(b, t, a.shape[-1])
    qt_ml, k_ml = _conv_qk(seq(c), conv_w.astype(F32), row(conv_b),
                           w_q_ml.transpose(0, 2, 1).astype(BF16), w_k_ml.astype(BF16))
    rowq, colq = _mlstm_gates(ga.reshape(2, -1, ML_L), gb.reshape(2, -1, ML_L), b, t)
    gnb = jnp.broadcast_to(gn_ml.astype(F32).reshape(W, 1), (W, ML_L))
    y_ml = _mlstm(qt_ml, k_ml, vt_ml.reshape(b, t // ML_L, W, ML_L), seq(o), rowq, colq, gnb)
    y_na = _na(seq(qn), seq(kn), seq(vn), _na_bias_table(rpb))

    out = _outproj_ffn2(x1, y_ml.reshape(n, W), y_na.reshape(n, NA_WIDTH),
                        w_out.astype(BF16), row(norm_ffn2),
                        w2_gate.astype(BF16), w2_up.astype(BF16), w2_down.astype(BF16), row(norm_final))
    return out.reshape(b, t, dm)


def kernel(x, norm_ffn1, w1_gate, w1_up, w1_down, norm_mix, w_in, b_gates, conv_w, conv_b, w_q_ml, w_k_ml,
           gn_ml, gq_na, gk_na, rpb, w_out, norm_ffn2, w2_gate, w2_up, w2_down, norm_final):
    depth = norm_ffn1.shape[0]
    for l in range(depth):
        x = _layer(x, norm_ffn1[l], w1_gate[l], w1_up[l], w1_down[l], norm_mix[l], w_in[l], b_gates[l],
                   conv_w[l], conv_b[l], w_q_ml[l], w_k_ml[l], gn_ml[l], gq_na[l], gk_na[l], rpb[l],
                   w_out[l], norm_ffn2[l], w2_gate[l], w2_up[l], w2_down[l], norm_final[l])
    return x
```

```python
import functools

import jax
import jax.numpy as jnp
import numpy as np
from jax import lax
from jax.experimental import pallas as pl
from jax.experimental.pallas import tpu as pltpu

F32 = jnp.float32
BF16 = jnp.bfloat16

D_MODEL = 1024
D_FF = 2816
GRID_W = 64
ML_HEADS = 4
ML_DH = 128
ML_WIDTH = ML_HEADS * ML_DH
CONV_W = 5
NA_HEADS = 8
NA_DH = 64
NA_WIDTH = NA_HEADS * NA_DH
NA_KH = 8
NA_KW = 16
N_GATES = 4 * ML_HEADS
EPS = 1e-6
NEG = -1e30
LOG2E = 1.4426950408889634

SUBLANES = 8
LANES = 128

FFN_TM = 512
FFN_CK = 2816
FFN_STAGE_ROWS = 64
ML_L = 128
ML_TT = 1024
ML_NROWQ = 6
ML_NONES = 16
CONV_TT = 1024
NA_R = 16
NA_PACK = 4
VMEM_LIMIT = 56 * 1024 * 1024


def _dot(a, b):
    return jnp.dot(a, b, preferred_element_type=F32)


def _dot_nt(a, b):
    return lax.dot_general(a, b, (((1,), (1,)), ((), ())), preferred_element_type=F32)


def _dot_tn(a, b):
    return lax.dot_general(a, b, (((0,), (0,)), ((), ())), preferred_element_type=F32)


def _dot_exact(a, b):
    return jnp.dot(a, b, preferred_element_type=F32, precision=lax.Precision.HIGHEST)


def _rms(x, g):
    ms = jnp.mean(x * x, axis=-1, keepdims=True)
    return x * lax.rsqrt(ms + EPS) * g


def _log_sigmoid(x):
    return jnp.minimum(x, 0.0) - jnp.log1p(jnp.exp(-jnp.abs(x)))


def _swiglu_half(h, wg_ref, wu_ref, wd_ref):
    acc = None
    for j in range(D_FF // FFN_CK):
        sl = slice(j * FFN_CK, (j + 1) * FFN_CK)
        g = _dot(h, wg_ref[:, sl])
        u = _dot(h, wu_ref[:, sl])
        a = (g * jax.nn.sigmoid(g) * u).astype(BF16)
        part = _dot(a, wd_ref[sl, :])
        acc = part if acc is None else acc + part
    return 0.5 * acc


def _stage_rows(cols):
    return FFN_STAGE_ROWS * D_FF // cols


def _stage_scratch(cols):
    return [pltpu.VMEM((2, _stage_rows(cols), cols), F32), pltpu.SemaphoreType.DMA((2,))]


def _load_weights_as_bf16(jobs, stages):
    chunks, issued = [], {}
    for src, dst in jobs:
        rows, cols = src.shape
        step = _stage_rows(cols)
        for r0 in range(0, rows, step):
            slot = issued.get(cols, 0) % 2
            issued[cols] = issued.get(cols, 0) + 1
            chunks.append((src, dst, r0, min(step, rows - r0), cols, slot))

    def copy_of(chunk):
        src, _, r0, nr, cols, slot = chunk
        stage, sem = stages[cols]
        return pltpu.make_async_copy(src.at[pl.ds(r0, nr), :], stage.at[slot, pl.ds(0, nr), :], sem.at[slot])

    copy_of(chunks[0]).start()
    for k, chunk in enumerate(chunks):
        if k + 1 < len(chunks):
            copy_of(chunks[k + 1]).start()
        copy_of(chunk).wait()
        _, dst, r0, nr, cols, slot = chunk
        dst[r0:r0 + nr, :] = stages[cols][0][slot, 0:nr, :].astype(BF16)


def _group_mean_sq(x, ones_ref):
    xx = x * x
    return _dot(xx.astype(BF16), ones_ref[...]) * (1.0 / NA_DH)


def _ffn1_inproj_kernel(x_ref, g1_ref, wg_hbm, wu_hbm, wd_hbm, gm_ref, wc_ref, wvg_ref, wo_ref, wna_ref,
                        bgr_ref, ones_ref, gq_ref, gk_ref,
                        x1_ref, c_ref, vt_ref, o_ref, ga_ref, gb_ref, qn_ref, kn_ref, vn_ref,
                        wg_ref, wu_ref, wd_ref, stage_ff, sem_ff, stage_dm, sem_dm):
    @pl.when(pl.program_id(0) == 0)
    def _():
        _load_weights_as_bf16([(wg_hbm, wg_ref), (wu_hbm, wu_ref), (wd_hbm, wd_ref)],
                              {D_FF: (stage_ff, sem_ff), D_MODEL: (stage_dm, sem_dm)})

    x = x_ref[...]
    h = _rms(x, g1_ref[...]).astype(BF16)
    x1 = x + _swiglu_half(h, wg_ref, wu_ref, wd_ref)
    x1_ref[...] = x1
    h2 = _rms(x1, gm_ref[...]).astype(BF16)
    W = ML_WIDTH
    c_ref[...] = _dot(h2, wc_ref[...])
    o_ref[...] = jax.nn.sigmoid(_dot(h2, wo_ref[...])).astype(BF16)
    vg = _dot_nt(wvg_ref[...], h2)
    vt = vg[:W].astype(BF16)
    for cidx in range(FFN_TM // ML_L):
        vt_ref[cidx] = vt[:, cidx * ML_L:(cidx + 1) * ML_L]
    p = _dot(h2, wna_ref[...])
    q = p[:, 0:W]
    k = p[:, W:2 * W]
    vn_ref[...] = p[:, 2 * W:3 * W].astype(BF16)
    qn = q * lax.rsqrt(_group_mean_sq(q, ones_ref) + EPS) * gq_ref[...] * (NA_DH ** -0.5 * LOG2E)
    kn = k * lax.rsqrt(_group_mean_sq(k, ones_ref) + EPS) * gk_ref[...]
    qn_ref[...] = qn.astype(BF16)
    kn_ref[...] = kn.astype(BF16)
    gr = vg[W:] + bgr_ref[...]
    row = lax.broadcasted_iota(jnp.int32, gr.shape, 0) % (4 * ML_HEADS)
    gr = jnp.where((row >= ML_HEADS) & (row < 3 * ML_HEADS), _log_sigmoid(gr), gr)
    for dd in range(2):
        for cidx in range(FFN_TM // ML_L):
            lanes = slice(cidx * ML_L, (cidx + 1) * ML_L)
            ga_ref[dd, cidx] = gr[dd * 16:dd * 16 + 8, lanes]
            gb_ref[dd, cidx] = gr[dd * 16 + 8:dd * 16 + 16, lanes]


def _const_spec(shape):
    nd = len(shape)
    return pl.BlockSpec(shape, lambda *_: (0,) * nd, pipeline_mode=pl.Buffered(1))


def _ffn1_inproj(x2d, g1, wg, wu, wd, gm, wc, wvg, wo, wna, bgr, ones, gq, gk):
    n = x2d.shape[0]
    tm = FFN_TM
    W = ML_WIDTH
    tok = lambda w: pl.BlockSpec((tm, w), lambda i: (i, 0))
    out_shape = (
        jax.ShapeDtypeStruct((n, D_MODEL), F32),
        jax.ShapeDtypeStruct((n, W), F32),
        jax.ShapeDtypeStruct((n // ML_L, W, ML_L), BF16),
        jax.ShapeDtypeStruct((n, W), BF16),
        jax.ShapeDtypeStruct((2, n // ML_L, SUBLANES, ML_L), F32),
        jax.ShapeDtypeStruct((2, n // ML_L, SUBLANES, ML_L), F32),
        jax.ShapeDtypeStruct((n, W), BF16),
        jax.ShapeDtypeStruct((n, W), BF16),
        jax.ShapeDtypeStruct((n, W), BF16),
    )
    out_specs = (
        tok(D_MODEL), tok(W), pl.BlockSpec((tm // ML_L, W, ML_L), lambda i: (i, 0, 0)), tok(W),
        pl.BlockSpec((2, tm // ML_L, SUBLANES, ML_L), lambda i: (0, i, 0, 0)),
        pl.BlockSpec((2, tm // ML_L, SUBLANES, ML_L), lambda i: (0, i, 0, 0)),
        tok(W), tok(W), tok(W),
    )
    hbm = pl.BlockSpec(memory_space=pl.ANY)
    in_specs = [tok(D_MODEL), _const_spec(g1.shape), hbm, hbm, hbm] + [
        _const_spec(a.shape) for a in (gm, wc, wvg, wo, wna, bgr, ones, gq, gk)]
    return pl.pallas_call(
        _ffn1_inproj_kernel,
        grid=(n // tm,),
        in_specs=in_specs,
        out_specs=out_specs,
        out_shape=out_shape,
        scratch_shapes=[pltpu.VMEM(a.shape, BF16) for a in (wg, wu, wd)]
        + _stage_scratch(D_FF) + _stage_scratch(D_MODEL),
        compiler_params=pltpu.CompilerParams(
            dimension_semantics=("arbitrary",), vmem_limit_bytes=VMEM_LIMIT),
        name="ffn1_inproj",
    )(x2d, g1, wg, wu, wd, gm, wc, wvg, wo, wna, bgr, ones, gq, gk)


def _conv_qk_kernel(c_ref, prev_ref, next_ref, cw_ref, cb_ref, wqt_ref, wk_ref, qt_ref, k_ref, pad_ref):
    i = pl.program_id(1)
    nb = pl.num_programs(1)
    tt = CONV_TT
    half = CONV_W // 2
    pad_ref[0:SUBLANES, :] = jnp.where(i > 0, prev_ref[...], 0.0)
    pad_ref[SUBLANES:SUBLANES + tt, :] = c_ref[...]
    pad_ref[SUBLANES + tt:2 * SUBLANES + tt, :] = jnp.where(i < nb - 1, next_ref[...], 0.0)
    y = cb_ref[...]
    for kk in range(CONV_W):
        start = SUBLANES + kk - half
        y = y + pad_ref[start:start + tt, :] * cw_ref[kk:kk + 1, :]
    u = y * jax.nn.sigmoid(y)
    for h in range(ML_HEADS):
        sl = slice(h * ML_DH, (h + 1) * ML_DH)
        uh = u[:, sl].astype(BF16)
        k_ref[:, sl] = (_dot(uh, wk_ref[h]) * (ML_DH ** -0.5)).astype(BF16)
        qt = _dot_nt(wqt_ref[h], uh).astype(BF16)
        for cidx in range(tt // ML_L):
            qt_ref[cidx, sl, :] = qt[:, cidx * ML_L:(cidx + 1) * ML_L]


def _conv_qk(c3, cw, cb, wqt, wk):
    b, t, w = c3.shape
    tt = CONV_TT
    nb = t // tt
    per = tt // SUBLANES
    last = t // SUBLANES - 1
    tile = pl.BlockSpec((None, tt, w), lambda bi, i: (bi, i, 0))
    prev = pl.BlockSpec((None, SUBLANES, w), lambda bi, i: (bi, jnp.maximum(i * per - 1, 0), 0))
    nxt = pl.BlockSpec((None, SUBLANES, w), lambda bi, i: (bi, jnp.minimum((i + 1) * per, last), 0))
    return pl.pallas_call(
        _conv_qk_kernel,
        grid=(b, nb),
        in_specs=[tile, prev, nxt] + [
            pl.BlockSpec(a.shape, functools.partial(lambda nd, bi, i: (0,) * nd, a.ndim))
            for a in (cw, cb, wqt, wk)],
        out_specs=(pl.BlockSpec((None, tt // ML_L, w, ML_L), lambda bi, i: (bi, i, 0, 0)), tile),
        out_shape=(jax.ShapeDtypeStruct((b, t // ML_L, w, ML_L), BF16),
                   jax.ShapeDtypeStruct((b, t, w), BF16)),
        scratch_shapes=[pltpu.VMEM((tt + 2 * SUBLANES, w), F32)],
        compiler_params=pltpu.CompilerParams(dimension_semantics=("parallel", "parallel")),
        name="conv_qk",
    )(c3, c3, c3, cw, cb, wqt, wk)


def _lane_scan(x, pos, d, combine, ident):
    L = x.shape[-1]
    k = 1
    while k < L:
        shifted = pltpu.roll(x, jnp.where(d == 0, k, L - k), axis=1)
        x = combine(x, jnp.where(pos >= k, shifted, ident))
        k *= 2
    return x


def _mlstm_gates_kernel(ga_ref, gb_ref, rowq_ref, colq_ref, b_s, g_s, mloc_s, rmax_s, w_s, vec_s):
    d = pl.program_id(1)
    R = SUBLANES
    L = ML_L
    nc = ga_ref.shape[0] // R
    xa = ga_ref[...]
    xb = gb_ref[...]
    lane = lax.broadcasted_iota(jnp.int32, xa.shape, 1)
    pos = lane + d * (L - 1 - 2 * lane)
    b = _lane_scan(xb, pos, d, jnp.add, 0.0)
    g = jnp.broadcast_to(jnp.sum(xb, axis=1, keepdims=True), xb.shape)
    w = xa - b
    a = g + w
    m_loc = jnp.broadcast_to(jnp.max(a, axis=1, keepdims=True), a.shape)
    rowq_ref[3] = jnp.exp(a - m_loc)
    b_s[...] = b
    g_s[...] = g
    mloc_s[...] = m_loc
    rmax_s[...] = _lane_scan(w, pos, d, jnp.maximum, NEG)
    w_s[...] = w
    vec_s[...] = jnp.zeros_like(vec_s)

    def chunk(k, m0):
        ck = k + d * (nc - 1 - 2 * k)
        rows = pl.ds(pl.multiple_of(ck * R, R), R)
        gk = g_s[rows, :]
        mlk = mloc_s[rows, :]
        mm = jnp.maximum(m0, rmax_s[rows, :])
        m_new = jnp.maximum(gk + m0, mlk)
        rowq_ref[0, rows, :] = mm * LOG2E
        rowq_ref[1, rows, :] = jnp.exp(m0 - mm)
        rowq_ref[2, rows, :] = jnp.exp(-(b_s[rows, :] + mm))
        rowq_ref[4, rows, :] = jnp.exp(gk + m0 - m_new)
        rowq_ref[5, rows, :] = jnp.exp(mlk - m_new)
        vec_s[ck, 0:R, :] = w_s[rows, :] * LOG2E
        colq_ref[ck] = vec_s[ck].T
        return m_new

    lax.fori_loop(0, nc, chunk, jnp.zeros((R, L), F32), unroll=8)


def _mlstm_gates(ga, gb, b, t):
    nc = t // ML_L
    R = SUBLANES
    in_spec = pl.BlockSpec((None, nc * R, ML_L), lambda bi, d: (d, bi, 0))
    return pl.pallas_call(
        _mlstm_gates_kernel,
        grid=(b, 2),
        in_specs=[in_spec, in_spec],
        out_specs=(pl.BlockSpec((None, None, ML_NROWQ, nc * R, ML_L), lambda bi, d: (bi, d, 0, 0, 0)),
                   pl.BlockSpec((None, None, nc, ML_L, LANES), lambda bi, d: (bi, d, 0, 0, 0))),
        out_shape=(jax.ShapeDtypeStruct((b, 2, ML_NROWQ, nc * R, ML_L), F32),
                   jax.ShapeDtypeStruct((b, 2, nc, ML_L, LANES), F32)),
        scratch_shapes=[pltpu.VMEM((nc * R, ML_L), F32)] * 5 + [pltpu.VMEM((nc, LANES, ML_L), F32)],
        compiler_params=pltpu.CompilerParams(dimension_semantics=("parallel", "parallel")),
        name="mlstm_gates",
    )(ga, gb)


def _mlstm_kernel(qt_ref, k_ref, vt_ref, o_ref, rowq_ref, colq_ref, gnb_ref, y_ref,
                  hf_ref, state_ref, sq_s, upd_s, *, nblk):
    d = pl.program_id(1)
    i = pl.program_id(2)
    blk = i + d * (nblk - 1 - 2 * i)
    L = ML_L
    nch = ML_TT // L
    H = ML_HEADS
    DH = ML_DH
    R = SUBLANES

    @pl.when(i == 0)
    def _():
        state_ref[...] = jnp.zeros_like(state_ref)

    s_idx = lax.broadcasted_iota(jnp.int32, (L, L), 0)
    t_idx = lax.broadcasted_iota(jnp.int32, (L, L), 1)
    fwd = d == 0
    mask = (t_idx - s_idx) * (1 - 2 * d) >= 0
    ones = jnp.ones((ML_NONES, L), BF16)
    NA = DH + ML_NONES
    c_total = hf_ref.shape[0] - nch
    hbase = jnp.where(fwd, blk * nch, c_total)

    def chunk_of(j):
        cj = j + d * (nch - 1 - 2 * j)
        return cj, pl.ds(pl.multiple_of(cj * R, R), R)

    for j in range(nch):
        cj, rows = chunk_of(j)
        off = pl.multiple_of(cj * L, L)
        mm = rowq_ref[0, rows, :]
        sc = rowq_ref[1, rows, :]
        w_end = rowq_ref[3, rows, :]
        cols = colq_ref[cj]
        for h in range(0, H, 2):
            ka = k_ref[pl.ds(off, L), h * DH:(h + 1) * DH]
            kb = k_ref[pl.ds(off, L), (h + 1) * DH:(h + 2) * DH]
            qta = qt_ref[cj, h * DH:(h + 1) * DH, :]
            qtb = qt_ref[cj, (h + 1) * DH:(h + 2) * DH, :]
            zero = jnp.zeros((DH, L), BF16)
            st2 = _dot(jnp.concatenate([ka, kb], axis=1),
                       jnp.concatenate([jnp.concatenate([qta, zero], axis=1),
                                        jnp.concatenate([zero, qtb], axis=1)], axis=0))
            vws = []
            for hh, qt in ((h, qta), (h + 1, qtb)):
                sl = slice(hh * DH, (hh + 1) * DH)
                lanes = slice((hh - h) * L, (hh - h + 1) * L)
                vaug = jnp.concatenate([vt_ref[cj, sl, :], ones], axis=0)
                e = jnp.exp2(jnp.where(mask, cols[:, hh:hh + 1] - mm[hh:hh + 1, :], NEG))
                sq_s[j * H + hh, 0:L, :] = (st2[:, lanes] * e).astype(BF16)
                sq_s[j * H + hh, L:L + DH, :] = qt * sc[hh:hh + 1, :].astype(BF16)
                vws.append(vaug * w_end[hh:hh + 1, :].astype(BF16))
            upd2 = _dot(jnp.concatenate(vws, axis=1),
                        jnp.concatenate([jnp.concatenate([ka, zero], axis=1),
                                         jnp.concatenate([zero, kb], axis=1)], axis=0))
            upd_s[j * H + h] = upd2[:, :DH]
            upd_s[j * H + h + 1] = upd2[:, DH:]

    for j in range(nch):
        cj, rows = chunk_of(j)
        em = rowq_ref[2, rows, :]
        s_old = rowq_ref[4, rows, :]
        s_loc = rowq_ref[5, rows, :]
        for h in range(H):
            sl = slice(h * DH, (h + 1) * DH)
            vaug = jnp.concatenate([vt_ref[cj, sl, :], ones], axis=0)
            st = state_ref[h]
            r = _dot(jnp.concatenate([vaug, st.astype(BF16)], axis=1), sq_s[j * H + h])
            inv = 1.0 / jnp.maximum(jnp.abs(r[DH:DH + 1, :]), em[h:h + 1, :])
            hf_ref[hbase + cj, sl, :] = r[:DH, :] * inv
            state_ref[h] = (jnp.broadcast_to(s_old[h:h + 1, :], (NA, DH)) * st
                            + jnp.broadcast_to(s_loc[h:h + 1, :], (NA, DH)) * upd_s[j * H + h])

    @pl.when(jnp.logical_not(fwd))
    def _():
        for c in range(nch):
            for h in range(H):
                sl = slice(h * DH, (h + 1) * DH)
                tot = hf_ref[blk * nch + c, sl, :] + hf_ref[c_total + c, sl, :]
                ms = jnp.mean(tot * tot, axis=0, keepdims=True)
                hn = (tot * lax.rsqrt(ms + EPS) * gnb_ref[sl, :]).T
                tok = slice(c * L, (c + 1) * L)
                y_ref[tok, sl] = (o_ref[tok, sl].astype(F32) * hn).astype(BF16)


def _mlstm(qt4, k3, vt4, o3, rowq, colq, gnb):
    b, t, w = k3.shape
    tt = ML_TT
    nblk = t // tt
    nch = tt // ML_L
    blk_of = lambda d, i: i + d * (nblk - 1 - 2 * i)
    tile = pl.BlockSpec((None, tt, w), lambda bi, d, i: (bi, blk_of(d, i), 0))
    ft_spec = pl.BlockSpec((None, nch, w, ML_L), lambda bi, d, i: (bi, blk_of(d, i), 0, 0))
    rq_spec = pl.BlockSpec((None, None, ML_NROWQ, nch * SUBLANES, ML_L),
                           lambda bi, d, i: (bi, d, 0, blk_of(d, i), 0))
    cq_spec = pl.BlockSpec((None, None, nch, ML_L, LANES), lambda bi, d, i: (bi, d, blk_of(d, i), 0, 0))
    gn_spec = pl.BlockSpec(gnb.shape, lambda bi, d, i: (0, 0))
    y_spec = pl.BlockSpec((None, tt, w), lambda bi, d, i: (bi, d * blk_of(d, i) + (1 - d) * (nblk - 1), 0))
    return pl.pallas_call(
        functools.partial(_mlstm_kernel, nblk=nblk),
        grid=(b, 2, nblk),
        in_specs=[ft_spec, tile, ft_spec, tile, rq_spec, cq_spec, gn_spec],
        out_specs=y_spec,
        out_shape=jax.ShapeDtypeStruct((b, t, w), BF16),
        scratch_shapes=[
            pltpu.VMEM((t // ML_L + nch, w, ML_L), F32),
            pltpu.VMEM((ML_HEADS, ML_DH + ML_NONES, ML_DH), F32),
            pltpu.VMEM((nch * ML_HEADS, ML_L + ML_DH, ML_L), BF16),
            pltpu.VMEM((nch * ML_HEADS, ML_DH + ML_NONES, ML_DH), F32),
        ],
        compiler_params=pltpu.CompilerParams(
            dimension_semantics=("arbitrary", "arbitrary", "arbitrary"), vmem_limit_bytes=VMEM_LIMIT),
        name="mlstm",
    )(qt4, k3, vt4, o3, rowq, colq, gnb)


def _na_kernel(q_ref, k_ref, v_ref, toep_ref, out_ref, bias_ref, *, rows):
    r0 = pl.program_id(1) * NA_R
    nkeys = NA_KH * GRID_W
    gw = NA_PACK * NA_DH

    @pl.when(pl.program_id(1) == 0)
    def _():
        for delta in range(NA_KH):
            lo = (NA_KH - 1 - delta) * GRID_W
            for h in range(NA_HEADS):
                g, hh = divmod(h, NA_PACK)
                bias_ref[delta, g, hh * GRID_W:(hh + 1) * GRID_W, :] = toep_ref[h, :, lo:lo + nkeys]

    row_blk = lax.broadcasted_iota(jnp.int32, (NA_PACK * GRID_W, gw), 0) // GRID_W
    lane_blk = lax.broadcasted_iota(jnp.int32, (NA_PACK * GRID_W, gw), 1) // NA_DH
    diag = row_blk == lane_blk
    out_blk = lax.broadcasted_iota(jnp.int32, (GRID_W, gw), 1) // NA_DH

    ng = NA_HEADS // NA_PACK

    def window(ri):
        r = r0 + ri
        rs = jnp.clip(r - NA_KH // 2, 0, rows - NA_KH)
        return pl.ds(pl.multiple_of(rs * GRID_W, GRID_W), nkeys), r - rs

    def row_body(ri, carry):
        keys, delta = window(ri)
        qoff = pl.multiple_of(ri * GRID_W, GRID_W)
        outs = []
        for g in range(ng):
            sl = slice(g * gw, (g + 1) * gw)
            q4 = q_ref[pl.ds(qoff, GRID_W), sl]
            qbd = jnp.where(diag, jnp.concatenate([q4] * NA_PACK, axis=0), jnp.zeros((), BF16))
            s = _dot_nt(qbd, k_ref[keys, sl]) + bias_ref[delta, g]
            p = jnp.exp2(s - jnp.max(s, axis=-1, keepdims=True))
            linv = 1.0 / jnp.sum(p, axis=-1, keepdims=True)
            o = _dot(p.astype(BF16), v_ref[keys, sl]) * linv
            og = o[(NA_PACK - 1) * GRID_W:, :]
            for h in range(NA_PACK - 2, -1, -1):
                og = jnp.where(out_blk == h, o[h * GRID_W:(h + 1) * GRID_W, :], og)
            outs.append(og)
        out_ref[pl.ds(qoff, GRID_W), :] = jnp.concatenate(outs, axis=-1).astype(BF16)
        return carry

    lax.fori_loop(0, NA_R, row_body, 0, unroll=True)


def _na(q3, k3, v3, toep):
    b, t, w = q3.shape
    rows = t // GRID_W
    tq = NA_R * GRID_W
    qtile = pl.BlockSpec((None, tq, w), lambda bi, i: (bi, i, 0))
    seq = pl.BlockSpec((None, t, w), lambda bi, i: (bi, 0, 0))
    toep_spec = pl.BlockSpec(toep.shape, lambda bi, i: (0, 0, 0), pipeline_mode=pl.Buffered(1))
    return pl.pallas_call(
        functools.partial(_na_kernel, rows=rows),
        grid=(b, rows // NA_R),
        in_specs=[qtile, seq, seq, toep_spec],
        out_specs=qtile,
        out_shape=jax.ShapeDtypeStruct((b, t, w), BF16),
        scratch_shapes=[pltpu.VMEM((NA_KH, NA_HEADS // NA_PACK, NA_PACK * GRID_W, NA_KH * GRID_W), F32)],
        compiler_params=pltpu.CompilerParams(
            dimension_semantics=("arbitrary", "arbitrary"), vmem_limit_bytes=VMEM_LIMIT),
        name="natten",
    )(q3, k3, v3, toep)


def _na_bias_table(rpb):
    c = np.arange(GRID_W)
    cs = np.clip(c - NA_KW // 2, 0, GRID_W - NA_KW)
    cc = np.arange(GRID_W)
    valid = (cc[None, :] >= cs[:, None]) & (cc[None, :] < cs[:, None] + NA_KW)
    nh, ndr, ndc = rpb.shape
    lead = GRID_W - NA_KW
    w = jnp.pad(rpb.astype(F32), ((0, 0), (0, 0), (lead, 2 * GRID_W - lead - ndc)))
    skew = jnp.broadcast_to(w[:, :, None, :], (nh, ndr, GRID_W, 2 * GRID_W)).reshape(nh, ndr, -1)
    skew = skew[:, :, :GRID_W * (2 * GRID_W - 1)].reshape(nh, ndr, GRID_W, 2 * GRID_W - 1)
    toep = jnp.where(valid[None, None], skew[..., GRID_W - 1:] * LOG2E, NEG).transpose(0, 2, 1, 3)
    return toep.reshape(nh, GRID_W, ndr * GRID_W)


def _outproj_ffn2_kernel(x1_ref, yml_ref, yna_ref, wout_hbm, g2_ref, wg_hbm, wu_hbm, wd_hbm, gf_ref, out_ref,
                         wout_ref, wg_ref, wu_ref, wd_ref, stage_ff, sem_ff, stage_dm, sem_dm):
    @pl.when(pl.program_id(0) == 0)
    def _():
        _load_weights_as_bf16(
            [(wout_hbm, wout_ref), (wg_hbm, wg_ref), (wu_hbm, wu_ref), (wd_hbm, wd_ref)],
            {D_FF: (stage_ff, sem_ff), D_MODEL: (stage_dm, sem_dm)})

    x2 = x1_ref[...] + _dot(jnp.concatenate([yml_ref[...], yna_ref[...]], axis=1), wout_ref[...])
    h = _rms(x2, g2_ref[...]).astype(BF16)
    x3 = x2 + _swiglu_half(h, wg_ref, wu_ref, wd_ref)
    out_ref[...] = _rms(x3, gf_ref[...])


def _outproj_ffn2(x1, yml, yna, wout, g2, wg, wu, wd, gf):
    n = x1.shape[0]
    tm = FFN_TM
    tok = lambda w: pl.BlockSpec((tm, w), lambda i: (i, 0))
    hbm = pl.BlockSpec(memory_space=pl.ANY)
    return pl.pallas_call(
        _outproj_ffn2_kernel,
        grid=(n // tm,),
        in_specs=[tok(D_MODEL), tok(ML_WIDTH), tok(NA_WIDTH), hbm, _const_spec(g2.shape), hbm, hbm, hbm,
                  _const_spec(gf.shape)],
        out_specs=tok(D_MODEL),
        out_shape=jax.ShapeDtypeStruct((n, D_MODEL), F32),
        scratch_shapes=[pltpu.VMEM(a.shape, BF16) for a in (wout, wg, wu, wd)]
        + _stage_scratch(D_FF) + _stage_scratch(D_MODEL),
        compiler_params=pltpu.CompilerParams(
            dimension_semantics=("arbitrary",), vmem_limit_bytes=VMEM_LIMIT),
        name="outproj_ffn2",
    )(x1, yml, yna, wout, g2, wg, wu, wd, gf)


def _layer(x, norm_ffn1, w1_gate, w1_up, w1_down, norm_mix, w_in, b_gates, conv_w, conv_b, w_q_ml,
           w_k_ml, gn_ml, gq_na, gk_na, rpb, w_out, norm_ffn2, w2_gate, w2_up, w2_down, norm_final):
    b, t, dm = x.shape
    n = b * t
    W = ML_WIDTH
    row = lambda a: a.reshape(1, -1).astype(F32)
    g0 = 3 * W
    wc = w_in[:, :W].astype(BF16)
    wo = w_in[:, 2 * W:g0].astype(BF16)
    wna = w_in[:, g0 + N_GATES:].astype(BF16)
    wgate = w_in[:, g0:g0 + N_GATES]
    H = ML_HEADS
    order = np.concatenate([dd * 2 * H + np.r_[0:2 * H, H:2 * H, 0:H] for dd in range(2)])
    wvg = jnp.concatenate([w_in[:, W:2 * W].T, wgate.T[order]], axis=0).astype(BF16)
    bgr = b_gates[order].reshape(-1, 1).astype(F32)
    ones = jnp.asarray(np.kron(np.eye(NA_HEADS), np.ones((NA_DH, NA_DH))), BF16)
    gq = jnp.tile(gq_na.astype(F32), NA_HEADS).reshape(1, -1)
    gk = jnp.tile(gk_na.astype(F32), NA_HEADS).reshape(1, -1)

    x1, c, vt_ml, o, ga, gb, qn, kn, vn = _ffn1_inproj(
        x.reshape(n, dm), row(norm_ffn1), w1_gate.astype(F32), w1_up.astype(F32), w1_down.astype(F32),
        row(norm_mix), wc, wvg, wo, wna, bgr, ones, gq, gk)

    seq = lambda a: a.reshape(b, t, a.shape[-1])
    qt_ml, k_ml = _conv_qk(seq(c), conv_w.astype(F32), row(conv_b),
                           w_q_ml.transpose(0, 2, 1).astype(BF16), w_k_ml.astype(BF16))
    rowq, colq = _mlstm_gates(ga.reshape(2, -1, ML_L), gb.reshape(2, -1, ML_L), b, t)
    gnb = jnp.broadcast_to(gn_ml.astype(F32).reshape(W, 1), (W, ML_L))
    y_ml = _mlstm(qt_ml, k_ml, vt_ml.reshape(b, t // ML_L, W, ML_L), seq(o), rowq, colq, gnb)
    y_na = _na(seq(qn), seq(kn), seq(vn), _na_bias_table(rpb))

    out = _outproj_ffn2(x1, y_ml.reshape(n, W), y_na.reshape(n, NA_WIDTH),
                        w_out.astype(F32), row(norm_ffn2),
                        w2_gate.astype(F32), w2_up.astype(F32), w2_down.astype(F32), row(norm_final))
    return out.reshape(b, t, dm)


def kernel(x, norm_ffn1, w1_gate, w1_up, w1_down, norm_mix, w_in, b_gates, conv_w, conv_b, w_q_ml, w_k_ml,
           gn_ml, gq_na, gk_na, rpb, w_out, norm_ffn2, w2_gate, w2_up, w2_down, norm_final):
    depth = norm_ffn1.shape[0]
    for l in range(depth):
        x = _layer(x, norm_ffn1[l], w1_gate[l], w1_up[l], w1_down[l], norm_mix[l], w_in[l], b_gates[l],
                   conv_w[l], conv_b[l], w_q_ml[l], w_k_ml[l], gn_ml[l], gq_na[l], gk_na[l], rpb[l],
                   w_out[l], norm_ffn2[l], w2_gate[l], w2_up[l], w2_down[l], norm_final[l])
    return x
```

```python
import functools

import jax
import jax.numpy as jnp
import numpy as np
from jax import lax
from jax.experimental import pallas as pl
from jax.experimental.pallas import tpu as pltpu

F32 = jnp.float32
BF16 = jnp.bfloat16

D_MODEL = 1024
D_FF = 2816
GRID_W = 64
ML_HEADS = 4
ML_DH = 128
ML_WIDTH = ML_HEADS * ML_DH
CONV_W = 5
NA_HEADS = 8
NA_DH = 64
NA_WIDTH = NA_HEADS * NA_DH
NA_KH = 8
NA_KW = 16
N_GATES = 4 * ML_HEADS
EPS = 1e-6
NEG = -1e30
LOG2E = 1.4426950408889634

SUBLANES = 8
LANES = 128

FFN_TM = 512
FFN_CK = 2816
FFN_STAGE_ROWS = 64
FFN_STAGE_DEPTH = 4
ML_L = 128
ML_TT = 1024
ML_NROWQ = 6
ML_NONES = 16
CONV_TT = 1024
NA_R = 16
NA_PACK = 4
VMEM_LIMIT = 56 * 1024 * 1024


def _dot(a, b):
    return jnp.dot(a, b, preferred_element_type=F32)


def _dot_nt(a, b):
    return lax.dot_general(a, b, (((1,), (1,)), ((), ())), preferred_element_type=F32)


def _dot_tn(a, b):
    return lax.dot_general(a, b, (((0,), (0,)), ((), ())), preferred_element_type=F32)


def _dot_exact(a, b):
    return jnp.dot(a, b, preferred_element_type=F32, precision=lax.Precision.HIGHEST)


def _rms(x, g):
    ms = jnp.mean(x * x, axis=-1, keepdims=True)
    return x * lax.rsqrt(ms + EPS) * g


def _log_sigmoid(x):
    return jnp.minimum(x, 0.0) - jnp.log1p(jnp.exp(-jnp.abs(x)))


def _swiglu_half(h, wg_ref, wu_ref, wd_ref):
    acc = None
    for j in range(D_FF // FFN_CK):
        sl = slice(j * FFN_CK, (j + 1) * FFN_CK)
        g = _dot(h, wg_ref[:, sl])
        u = _dot(h, wu_ref[:, sl])
        a = (g * jax.nn.sigmoid(g) * u).astype(BF16)
        part = _dot(a, wd_ref[sl, :])
        acc = part if acc is None else acc + part
    return 0.5 * acc


def _stage_rows(cols):
    return FFN_STAGE_ROWS * D_FF // cols


def _stage_scratch(cols):
    return [pltpu.VMEM((FFN_STAGE_DEPTH, _stage_rows(cols), cols), F32),
            pltpu.SemaphoreType.DMA((FFN_STAGE_DEPTH,))]


def _load_weights_as_bf16(jobs, stages):
    chunks, issued = [], {}
    for src, dst in jobs:
        rows, cols = src.shape
        step = _stage_rows(cols)
        for r0 in range(0, rows, step):
            slot = issued.get(cols, 0) % FFN_STAGE_DEPTH
            issued[cols] = issued.get(cols, 0) + 1
            chunks.append((src, dst, r0, min(step, rows - r0), cols, slot))

    def copy_of(chunk):
        src, _, r0, nr, cols, slot = chunk
        stage, sem = stages[cols]
        return pltpu.make_async_copy(src.at[pl.ds(r0, nr), :], stage.at[slot, pl.ds(0, nr), :], sem.at[slot])

    ahead = FFN_STAGE_DEPTH - 1
    for chunk in chunks[:ahead]:
        copy_of(chunk).start()
    for k, chunk in enumerate(chunks):
        if k + ahead < len(chunks):
            copy_of(chunks[k + ahead]).start()
        copy_of(chunk).wait()
        _, dst, r0, nr, cols, slot = chunk
        dst[r0:r0 + nr, :] = stages[cols][0][slot, 0:nr, :].astype(BF16)


def _group_mean_sq(x, ones_ref):
    xx = x * x
    return _dot(xx.astype(BF16), ones_ref[...]) * (1.0 / NA_DH)


def _ffn1_inproj_kernel(x_ref, g1_ref, wg_hbm, wu_hbm, wd_hbm, gm_ref, wc_ref, wvg_ref, wo_ref, wna_ref,
                        bgr_ref, ones_ref, gq_ref, gk_ref,
                        x1_ref, c_ref, vt_ref, o_ref, ga_ref, gb_ref, qn_ref, kn_ref, vn_ref,
                        wg_ref, wu_ref, wd_ref, stage_ff, sem_ff, stage_dm, sem_dm):
    @pl.when(pl.program_id(0) == 0)
    def _():
        _load_weights_as_bf16([(wg_hbm, wg_ref), (wu_hbm, wu_ref), (wd_hbm, wd_ref)],
                              {D_FF: (stage_ff, sem_ff), D_MODEL: (stage_dm, sem_dm)})

    x = x_ref[...]
    h = _rms(x, g1_ref[...]).astype(BF16)
    x1 = x + _swiglu_half(h, wg_ref, wu_ref, wd_ref)
    x1_ref[...] = x1
    h2 = _rms(x1, gm_ref[...]).astype(BF16)
    W = ML_WIDTH
    c_ref[...] = _dot(h2, wc_ref[...])
    o_ref[...] = jax.nn.sigmoid(_dot(h2, wo_ref[...])).astype(BF16)
    vg = _dot_nt(wvg_ref[...], h2)
    vt = vg[:W].astype(BF16)
    for cidx in range(FFN_TM // ML_L):
        vt_ref[cidx] = vt[:, cidx * ML_L:(cidx + 1) * ML_L]
    p = _dot(h2, wna_ref[...])
    q = p[:, 0:W]
    k = p[:, W:2 * W]
    vn_ref[...] = p[:, 2 * W:3 * W].astype(BF16)
    qn = q * lax.rsqrt(_group_mean_sq(q, ones_ref) + EPS) * gq_ref[...] * (NA_DH ** -0.5 * LOG2E)
    kn = k * lax.rsqrt(_group_mean_sq(k, ones_ref) + EPS) * gk_ref[...]
    qn_ref[...] = qn.astype(BF16)
    kn_ref[...] = kn.astype(BF16)
    gr = vg[W:] + bgr_ref[...]
    row = lax.broadcasted_iota(jnp.int32, gr.shape, 0) % (4 * ML_HEADS)
    gr = jnp.where((row >= ML_HEADS) & (row < 3 * ML_HEADS), _log_sigmoid(gr), gr)
    for dd in range(2):
        for cidx in range(FFN_TM // ML_L):
            lanes = slice(cidx * ML_L, (cidx + 1) * ML_L)
            ga_ref[dd, cidx] = gr[dd * 16:dd * 16 + 8, lanes]
            gb_ref[dd, cidx] = gr[dd * 16 + 8:dd * 16 + 16, lanes]


def _const_spec(shape):
    nd = len(shape)
    return pl.BlockSpec(shape, lambda *_: (0,) * nd, pipeline_mode=pl.Buffered(1))


def _ffn1_inproj(x2d, g1, wg, wu, wd, gm, wc, wvg, wo, wna, bgr, ones, gq, gk):
    n = x2d.shape[0]
    tm = FFN_TM
    W = ML_WIDTH
    tok = lambda w: pl.BlockSpec((tm, w), lambda i: (i, 0))
    out_shape = (
        jax.ShapeDtypeStruct((n, D_MODEL), F32),
        jax.ShapeDtypeStruct((n, W), F32),
        jax.ShapeDtypeStruct((n // ML_L, W, ML_L), BF16),
        jax.ShapeDtypeStruct((n, W), BF16),
        jax.ShapeDtypeStruct((2, n // ML_L, SUBLANES, ML_L), F32),
        jax.ShapeDtypeStruct((2, n // ML_L, SUBLANES, ML_L), F32),
        jax.ShapeDtypeStruct((n, W), BF16),
        jax.ShapeDtypeStruct((n, W), BF16),
        jax.ShapeDtypeStruct((n, W), BF16),
    )
    out_specs = (
        tok(D_MODEL), tok(W), pl.BlockSpec((tm // ML_L, W, ML_L), lambda i: (i, 0, 0)), tok(W),
        pl.BlockSpec((2, tm // ML_L, SUBLANES, ML_L), lambda i: (0, i, 0, 0)),
        pl.BlockSpec((2, tm // ML_L, SUBLANES, ML_L), lambda i: (0, i, 0, 0)),
        tok(W), tok(W), tok(W),
    )
    hbm = pl.BlockSpec(memory_space=pl.ANY)
    in_specs = [tok(D_MODEL), _const_spec(g1.shape), hbm, hbm, hbm] + [
        _const_spec(a.shape) for a in (gm, wc, wvg, wo, wna, bgr, ones, gq, gk)]
    return pl.pallas_call(
        _ffn1_inproj_kernel,
        grid=(n // tm,),
        in_specs=in_specs,
        out_specs=out_specs,
        out_shape=out_shape,
        scratch_shapes=[pltpu.VMEM(a.shape, BF16) for a in (wg, wu, wd)]
        + _stage_scratch(D_FF) + _stage_scratch(D_MODEL),
        compiler_params=pltpu.CompilerParams(
            dimension_semantics=("arbitrary",), vmem_limit_bytes=VMEM_LIMIT),
        name="ffn1_inproj",
    )(x2d, g1, wg, wu, wd, gm, wc, wvg, wo, wna, bgr, ones, gq, gk)


def _conv_qk_kernel(c_ref, prev_ref, next_ref, cw_ref, cb_ref, wqt_ref, wk_ref, qt_ref, k_ref, pad_ref):
    i = pl.program_id(1)
    nb = pl.num_programs(1)
    tt = CONV_TT
    half = CONV_W // 2
    pad_ref[0:SUBLANES, :] = jnp.where(i > 0, prev_ref[...], 0.0)
    pad_ref[SUBLANES:SUBLANES + tt, :] = c_ref[...]
    pad_ref[SUBLANES + tt:2 * SUBLANES + tt, :] = jnp.where(i < nb - 1, next_ref[...], 0.0)
    y = cb_ref[...]
    for kk in range(CONV_W):
        start = SUBLANES + kk - half
        y = y + pad_ref[start:start + tt, :] * cw_ref[kk:kk + 1, :]
    u = y * jax.nn.sigmoid(y)
    for h in range(ML_HEADS):
        sl = slice(h * ML_DH, (h + 1) * ML_DH)
        uh = u[:, sl].astype(BF16)
        k_ref[:, sl] = (_dot(uh, wk_ref[h]) * (ML_DH ** -0.5)).astype(BF16)
        qt = _dot_nt(wqt_ref[h], uh).astype(BF16)
        for cidx in range(tt // ML_L):
            qt_ref[cidx, sl, :] = qt[:, cidx * ML_L:(cidx + 1) * ML_L]


def _conv_qk(c3, cw, cb, wqt, wk):
    b, t, w = c3.shape
    tt = CONV_TT
    nb = t // tt
    per = tt // SUBLANES
    last = t // SUBLANES - 1
    tile = pl.BlockSpec((None, tt, w), lambda bi, i: (bi, i, 0))
    prev = pl.BlockSpec((None, SUBLANES, w), lambda bi, i: (bi, jnp.maximum(i * per - 1, 0), 0))
    nxt = pl.BlockSpec((None, SUBLANES, w), lambda bi, i: (bi, jnp.minimum((i + 1) * per, last), 0))
    return pl.pallas_call(
        _conv_qk_kernel,
        grid=(b, nb),
        in_specs=[tile, prev, nxt] + [
            pl.BlockSpec(a.shape, functools.partial(lambda nd, bi, i: (0,) * nd, a.ndim))
            for a in (cw, cb, wqt, wk)],
        out_specs=(pl.BlockSpec((None, tt // ML_L, w, ML_L), lambda bi, i: (bi, i, 0, 0)), tile),
        out_shape=(jax.ShapeDtypeStruct((b, t // ML_L, w, ML_L), BF16),
                   jax.ShapeDtypeStruct((b, t, w), BF16)),
        scratch_shapes=[pltpu.VMEM((tt + 2 * SUBLANES, w), F32)],
        compiler_params=pltpu.CompilerParams(dimension_semantics=("parallel", "parallel")),
        name="conv_qk",
    )(c3, c3, c3, cw, cb, wqt, wk)


def _lane_scan(x, pos, d, combine, ident):
    L = x.shape[-1]
    k = 1
    while k < L:
        shifted = pltpu.roll(x, jnp.where(d == 0, k, L - k), axis=1)
        x = combine(x, jnp.where(pos >= k, shifted, ident))
        k *= 2
    return x


def _mlstm_gates_kernel(ga_ref, gb_ref, rowq_ref, colq_ref, b_s, g_s, mloc_s, rmax_s, w_s, vec_s):
    d = pl.program_id(1)
    R = SUBLANES
    L = ML_L
    nc = ga_ref.shape[0] // R
    xa = ga_ref[...]
    xb = gb_ref[...]
    lane = lax.broadcasted_iota(jnp.int32, xa.shape, 1)
    pos = lane + d * (L - 1 - 2 * lane)
    b = _lane_scan(xb, pos, d, jnp.add, 0.0)
    g = jnp.broadcast_to(jnp.sum(xb, axis=1, keepdims=True), xb.shape)
    w = xa - b
    a = g + w
    m_loc = jnp.broadcast_to(jnp.max(a, axis=1, keepdims=True), a.shape)
    rowq_ref[3] = jnp.exp(a - m_loc)
    b_s[...] = b
    g_s[...] = g
    mloc_s[...] = m_loc
    rmax_s[...] = _lane_scan(w, pos, d, jnp.maximum, NEG)
    w_s[...] = w
    vec_s[...] = jnp.zeros_like(vec_s)

    def chunk(k, m0):
        ck = k + d * (nc - 1 - 2 * k)
        rows = pl.ds(pl.multiple_of(ck * R, R), R)
        gk = g_s[rows, :]
        mlk = mloc_s[rows, :]
        mm = jnp.maximum(m0, rmax_s[rows, :])
        m_new = jnp.maximum(gk + m0, mlk)
        rowq_ref[0, rows, :] = mm * LOG2E
        rowq_ref[1, rows, :] = jnp.exp(m0 - mm)
        rowq_ref[2, rows, :] = jnp.exp(-(b_s[rows, :] + mm))
        rowq_ref[4, rows, :] = jnp.exp(gk + m0 - m_new)
        rowq_ref[5, rows, :] = jnp.exp(mlk - m_new)
        vec_s[ck, 0:R, :] = w_s[rows, :] * LOG2E
        colq_ref[ck] = vec_s[ck].T
        return m_new

    lax.fori_loop(0, nc, chunk, jnp.zeros((R, L), F32), unroll=8)


def _mlstm_gates(ga, gb, b, t):
    nc = t // ML_L
    R = SUBLANES
    in_spec = pl.BlockSpec((None, nc * R, ML_L), lambda bi, d: (d, bi, 0))
    return pl.pallas_call(
        _mlstm_gates_kernel,
        grid=(b, 2),
        in_specs=[in_spec, in_spec],
        out_specs=(pl.BlockSpec((None, None, ML_NROWQ, nc * R, ML_L), lambda bi, d: (bi, d, 0, 0, 0)),
                   pl.BlockSpec((None, None, nc, ML_L, LANES), lambda bi, d: (bi, d, 0, 0, 0))),
        out_shape=(jax.ShapeDtypeStruct((b, 2, ML_NROWQ, nc * R, ML_L), F32),
                   jax.ShapeDtypeStruct((b, 2, nc, ML_L, LANES), F32)),
        scratch_shapes=[pltpu.VMEM((nc * R, ML_L), F32)] * 5 + [pltpu.VMEM((nc, LANES, ML_L), F32)],
        compiler_params=pltpu.CompilerParams(dimension_semantics=("parallel", "parallel")),
        name="mlstm_gates",
    )(ga, gb)


def _mlstm_kernel(qt_ref, k_ref, vt_ref, o_ref, rowq_ref, colq_ref, gnb_ref, y_ref,
                  hf_ref, state_ref, sq_s, upd_s, *, nblk):
    d = pl.program_id(1)
    i = pl.program_id(2)
    blk = i + d * (nblk - 1 - 2 * i)
    L = ML_L
    nch = ML_TT // L
    H = ML_HEADS
    DH = ML_DH
    R = SUBLANES

    @pl.when(i == 0)
    def _():
        state_ref[...] = jnp.zeros_like(state_ref)

    s_idx = lax.broadcasted_iota(jnp.int32, (L, L), 0)
    t_idx = lax.broadcasted_iota(jnp.int32, (L, L), 1)
    fwd = d == 0
    mask = (t_idx - s_idx) * (1 - 2 * d) >= 0
    ones = jnp.ones((ML_NONES, L), BF16)
    NA = DH + ML_NONES
    c_total = hf_ref.shape[0] - nch
    hbase = jnp.where(fwd, blk * nch, c_total)

    def chunk_of(j):
        cj = j + d * (nch - 1 - 2 * j)
        return cj, pl.ds(pl.multiple_of(cj * R, R), R)

    for j in range(nch):
        cj, rows = chunk_of(j)
        off = pl.multiple_of(cj * L, L)
        mm = rowq_ref[0, rows, :]
        sc = rowq_ref[1, rows, :]
        w_end = rowq_ref[3, rows, :]
        cols = colq_ref[cj]
        for h in range(0, H, 2):
            ka = k_ref[pl.ds(off, L), h * DH:(h + 1) * DH]
            kb = k_ref[pl.ds(off, L), (h + 1) * DH:(h + 2) * DH]
            qta = qt_ref[cj, h * DH:(h + 1) * DH, :]
            qtb = qt_ref[cj, (h + 1) * DH:(h + 2) * DH, :]
            zero = jnp.zeros((DH, L), BF16)
            st2 = _dot(jnp.concatenate([ka, kb], axis=1),
                       jnp.concatenate([jnp.concatenate([qta, zero], axis=1),
                                        jnp.concatenate([zero, qtb], axis=1)], axis=0))
            vws = []
            for hh, qt in ((h, qta), (h + 1, qtb)):
                sl = slice(hh * DH, (hh + 1) * DH)
                lanes = slice((hh - h) * L, (hh - h + 1) * L)
                vaug = jnp.concatenate([vt_ref[cj, sl, :], ones], axis=0)
                e = jnp.exp2(jnp.where(mask, cols[:, hh:hh + 1] - mm[hh:hh + 1, :], NEG))
                sq_s[j * H + hh, 0:L, :] = (st2[:, lanes] * e).astype(BF16)
                sq_s[j * H + hh, L:L + DH, :] = qt * sc[hh:hh + 1, :].astype(BF16)
                vws.append(vaug * w_end[hh:hh + 1, :].astype(BF16))
            upd2 = _dot(jnp.concatenate(vws, axis=1),
                        jnp.concatenate([jnp.concatenate([ka, zero], axis=1),
                                         jnp.concatenate([zero, kb], axis=1)], axis=0))
            upd_s[j * H + h] = upd2[:, :DH]
            upd_s[j * H + h + 1] = upd2[:, DH:]

    for j in range(nch):
        cj, rows = chunk_of(j)
        em = rowq_ref[2, rows, :]
        s_old = rowq_ref[4, rows, :]
        s_loc = rowq_ref[5, rows, :]
        for h in range(H):
            sl = slice(h * DH, (h + 1) * DH)
            vaug = jnp.concatenate([vt_ref[cj, sl, :], ones], axis=0)
            st = state_ref[h]
            r = _dot(jnp.concatenate([vaug, st.astype(BF16)], axis=1), sq_s[j * H + h])
            inv = 1.0 / jnp.maximum(jnp.abs(r[DH:DH + 1, :]), em[h:h + 1, :])
            hf_ref[hbase + cj, sl, :] = r[:DH, :] * inv
            state_ref[h] = (jnp.broadcast_to(s_old[h:h + 1, :], (NA, DH)) * st
                            + jnp.broadcast_to(s_loc[h:h + 1, :], (NA, DH)) * upd_s[j * H + h])

    @pl.when(jnp.logical_not(fwd))
    def _():
        for c in range(nch):
            for h in range(H):
                sl = slice(h * DH, (h + 1) * DH)
                tot = hf_ref[blk * nch + c, sl, :] + hf_ref[c_total + c, sl, :]
                ms = jnp.mean(tot * tot, axis=0, keepdims=True)
                hn = (tot * lax.rsqrt(ms + EPS) * gnb_ref[sl, :]).T
                tok = slice(c * L, (c + 1) * L)
                y_ref[tok, sl] = (o_ref[tok, sl].astype(F32) * hn).astype(BF16)


def _mlstm(qt4, k3, vt4, o3, rowq, colq, gnb):
    b, t, w = k3.shape
    tt = ML_TT
    nblk = t // tt
    nch = tt // ML_L
    blk_of = lambda d, i: i + d * (nblk - 1 - 2 * i)
    tile = pl.BlockSpec((None, tt, w), lambda bi, d, i: (bi, blk_of(d, i), 0))
    ft_spec = pl.BlockSpec((None, nch, w, ML_L), lambda bi, d, i: (bi, blk_of(d, i), 0, 0))
    rq_spec = pl.BlockSpec((None, None, ML_NROWQ, nch * SUBLANES, ML_L),
                           lambda bi, d, i: (bi, d, 0, blk_of(d, i), 0))
    cq_spec = pl.BlockSpec((None, None, nch, ML_L, LANES), lambda bi, d, i: (bi, d, blk_of(d, i), 0, 0))
    gn_spec = pl.BlockSpec(gnb.shape, lambda bi, d, i: (0, 0))
    y_spec = pl.BlockSpec((None, tt, w), lambda bi, d, i: (bi, d * blk_of(d, i) + (1 - d) * (nblk - 1), 0))
    return pl.pallas_call(
        functools.partial(_mlstm_kernel, nblk=nblk),
        grid=(b, 2, nblk),
        in_specs=[ft_spec, tile, ft_spec, tile, rq_spec, cq_spec, gn_spec],
        out_specs=y_spec,
        out_shape=jax.ShapeDtypeStruct((b, t, w), BF16),
        scratch_shapes=[
            pltpu.VMEM((t // ML_L + nch, w, ML_L), F32),
            pltpu.VMEM((ML_HEADS, ML_DH + ML_NONES, ML_DH), F32),
            pltpu.VMEM((nch * ML_HEADS, ML_L + ML_DH, ML_L), BF16),
            pltpu.VMEM((nch * ML_HEADS, ML_DH + ML_NONES, ML_DH), F32),
        ],
        compiler_params=pltpu.CompilerParams(
            dimension_semantics=("arbitrary", "arbitrary", "arbitrary"), vmem_limit_bytes=VMEM_LIMIT),
        name="mlstm",
    )(qt4, k3, vt4, o3, rowq, colq, gnb)


def _na_kernel(q_ref, k_ref, v_ref, toep_ref, out_ref, bias_ref, *, rows):
    r0 = pl.program_id(1) * NA_R
    nkeys = NA_KH * GRID_W
    gw = NA_PACK * NA_DH

    @pl.when(pl.program_id(1) == 0)
    def _():
        for delta in range(NA_KH):
            lo = (NA_KH - 1 - delta) * GRID_W
            for h in range(NA_HEADS):
                g, hh = divmod(h, NA_PACK)
                bias_ref[delta, g, hh * GRID_W:(hh + 1) * GRID_W, :] = toep_ref[h, :, lo:lo + nkeys]

    row_blk = lax.broadcasted_iota(jnp.int32, (NA_PACK * GRID_W, gw), 0) // GRID_W
    lane_blk = lax.broadcasted_iota(jnp.int32, (NA_PACK * GRID_W, gw), 1) // NA_DH
    diag = row_blk == lane_blk
    out_blk = lax.broadcasted_iota(jnp.int32, (GRID_W, gw), 1) // NA_DH

    ng = NA_HEADS // NA_PACK

    def window(ri):
        r = r0 + ri
        rs = jnp.clip(r - NA_KH // 2, 0, rows - NA_KH)
        return pl.ds(pl.multiple_of(rs * GRID_W, GRID_W), nkeys), r - rs

    def row_body(ri, carry):
        keys, delta = window(ri)
        qoff = pl.multiple_of(ri * GRID_W, GRID_W)
        outs = []
        for g in range(ng):
            sl = slice(g * gw, (g + 1) * gw)
            q4 = q_ref[pl.ds(qoff, GRID_W), sl]
            qbd = jnp.where(diag, jnp.concatenate([q4] * NA_PACK, axis=0), jnp.zeros((), BF16))
            s = _dot_nt(qbd, k_ref[keys, sl]) + bias_ref[delta, g]
            p = jnp.exp2(s - jnp.max(s, axis=-1, keepdims=True))
            linv = 1.0 / jnp.sum(p, axis=-1, keepdims=True)
            o = _dot(p.astype(BF16), v_ref[keys, sl]) * linv
            og = o[(NA_PACK - 1) * GRID_W:, :]
            for h in range(NA_PACK - 2, -1, -1):
                og = jnp.where(out_blk == h, o[h * GRID_W:(h + 1) * GRID_W, :], og)
            outs.append(og)
        out_ref[pl.ds(qoff, GRID_W), :] = jnp.concatenate(outs, axis=-1).astype(BF16)
        return carry

    lax.fori_loop(0, NA_R, row_body, 0, unroll=True)


def _na(q3, k3, v3, toep):
    b, t, w = q3.shape
    rows = t // GRID_W
    tq = NA_R * GRID_W
    qtile = pl.BlockSpec((None, tq, w), lambda bi, i: (bi, i, 0))
    seq = pl.BlockSpec((None, t, w), lambda bi, i: (bi, 0, 0))
    toep_spec = pl.BlockSpec(toep.shape, lambda bi, i: (0, 0, 0), pipeline_mode=pl.Buffered(1))
    return pl.pallas_call(
        functools.partial(_na_kernel, rows=rows),
        grid=(b, rows // NA_R),
        in_specs=[qtile, seq, seq, toep_spec],
        out_specs=qtile,
        out_shape=jax.ShapeDtypeStruct((b, t, w), BF16),
        scratch_shapes=[pltpu.VMEM((NA_KH, NA_HEADS // NA_PACK, NA_PACK * GRID_W, NA_KH * GRID_W), F32)],
        compiler_params=pltpu.CompilerParams(
            dimension_semantics=("arbitrary", "arbitrary"), vmem_limit_bytes=VMEM_LIMIT),
        name="natten",
    )(q3, k3, v3, toep)


def _na_bias_table(rpb):
    c = np.arange(GRID_W)
    cs = np.clip(c - NA_KW // 2, 0, GRID_W - NA_KW)
    cc = np.arange(GRID_W)
    valid = (cc[None, :] >= cs[:, None]) & (cc[None, :] < cs[:, None] + NA_KW)
    nh, ndr, ndc = rpb.shape
    lead = GRID_W - NA_KW
    w = jnp.pad(rpb.astype(F32), ((0, 0), (0, 0), (lead, 2 * GRID_W - lead - ndc)))
    skew = jnp.broadcast_to(w[:, :, None, :], (nh, ndr, GRID_W, 2 * GRID_W)).reshape(nh, ndr, -1)
    skew = skew[:, :, :GRID_W * (2 * GRID_W - 1)].reshape(nh, ndr, GRID_W, 2 * GRID_W - 1)
    toep = jnp.where(valid[None, None], skew[..., GRID_W - 1:] * LOG2E, NEG).transpose(0, 2, 1, 3)
    return toep.reshape(nh, GRID_W, ndr * GRID_W)


def _outproj_ffn2_kernel(x1_ref, yml_ref, yna_ref, wout_hbm, g2_ref, wg_hbm, wu_hbm, wd_hbm, gf_ref, out_ref,
                         wout_ref, wg_ref, wu_ref, wd_ref, stage_ff, sem_ff, stage_dm, sem_dm):
    @pl.when(pl.program_id(0) == 0)
    def _():
        _load_weights_as_bf16(
            [(wout_hbm, wout_ref), (wg_hbm, wg_ref), (wu_hbm, wu_ref), (wd_hbm, wd_ref)],
            {D_FF: (stage_ff, sem_ff), D_MODEL: (stage_dm, sem_dm)})

    x2 = x1_ref[...] + _dot(jnp.concatenate([yml_ref[...], yna_ref[...]], axis=1), wout_ref[...])
    h = _rms(x2, g2_ref[...]).astype(BF16)
    x3 = x2 + _swiglu_half(h, wg_ref, wu_ref, wd_ref)
    out_ref[...] = _rms(x3, gf_ref[...])


def _outproj_ffn2(x1, yml, yna, wout, g2, wg, wu, wd, gf):
    n = x1.shape[0]
    tm = FFN_TM
    tok = lambda w: pl.BlockSpec((tm, w), lambda i: (i, 0))
    hbm = pl.BlockSpec(memory_space=pl.ANY)
    return pl.pallas_call(
        _outproj_ffn2_kernel,
        grid=(n // tm,),
        in_specs=[tok(D_MODEL), tok(ML_WIDTH), tok(NA_WIDTH), hbm, _const_spec(g2.shape), hbm, hbm, hbm,
                  _const_spec(gf.shape)],
        out_specs=tok(D_MODEL),
        out_shape=jax.ShapeDtypeStruct((n, D_MODEL), F32),
        scratch_shapes=[pltpu.VMEM(a.shape, BF16) for a in (wout, wg, wu, wd)]
        + _stage_scratch(D_FF) + _stage_scratch(D_MODEL),
        compiler_params=pltpu.CompilerParams(
            dimension_semantics=("arbitrary",), vmem_limit_bytes=VMEM_LIMIT),
        name="outproj_ffn2",
    )(x1, yml, yna, wout, g2, wg, wu, wd, gf)


def _layer(x, norm_ffn1, w1_gate, w1_up, w1_down, norm_mix, w_in, b_gates, conv_w, conv_b, w_q_ml,
           w_k_ml, gn_ml, gq_na, gk_na, rpb, w_out, norm_ffn2, w2_gate, w2_up, w2_down, norm_final):
    b, t, dm = x.shape
    n = b * t
    W = ML_WIDTH
    row = lambda a: a.reshape(1, -1).astype(F32)
    g0 = 3 * W
    wc = w_in[:, :W].astype(BF16)
    wo = w_in[:, 2 * W:g0].astype(BF16)
    wna = w_in[:, g0 + N_GATES:].astype(BF16)
    wgate = w_in[:, g0:g0 + N_GATES]
    H = ML_HEADS
    order = np.concatenate([dd * 2 * H + np.r_[0:2 * H, H:2 * H, 0:H] for dd in range(2)])
    wvg = jnp.concatenate([w_in[:, W:2 * W].T, wgate.T[order]], axis=0).astype(BF16)
    bgr = b_gates[order].reshape(-1, 1).astype(F32)
    ones = jnp.asarray(np.kron(np.eye(NA_HEADS), np.ones((NA_DH, NA_DH))), BF16)
    gq = jnp.tile(gq_na.astype(F32), NA_HEADS).reshape(1, -1)
    gk = jnp.tile(gk_na.astype(F32), NA_HEADS).reshape(1, -1)

    x1, c, vt_ml, o, ga, gb, qn, kn, vn = _ffn1_inproj(
        x.reshape(n, dm), row(norm_ffn1), w1_gate.astype(F32), w1_up.astype(F32), w1_down.astype(F32),
        row(norm_mix), wc, wvg, wo, wna, bgr, ones, gq, gk)

    seq = lambda a: a.reshape(b, t, a.shape[-1])
    qt_ml, k_ml = _conv_qk(seq(c), conv_w.astype(F32), row(conv_b),
                           w_q_ml.transpose(0, 2, 1).astype(BF16), w_k_ml.astype(BF16))
    rowq, colq = _mlstm_gates(ga.reshape(2, -1, ML_L), gb.reshape(2, -1, ML_L), b, t)
    gnb = jnp.broadcast_to(gn_ml.astype(F32).reshape(W, 1), (W, ML_L))
    y_ml = _mlstm(qt_ml, k_ml, vt_ml.reshape(b, t // ML_L, W, ML_L), seq(o), rowq, colq, gnb)
    y_na = _na(seq(qn), seq(kn), seq(vn), _na_bias_table(rpb))

    out = _outproj_ffn2(x1, y_ml.reshape(n, W), y_na.reshape(n, NA_WIDTH),
                        w_out.astype(F32), row(norm_ffn2),
                        w2_gate.astype(F32), w2_up.astype(F32), w2_down.astype(F32), row(norm_final))
    return out.reshape(b, t, dm)


def kernel(x, norm_ffn1, w1_gate, w1_up, w1_down, norm_mix, w_in, b_gates, conv_w, conv_b, w_q_ml, w_k_ml,
           gn_ml, gq_na, gk_na, rpb, w_out, norm_ffn2, w2_gate, w2_up, w2_down, norm_final):
    depth = norm_ffn1.shape[0]
    for l in range(depth):
        x = _layer(x, norm_ffn1[l], w1_gate[l], w1_up[l], w1_down[l], norm_mix[l], w_in[l], b_gates[l],
                   conv_w[l], conv_b[l], w_q_ml[l], w_k_ml[l], gn_ml[l], gq_na[l], gk_na[l], rpb[l],
                   w_out[l], norm_ffn2[l], w2_gate[l], w2_up[l], w2_down[l], norm_final[l])
    return x
```

```python
import functools

import jax
import jax.numpy as jnp
import numpy as np
from jax import lax
from jax.experimental import pallas as pl
from jax.experimental.pallas import tpu as pltpu

F32 = jnp.float32
BF16 = jnp.bfloat16

D_MODEL = 1024
D_FF = 2816
GRID_W = 64
ML_HEADS = 4
ML_DH = 128
ML_WIDTH = ML_HEADS * ML_DH
CONV_W = 5
NA_HEADS = 8
NA_DH = 64
NA_WIDTH = NA_HEADS * NA_DH
NA_KH = 8
NA_KW = 16
N_GATES = 4 * ML_HEADS
EPS = 1e-6
NEG = -1e30
LOG2E = 1.4426950408889634

SUBLANES = 8
LANES = 128

FFN_TM = 512
FFN_CK = 2816
FFN_STAGE_ROWS = 64
FFN_STAGE_DEPTH = 4
ML_L = 128
ML_TT = 1024
ML_NROWQ = 6
ML_NONES = 16
CONV_TT = 1024
NA_R = 16
NA_PACK = 4
VMEM_LIMIT = 56 * 1024 * 1024


def _dot(a, b):
    return jnp.dot(a, b, preferred_element_type=F32)


def _dot_nt(a, b):
    return lax.dot_general(a, b, (((1,), (1,)), ((), ())), preferred_element_type=F32)


def _dot_tn(a, b):
    return lax.dot_general(a, b, (((0,), (0,)), ((), ())), preferred_element_type=F32)


def _dot_exact(a, b):
    return jnp.dot(a, b, preferred_element_type=F32, precision=lax.Precision.HIGHEST)


def _rms(x, g):
    ms = jnp.mean(x * x, axis=-1, keepdims=True)
    return x * lax.rsqrt(ms + EPS) * g


def _log_sigmoid(x):
    return jnp.minimum(x, 0.0) - jnp.log1p(jnp.exp(-jnp.abs(x)))


def _swiglu_half(h, wg_ref, wu_ref, wd_ref):
    acc = None
    for j in range(D_FF // FFN_CK):
        sl = slice(j * FFN_CK, (j + 1) * FFN_CK)
        g = _dot(h, wg_ref[:, sl])
        u = _dot(h, wu_ref[:, sl])
        a = (g * jax.nn.sigmoid(g) * u).astype(BF16)
        part = _dot(a, wd_ref[sl, :])
        acc = part if acc is None else acc + part
    return 0.5 * acc


def _stage_rows(cols):
    return FFN_STAGE_ROWS * D_FF // cols


def _stage_scratch(cols):
    return [pltpu.VMEM((FFN_STAGE_DEPTH, _stage_rows(cols), cols), F32),
            pltpu.SemaphoreType.DMA((FFN_STAGE_DEPTH,))]


def _load_weights_as_bf16(jobs, stages):
    chunks, issued = [], {}
    for src, dst in jobs:
        rows, cols = src.shape
        step = _stage_rows(cols)
        for r0 in range(0, rows, step):
            slot = issued.get(cols, 0) % FFN_STAGE_DEPTH
            issued[cols] = issued.get(cols, 0) + 1
            chunks.append((src, dst, r0, min(step, rows - r0), cols, slot))

    def copy_of(chunk):
        src, _, r0, nr, cols, slot = chunk
        stage, sem = stages[cols]
        return pltpu.make_async_copy(src.at[pl.ds(r0, nr), :], stage.at[slot, pl.ds(0, nr), :], sem.at[slot])

    ahead = FFN_STAGE_DEPTH - 1
    for chunk in chunks[:ahead]:
        copy_of(chunk).start()
    for k, chunk in enumerate(chunks):
        if k + ahead < len(chunks):
            copy_of(chunks[k + ahead]).start()
        copy_of(chunk).wait()
        _, dst, r0, nr, cols, slot = chunk
        dst[r0:r0 + nr, :] = stages[cols][0][slot, 0:nr, :].astype(BF16)


def _group_mean_sq(x, ones_ref):
    xx = x * x
    return _dot(xx.astype(BF16), ones_ref[...]) * (1.0 / NA_DH)


def _ffn1_inproj_kernel(x_ref, g1_ref, wg_hbm, wu_hbm, wd_hbm, gm_ref, wc_ref, wvg_ref, wo_ref, wna_ref,
                        bgr_ref, ones_ref, gq_ref, gk_ref,
                        x1_ref, c_ref, vt_ref, o_ref, ga_ref, gb_ref, qn_ref, kn_ref, vn_ref,
                        wg_ref, wu_ref, wd_ref, stage_ff, sem_ff, stage_dm, sem_dm):
    @pl.when(pl.program_id(0) == 0)
    def _():
        _load_weights_as_bf16([(wg_hbm, wg_ref), (wu_hbm, wu_ref), (wd_hbm, wd_ref)],
                              {D_FF: (stage_ff, sem_ff), D_MODEL: (stage_dm, sem_dm)})

    x = x_ref[...]
    h = _rms(x, g1_ref[...]).astype(BF16)
    x1 = x + _swiglu_half(h, wg_ref, wu_ref, wd_ref)
    x1_ref[...] = x1
    h2 = _rms(x1, gm_ref[...]).astype(BF16)
    W = ML_WIDTH
    c_ref[...] = _dot(h2, wc_ref[...])
    o_ref[...] = jax.nn.sigmoid(_dot(h2, wo_ref[...])).astype(BF16)
    vg = _dot_nt(wvg_ref[...], h2)
    vt = vg[:W].astype(BF16)
    for cidx in range(FFN_TM // ML_L):
        vt_ref[cidx] = vt[:, cidx * ML_L:(cidx + 1) * ML_L]
    p = _dot(h2, wna_ref[...])
    q = p[:, 0:W]
    k = p[:, W:2 * W]
    vn_ref[...] = p[:, 2 * W:3 * W].astype(BF16)
    qn = q * lax.rsqrt(_group_mean_sq(q, ones_ref) + EPS) * gq_ref[...] * (NA_DH ** -0.5 * LOG2E)
    kn = k * lax.rsqrt(_group_mean_sq(k, ones_ref) + EPS) * gk_ref[...]
    qn_ref[...] = qn.astype(BF16)
    kn_ref[...] = kn.astype(BF16)
    gr = vg[W:] + bgr_ref[...]
    row = lax.broadcasted_iota(jnp.int32, gr.shape, 0) % (4 * ML_HEADS)
    gr = jnp.where((row >= ML_HEADS) & (row < 3 * ML_HEADS), _log_sigmoid(gr), gr)
    for dd in range(2):
        for cidx in range(FFN_TM // ML_L):
            lanes = slice(cidx * ML_L, (cidx + 1) * ML_L)
            ga_ref[dd, cidx] = gr[dd * 16:dd * 16 + 8, lanes]
            gb_ref[dd, cidx] = gr[dd * 16 + 8:dd * 16 + 16, lanes]


def _const_spec(shape):
    nd = len(shape)
    return pl.BlockSpec(shape, lambda *_: (0,) * nd, pipeline_mode=pl.Buffered(1))


def _ffn1_inproj(x2d, g1, wg, wu, wd, gm, wc, wvg, wo, wna, bgr, ones, gq, gk):
    n = x2d.shape[0]
    tm = FFN_TM
    W = ML_WIDTH
    tok = lambda w: pl.BlockSpec((tm, w), lambda i: (i, 0))
    out_shape = (
        jax.ShapeDtypeStruct((n, D_MODEL), F32),
        jax.ShapeDtypeStruct((n, W), F32),
        jax.ShapeDtypeStruct((n // ML_L, W, ML_L), BF16),
        jax.ShapeDtypeStruct((n, W), BF16),
        jax.ShapeDtypeStruct((2, n // ML_L, SUBLANES, ML_L), F32),
        jax.ShapeDtypeStruct((2, n // ML_L, SUBLANES, ML_L), F32),
        jax.ShapeDtypeStruct((n, W), BF16),
        jax.ShapeDtypeStruct((n, W), BF16),
        jax.ShapeDtypeStruct((n, W), BF16),
    )
    out_specs = (
        tok(D_MODEL), tok(W), pl.BlockSpec((tm // ML_L, W, ML_L), lambda i: (i, 0, 0)), tok(W),
        pl.BlockSpec((2, tm // ML_L, SUBLANES, ML_L), lambda i: (0, i, 0, 0)),
        pl.BlockSpec((2, tm // ML_L, SUBLANES, ML_L), lambda i: (0, i, 0, 0)),
        tok(W), tok(W), tok(W),
    )
    hbm = pl.BlockSpec(memory_space=pl.ANY)
    in_specs = [tok(D_MODEL), _const_spec(g1.shape), hbm, hbm, hbm] + [
        _const_spec(a.shape) for a in (gm, wc, wvg, wo, wna, bgr, ones, gq, gk)]
    return pl.pallas_call(
        _ffn1_inproj_kernel,
        grid=(n // tm,),
        in_specs=in_specs,
        out_specs=out_specs,
        out_shape=out_shape,
        scratch_shapes=[pltpu.VMEM(a.shape, BF16) for a in (wg, wu, wd)]
        + _stage_scratch(D_FF) + _stage_scratch(D_MODEL),
        compiler_params=pltpu.CompilerParams(
            dimension_semantics=("arbitrary",), vmem_limit_bytes=VMEM_LIMIT),
        name="ffn1_inproj",
    )(x2d, g1, wg, wu, wd, gm, wc, wvg, wo, wna, bgr, ones, gq, gk)


def _conv_qk_kernel(c_ref, prev_ref, next_ref, cw_ref, cb_ref, wqt_ref, wk_ref, qt_ref, k_ref, pad_ref):
    i = pl.program_id(1)
    nb = pl.num_programs(1)
    tt = CONV_TT
    half = CONV_W // 2
    pad_ref[0:SUBLANES, :] = jnp.where(i > 0, prev_ref[...], 0.0)
    pad_ref[SUBLANES:SUBLANES + tt, :] = c_ref[...]
    pad_ref[SUBLANES + tt:2 * SUBLANES + tt, :] = jnp.where(i < nb - 1, next_ref[...], 0.0)
    y = cb_ref[...]
    for kk in range(CONV_W):
        start = SUBLANES + kk - half
        y = y + pad_ref[start:start + tt, :] * cw_ref[kk:kk + 1, :]
    u = y * jax.nn.sigmoid(y)
    for h in range(ML_HEADS):
        sl = slice(h * ML_DH, (h + 1) * ML_DH)
        uh = u[:, sl].astype(BF16)
        k_ref[:, sl] = (_dot(uh, wk_ref[h]) * (ML_DH ** -0.5)).astype(BF16)
        qt = _dot_nt(wqt_ref[h], uh).astype(BF16)
        for cidx in range(tt // ML_L):
            qt_ref[cidx, sl, :] = qt[:, cidx * ML_L:(cidx + 1) * ML_L]


def _conv_qk(c3, cw, cb, wqt, wk):
    b, t, w = c3.shape
    tt = CONV_TT
    nb = t // tt
    per = tt // SUBLANES
    last = t // SUBLANES - 1
    tile = pl.BlockSpec((None, tt, w), lambda bi, i: (bi, i, 0))
    prev = pl.BlockSpec((None, SUBLANES, w), lambda bi, i: (bi, jnp.maximum(i * per - 1, 0), 0))
    nxt = pl.BlockSpec((None, SUBLANES, w), lambda bi, i: (bi, jnp.minimum((i + 1) * per, last), 0))
    return pl.pallas_call(
        _conv_qk_kernel,
        grid=(b, nb),
        in_specs=[tile, prev, nxt] + [
            pl.BlockSpec(a.shape, functools.partial(lambda nd, bi, i: (0,) * nd, a.ndim))
            for a in (cw, cb, wqt, wk)],
        out_specs=(pl.BlockSpec((None, tt // ML_L, w, ML_L), lambda bi, i: (bi, i, 0, 0)), tile),
        out_shape=(jax.ShapeDtypeStruct((b, t // ML_L, w, ML_L), BF16),
                   jax.ShapeDtypeStruct((b, t, w), BF16)),
        scratch_shapes=[pltpu.VMEM((tt + 2 * SUBLANES, w), F32)],
        compiler_params=pltpu.CompilerParams(dimension_semantics=("parallel", "parallel")),
        name="conv_qk",
    )(c3, c3, c3, cw, cb, wqt, wk)


def _lane_scan(x, pos, d, combine, ident):
    L = x.shape[-1]
    k = 1
    while k < L:
        shifted = pltpu.roll(x, jnp.where(d == 0, k, L - k), axis=1)
        x = combine(x, jnp.where(pos >= k, shifted, ident))
        k *= 2
    return x


def _mlstm_gates_kernel(ga_ref, gb_ref, rowq_ref, colq_ref, b_s, g_s, mloc_s, rmax_s, w_s, vec_s):
    d = pl.program_id(1)
    R = SUBLANES
    L = ML_L
    nc = ga_ref.shape[0] // R
    xa = ga_ref[...]
    xb = gb_ref[...]
    lane = lax.broadcasted_iota(jnp.int32, xa.shape, 1)
    pos = lane + d * (L - 1 - 2 * lane)
    b = _lane_scan(xb, pos, d, jnp.add, 0.0)
    g = jnp.broadcast_to(jnp.sum(xb, axis=1, keepdims=True), xb.shape)
    w = xa - b
    a = g + w
    m_loc = jnp.broadcast_to(jnp.max(a, axis=1, keepdims=True), a.shape)
    rowq_ref[3] = jnp.exp(a - m_loc)
    b_s[...] = b
    g_s[...] = g
    mloc_s[...] = m_loc
    rmax_s[...] = _lane_scan(w, pos, d, jnp.maximum, NEG)
    w_s[...] = w
    vec_s[...] = jnp.zeros_like(vec_s)

    def chunk(k, m0):
        ck = k + d * (nc - 1 - 2 * k)
        rows = pl.ds(pl.multiple_of(ck * R, R), R)
        gk = g_s[rows, :]
        mlk = mloc_s[rows, :]
        mm = jnp.maximum(m0, rmax_s[rows, :])
        m_new = jnp.maximum(gk + m0, mlk)
        rowq_ref[0, rows, :] = mm * LOG2E
        rowq_ref[1, rows, :] = jnp.exp(m0 - mm)
        rowq_ref[2, rows, :] = jnp.exp(-(b_s[rows, :] + mm))
        rowq_ref[4, rows, :] = jnp.exp(gk + m0 - m_new)
        rowq_ref[5, rows, :] = jnp.exp(mlk - m_new)
        vec_s[ck, 0:R, :] = w_s[rows, :] * LOG2E
        colq_ref[ck] = vec_s[ck].T
        return m_new

    lax.fori_loop(0, nc, chunk, jnp.zeros((R, L), F32), unroll=8)


def _mlstm_gates(ga, gb, b, t):
    nc = t // ML_L
    R = SUBLANES
    in_spec = pl.BlockSpec((None, nc * R, ML_L), lambda bi, d: (d, bi, 0))
    return pl.pallas_call(
        _mlstm_gates_kernel,
        grid=(b, 2),
        in_specs=[in_spec, in_spec],
        out_specs=(pl.BlockSpec((None, None, ML_NROWQ, nc * R, ML_L), lambda bi, d: (bi, d, 0, 0, 0)),
                   pl.BlockSpec((None, None, nc, ML_L, LANES), lambda bi, d: (bi, d, 0, 0, 0))),
        out_shape=(jax.ShapeDtypeStruct((b, 2, ML_NROWQ, nc * R, ML_L), F32),
                   jax.ShapeDtypeStruct((b, 2, nc, ML_L, LANES), F32)),
        scratch_shapes=[pltpu.VMEM((nc * R, ML_L), F32)] * 5 + [pltpu.VMEM((nc, LANES, ML_L), F32)],
        compiler_params=pltpu.CompilerParams(dimension_semantics=("parallel", "parallel")),
        name="mlstm_gates",
    )(ga, gb)


def _mlstm_kernel(qt_ref, k_ref, vt_ref, o_ref, rowq_ref, colq_ref, gnb_ref, y_ref,
                  hf_ref, state_ref, sq_s, upd_s, *, nblk):
    d = pl.program_id(1)
    i = pl.program_id(2)
    blk = i + d * (nblk - 1 - 2 * i)
    L = ML_L
    nch = ML_TT // L
    H = ML_HEADS
    DH = ML_DH
    R = SUBLANES

    @pl.when(i == 0)
    def _():
        state_ref[...] = jnp.zeros_like(state_ref)

    s_idx = lax.broadcasted_iota(jnp.int32, (L, L), 0)
    t_idx = lax.broadcasted_iota(jnp.int32, (L, L), 1)
    fwd = d == 0
    mask = (t_idx - s_idx) * (1 - 2 * d) >= 0
    ones = jnp.ones((ML_NONES, L), BF16)
    NA = DH + ML_NONES
    c_total = hf_ref.shape[0] - nch
    hbase = jnp.where(fwd, blk * nch, c_total)

    def chunk_of(j):
        cj = j + d * (nch - 1 - 2 * j)
        return cj, pl.ds(pl.multiple_of(cj * R, R), R)

    for j in range(nch):
        cj, rows = chunk_of(j)
        off = pl.multiple_of(cj * L, L)
        mm = rowq_ref[0, rows, :]
        sc = rowq_ref[1, rows, :]
        w_end = rowq_ref[3, rows, :]
        cols = colq_ref[cj]
        for h in range(0, H, 2):
            ka = k_ref[pl.ds(off, L), h * DH:(h + 1) * DH]
            kb = k_ref[pl.ds(off, L), (h + 1) * DH:(h + 2) * DH]
            qta = qt_ref[cj, h * DH:(h + 1) * DH, :]
            qtb = qt_ref[cj, (h + 1) * DH:(h + 2) * DH, :]
            zero = jnp.zeros((DH, L), BF16)
            st2 = _dot(jnp.concatenate([ka, kb], axis=1),
                       jnp.concatenate([jnp.concatenate([qta, zero], axis=1),
                                        jnp.concatenate([zero, qtb], axis=1)], axis=0))
            vws = []
            for hh, qt in ((h, qta), (h + 1, qtb)):
                sl = slice(hh * DH, (hh + 1) * DH)
                lanes = slice((hh - h) * L, (hh - h + 1) * L)
                vaug = jnp.concatenate([vt_ref[cj, sl, :], ones], axis=0)
                e = jnp.exp2(jnp.where(mask, cols[:, hh:hh + 1] - mm[hh:hh + 1, :], NEG))
                sq_s[j * H + hh, 0:L, :] = (st2[:, lanes] * e).astype(BF16)
                sq_s[j * H + hh, L:L + DH, :] = qt * sc[hh:hh + 1, :].astype(BF16)
                vws.append(vaug * w_end[hh:hh + 1, :].astype(BF16))
            upd2 = _dot(jnp.concatenate(vws, axis=1),
                        jnp.concatenate([jnp.concatenate([ka, zero], axis=1),
                                         jnp.concatenate([zero, kb], axis=1)], axis=0))
            upd_s[j * H + h] = upd2[:, :DH]
            upd_s[j * H + h + 1] = upd2[:, DH:]

    for j in range(nch):
        cj, rows = chunk_of(j)
        em = rowq_ref[2, rows, :]
        s_old = rowq_ref[4, rows, :]
        s_loc = rowq_ref[5, rows, :]
        for h in range(H):
            sl = slice(h * DH, (h + 1) * DH)
            vaug = jnp.concatenate([vt_ref[cj, sl, :], ones], axis=0)
            st = state_ref[h]
            r = _dot(jnp.concatenate([vaug, st.astype(BF16)], axis=1), sq_s[j * H + h])
            inv = 1.0 / jnp.maximum(jnp.abs(r[DH:DH + 1, :]), em[h:h + 1, :])
            hf_ref[hbase + cj, sl, :] = r[:DH, :] * inv
            state_ref[h] = (jnp.broadcast_to(s_old[h:h + 1, :], (NA, DH)) * st
                            + jnp.broadcast_to(s_loc[h:h + 1, :], (NA, DH)) * upd_s[j * H + h])

    @pl.when(jnp.logical_not(fwd))
    def _():
        for c in range(nch):
            for h in range(H):
                sl = slice(h * DH, (h + 1) * DH)
                tot = hf_ref[blk * nch + c, sl, :] + hf_ref[c_total + c, sl, :]
                ms = jnp.mean(tot * tot, axis=0, keepdims=True)
                hn = (tot * lax.rsqrt(ms + EPS) * gnb_ref[sl, :]).T
                tok = slice(c * L, (c + 1) * L)
                y_ref[tok, sl] = (o_ref[tok, sl].astype(F32) * hn).astype(BF16)


def _mlstm(qt4, k3, vt4, o3, rowq, colq, gnb):
    b, t, w = k3.shape
    tt = ML_TT
    nblk = t // tt
    nch = tt // ML_L
    blk_of = lambda d, i: i + d * (nblk - 1 - 2 * i)
    tile = pl.BlockSpec((None, tt, w), lambda bi, d, i: (bi, blk_of(d, i), 0))
    ft_spec = pl.BlockSpec((None, nch, w, ML_L), lambda bi, d, i: (bi, blk_of(d, i), 0, 0))
    rq_spec = pl.BlockSpec((None, None, ML_NROWQ, nch * SUBLANES, ML_L),
                           lambda bi, d, i: (bi, d, 0, blk_of(d, i), 0))
    cq_spec = pl.BlockSpec((None, None, nch, ML_L, LANES), lambda bi, d, i: (bi, d, blk_of(d, i), 0, 0))
    gn_spec = pl.BlockSpec(gnb.shape, lambda bi, d, i: (0, 0))
    y_spec = pl.BlockSpec((None, tt, w), lambda bi, d, i: (bi, d * blk_of(d, i) + (1 - d) * (nblk - 1), 0))
    return pl.pallas_call(
        functools.partial(_mlstm_kernel, nblk=nblk),
        grid=(b, 2, nblk),
        in_specs=[ft_spec, tile, ft_spec, tile, rq_spec, cq_spec, gn_spec],
        out_specs=y_spec,
        out_shape=jax.ShapeDtypeStruct((b, t, w), BF16),
        scratch_shapes=[
            pltpu.VMEM((t // ML_L + nch, w, ML_L), F32),
            pltpu.VMEM((ML_HEADS, ML_DH + ML_NONES, ML_DH), F32),
            pltpu.VMEM((nch * ML_HEADS, ML_L + ML_DH, ML_L), BF16),
            pltpu.VMEM((nch * ML_HEADS, ML_DH + ML_NONES, ML_DH), F32),
        ],
        compiler_params=pltpu.CompilerParams(
            dimension_semantics=("arbitrary", "arbitrary", "arbitrary"), vmem_limit_bytes=VMEM_LIMIT),
        name="mlstm",
    )(qt4, k3, vt4, o3, rowq, colq, gnb)


def _na_kernel(q_ref, k_ref, v_ref, toep_ref, out_ref, bias_ref, *, rows):
    r0 = pl.program_id(1) * NA_R
    nkeys = NA_KH * GRID_W
    gw = NA_PACK * NA_DH

    @pl.when((pl.program_id(0) == 0) & (pl.program_id(1) == 0))
    def _():
        c = lax.broadcasted_iota(jnp.int32, (GRID_W, LANES), 0)
        lane = lax.broadcasted_iota(jnp.int32, (GRID_W, LANES), 1)
        cs = jnp.clip(c - NA_KW // 2, 0, GRID_W - NA_KW)
        in_lo = (lane >= cs) & (lane < cs + NA_KW)
        in_hi = (lane >= cs + GRID_W) & (lane < cs + GRID_W + NA_KW)
        ndr = 2 * NA_KH - 1
        for h in range(NA_HEADS):
            g, hh = divmod(h, NA_PACK)
            rows_of = [jnp.broadcast_to(toep_ref[h, dr:dr + 1, :], (GRID_W, LANES)) for dr in range(ndr)]
            lo = [pltpu.roll(t, 0, axis=1, stride=1, stride_axis=0) for t in rows_of]
            hi = [pltpu.roll(t, GRID_W, axis=1, stride=1, stride_axis=0) for t in rows_of]
            for delta in range(NA_KH):
                for jj in range(NA_KH // 2):
                    dr = 2 * jj + NA_KH - 1 - delta
                    pair = jnp.where(in_lo, lo[dr], jnp.where(in_hi, hi[dr + 1], NEG))
                    bias_ref[delta, g, hh * GRID_W:(hh + 1) * GRID_W, jj * LANES:(jj + 1) * LANES] = pair

    row_blk = lax.broadcasted_iota(jnp.int32, (NA_PACK * GRID_W, gw), 0) // GRID_W
    lane_blk = lax.broadcasted_iota(jnp.int32, (NA_PACK * GRID_W, gw), 1) // NA_DH
    diag = row_blk == lane_blk
    out_blk = lax.broadcasted_iota(jnp.int32, (GRID_W, gw), 1) // NA_DH

    ng = NA_HEADS // NA_PACK

    def window(ri):
        r = r0 + ri
        rs = jnp.clip(r - NA_KH // 2, 0, rows - NA_KH)
        return pl.ds(pl.multiple_of(rs * GRID_W, GRID_W), nkeys), r - rs

    def row_body(ri, carry):
        keys, delta = window(ri)
        qoff = pl.multiple_of(ri * GRID_W, GRID_W)
        outs = []
        for g in range(ng):
            sl = slice(g * gw, (g + 1) * gw)
            q4 = q_ref[pl.ds(qoff, GRID_W), sl]
            qbd = jnp.where(diag, jnp.concatenate([q4] * NA_PACK, axis=0), jnp.zeros((), BF16))
            s = _dot_nt(qbd, k_ref[keys, sl]) + bias_ref[delta, g]
            p = jnp.exp2(s - jnp.max(s, axis=-1, keepdims=True))
            linv = 1.0 / jnp.sum(p, axis=-1, keepdims=True)
            o = _dot(p.astype(BF16), v_ref[keys, sl]) * linv
            og = o[(NA_PACK - 1) * GRID_W:, :]
            for h in range(NA_PACK - 2, -1, -1):
                og = jnp.where(out_blk == h, o[h * GRID_W:(h + 1) * GRID_W, :], og)
            outs.append(og)
        out_ref[pl.ds(qoff, GRID_W), :] = jnp.concatenate(outs, axis=-1).astype(BF16)
        return carry

    lax.fori_loop(0, NA_R, row_body, 0, unroll=True)


def _na(q3, k3, v3, toep):
    b, t, w = q3.shape
    rows = t // GRID_W
    tq = NA_R * GRID_W
    qtile = pl.BlockSpec((None, tq, w), lambda bi, i: (bi, i, 0))
    seq = pl.BlockSpec((None, t, w), lambda bi, i: (bi, 0, 0))
    toep_spec = pl.BlockSpec(toep.shape, lambda bi, i: (0, 0, 0), pipeline_mode=pl.Buffered(1))
    return pl.pallas_call(
        functools.partial(_na_kernel, rows=rows),
        grid=(b, rows // NA_R),
        in_specs=[qtile, seq, seq, toep_spec],
        out_specs=qtile,
        out_shape=jax.ShapeDtypeStruct((b, t, w), BF16),
        scratch_shapes=[pltpu.VMEM((NA_KH, NA_HEADS // NA_PACK, NA_PACK * GRID_W, NA_KH * GRID_W), F32)],
        compiler_params=pltpu.CompilerParams(
            dimension_semantics=("arbitrary", "arbitrary"), vmem_limit_bytes=VMEM_LIMIT),
        name="natten",
    )(q3, k3, v3, toep)


def _na_bias_table(rpb):
    nh, ndr, ndc = rpb.shape
    mid = NA_KW - 1
    w = rpb.astype(F32) * LOG2E
    gap = jnp.zeros((nh, ndr, LANES - ndc), F32)
    return jnp.concatenate([w[..., mid:], gap, w[..., :mid]], axis=-1)


def _outproj_ffn2_kernel(x1_ref, yml_ref, yna_ref, wout_hbm, g2_ref, wg_hbm, wu_hbm, wd_hbm, gf_ref, out_ref,
                         wout_ref, wg_ref, wu_ref, wd_ref, stage_ff, sem_ff, stage_dm, sem_dm):
    @pl.when(pl.program_id(0) == 0)
    def _():
        _load_weights_as_bf16(
            [(wout_hbm, wout_ref), (wg_hbm, wg_ref), (wu_hbm, wu_ref), (wd_hbm, wd_ref)],
            {D_FF: (stage_ff, sem_ff), D_MODEL: (stage_dm, sem_dm)})

    x2 = x1_ref[...] + _dot(jnp.concatenate([yml_ref[...], yna_ref[...]], axis=1), wout_ref[...])
    h = _rms(x2, g2_ref[...]).astype(BF16)
    x3 = x2 + _swiglu_half(h, wg_ref, wu_ref, wd_ref)
    out_ref[...] = _rms(x3, gf_ref[...])


def _outproj_ffn2(x1, yml, yna, wout, g2, wg, wu, wd, gf):
    n = x1.shape[0]
    tm = FFN_TM
    tok = lambda w: pl.BlockSpec((tm, w), lambda i: (i, 0))
    hbm = pl.BlockSpec(memory_space=pl.ANY)
    return pl.pallas_call(
        _outproj_ffn2_kernel,
        grid=(n // tm,),
        in_specs=[tok(D_MODEL), tok(ML_WIDTH), tok(NA_WIDTH), hbm, _const_spec(g2.shape), hbm, hbm, hbm,
                  _const_spec(gf.shape)],
        out_specs=tok(D_MODEL),
        out_shape=jax.ShapeDtypeStruct((n, D_MODEL), F32),
        scratch_shapes=[pltpu.VMEM(a.shape, BF16) for a in (wout, wg, wu, wd)]
        + _stage_scratch(D_FF) + _stage_scratch(D_MODEL),
        compiler_params=pltpu.CompilerParams(
            dimension_semantics=("arbitrary",), vmem_limit_bytes=VMEM_LIMIT),
        name="outproj_ffn2",
    )(x1, yml, yna, wout, g2, wg, wu, wd, gf)


def _layer(x, norm_ffn1, w1_gate, w1_up, w1_down, norm_mix, w_in, b_gates, conv_w, conv_b, w_q_ml,
           w_k_ml, gn_ml, gq_na, gk_na, rpb, w_out, norm_ffn2, w2_gate, w2_up, w2_down, norm_final):
    b, t, dm = x.shape
    n = b * t
    W = ML_WIDTH
    row = lambda a: a.reshape(1, -1).astype(F32)
    g0 = 3 * W
    wc = w_in[:, :W].astype(BF16)
    wo = w_in[:, 2 * W:g0].astype(BF16)
    wna = w_in[:, g0 + N_GATES:].astype(BF16)
    wgate = w_in[:, g0:g0 + N_GATES]
    H = ML_HEADS
    order = np.concatenate([dd * 2 * H + np.r_[0:2 * H, H:2 * H, 0:H] for dd in range(2)])
    wvg = jnp.concatenate([w_in[:, W:2 * W].T, wgate.T[order]], axis=0).astype(BF16)
    bgr = b_gates[order].reshape(-1, 1).astype(F32)
    ones = jnp.asarray(np.kron(np.eye(NA_HEADS), np.ones((NA_DH, NA_DH))), BF16)
    gq = jnp.tile(gq_na.astype(F32), NA_HEADS).reshape(1, -1)
    gk = jnp.tile(gk_na.astype(F32), NA_HEADS).reshape(1, -1)

    x1, c, vt_ml, o, ga, gb, qn, kn, vn = _ffn1_inproj(
        x.reshape(n, dm), row(norm_ffn1), w1_gate.astype(F32), w1_up.astype(F32), w1_down.astype(F32),
        row(norm_mix), wc, wvg, wo, wna, bgr, ones, gq, gk)

    seq = lambda a: a.reshape(b, t, a.shape[-1])
    qt_ml, k_ml = _conv_qk(seq(c), conv_w.astype(F32), row(conv_b),
                           w_q_ml.transpose(0, 2, 1).astype(BF16), w_k_ml.astype(BF16))
    rowq, colq = _mlstm_gates(ga.reshape(2, -1, ML_L), gb.reshape(2, -1, ML_L), b, t)
    gnb = jnp.broadcast_to(gn_ml.astype(F32).reshape(W, 1), (W, ML_L))
    y_ml = _mlstm(qt_ml, k_ml, vt_ml.reshape(b, t // ML_L, W, ML_L), seq(o), rowq, colq, gnb)
    y_na = _na(seq(qn), seq(kn), seq(vn), _na_bias_table(rpb))

    out = _outproj_ffn2(x1, y_ml.reshape(n, W), y_na.reshape(n, NA_WIDTH),
                        w_out.astype(F32), row(norm_ffn2),
                        w2_gate.astype(F32), w2_up.astype(F32), w2_down.astype(F32), row(norm_final))
    return out.reshape(b, t, dm)


def kernel(x, norm_ffn1, w1_gate, w1_up, w1_down, norm_mix, w_in, b_gates, conv_w, conv_b, w_q_ml, w_k_ml,
           gn_ml, gq_na, gk_na, rpb, w_out, norm_ffn2, w2_gate, w2_up, w2_down, norm_final):
    depth = norm_ffn1.shape[0]
    for l in range(depth):
        x = _layer(x, norm_ffn1[l], w1_gate[l], w1_up[l], w1_down[l], norm_mix[l], w_in[l], b_gates[l],
                   conv_w[l], conv_b[l], w_q_ml[l], w_k_ml[l], gn_ml[l], gq_na[l], gk_na[l], rpb[l],
                   w_out[l], norm_ffn2[l], w2_gate[l], w2_up[l], w2_down[l], norm_final[l])
    return x
```

```python
import functools

import jax
import jax.numpy as jnp
import numpy as np
from jax import lax
from jax.experimental import pallas as pl
from jax.experimental.pallas import tpu as pltpu

F32 = jnp.float32
BF16 = jnp.bfloat16

D_MODEL = 1024
D_FF = 2816
GRID_W = 64
ML_HEADS = 4
ML_DH = 128
ML_WIDTH = ML_HEADS * ML_DH
CONV_W = 5
NA_HEADS = 8
NA_DH = 64
NA_WIDTH = NA_HEADS * NA_DH
NA_KH = 8
NA_KW = 16
N_GATES = 4 * ML_HEADS
EPS = 1e-6
NEG = -1e30
LOG2E = 1.4426950408889634

SUBLANES = 8
LANES = 128

FFN_TM = 512
FFN_CK = 2816
FFN_STAGE_ROWS = 64
FFN_STAGE_DEPTH = 4
ML_L = 128
ML_TT = 1024
ML_NROWQ = 6
ML_NONES = 16
CONV_TT = 1024
NA_R = 32
NA_PACK = 4
VMEM_LIMIT = 56 * 1024 * 1024


def _dot(a, b):
    return jnp.dot(a, b, preferred_element_type=F32)


def _dot_nt(a, b):
    return lax.dot_general(a, b, (((1,), (1,)), ((), ())), preferred_element_type=F32)


def _dot_tn(a, b):
    return lax.dot_general(a, b, (((0,), (0,)), ((), ())), preferred_element_type=F32)


def _dot_exact(a, b):
    return jnp.dot(a, b, preferred_element_type=F32, precision=lax.Precision.HIGHEST)


def _rms(x, g):
    ms = jnp.mean(x * x, axis=-1, keepdims=True)
    return x * lax.rsqrt(ms + EPS) * g


def _log_sigmoid(x):
    return jnp.minimum(x, 0.0) - jnp.log1p(jnp.exp(-jnp.abs(x)))


def _swiglu_half(h, wg_ref, wu_ref, wd_ref):
    acc = None
    for j in range(D_FF // FFN_CK):
        sl = slice(j * FFN_CK, (j + 1) * FFN_CK)
        g = _dot(h, wg_ref[:, sl])
        u = _dot(h, wu_ref[:, sl])
        a = (g * jax.nn.sigmoid(g) * u).astype(BF16)
        part = _dot(a, wd_ref[sl, :])
        acc = part if acc is None else acc + part
    return 0.5 * acc


def _stage_rows(cols):
    return FFN_STAGE_ROWS * D_FF // cols


def _stage_scratch(cols):
    return [pltpu.VMEM((FFN_STAGE_DEPTH, _stage_rows(cols), cols), F32),
            pltpu.SemaphoreType.DMA((FFN_STAGE_DEPTH,))]


def _load_weights_as_bf16(jobs, stages):
    chunks, issued = [], {}
    for src, dst in jobs:
        rows, cols = src.shape
        step = _stage_rows(cols)
        for r0 in range(0, rows, step):
            slot = issued.get(cols, 0) % FFN_STAGE_DEPTH
            issued[cols] = issued.get(cols, 0) + 1
            chunks.append((src, dst, r0, min(step, rows - r0), cols, slot))

    def copy_of(chunk):
        src, _, r0, nr, cols, slot = chunk
        stage, sem = stages[cols]
        return pltpu.make_async_copy(src.at[pl.ds(r0, nr), :], stage.at[slot, pl.ds(0, nr), :], sem.at[slot])

    ahead = FFN_STAGE_DEPTH - 1
    for chunk in chunks[:ahead]:
        copy_of(chunk).start()
    for k, chunk in enumerate(chunks):
        if k + ahead < len(chunks):
            copy_of(chunks[k + ahead]).start()
        copy_of(chunk).wait()
        _, dst, r0, nr, cols, slot = chunk
        dst[r0:r0 + nr, :] = stages[cols][0][slot, 0:nr, :].astype(BF16)


def _group_mean_sq(x, ones_ref):
    xx = x * x
    return _dot(xx.astype(BF16), ones_ref[...]) * (1.0 / NA_DH)


def _ffn1_inproj_kernel(x_ref, g1_ref, wg_hbm, wu_hbm, wd_hbm, gm_ref, wc_ref, wvg_ref, wo_ref, wna_ref,
                        bgr_ref, ones_ref, gq_ref, gk_ref,
                        x1_ref, c_ref, vt_ref, o_ref, ga_ref, gb_ref, qn_ref, kn_ref, vn_ref,
                        wg_ref, wu_ref, wd_ref, stage_ff, sem_ff, stage_dm, sem_dm):
    @pl.when(pl.program_id(0) == 0)
    def _():
        _load_weights_as_bf16([(wg_hbm, wg_ref), (wu_hbm, wu_ref), (wd_hbm, wd_ref)],
                              {D_FF: (stage_ff, sem_ff), D_MODEL: (stage_dm, sem_dm)})

    x = x_ref[...]
    h = _rms(x, g1_ref[...]).astype(BF16)
    x1 = x + _swiglu_half(h, wg_ref, wu_ref, wd_ref)
    x1_ref[...] = x1
    h2 = _rms(x1, gm_ref[...]).astype(BF16)
    W = ML_WIDTH
    c_ref[...] = _dot(h2, wc_ref[...])
    o_ref[...] = jax.nn.sigmoid(_dot(h2, wo_ref[...])).astype(BF16)
    vg = _dot_nt(wvg_ref[...], h2)
    vt = vg[:W].astype(BF16)
    for cidx in range(FFN_TM // ML_L):
        vt_ref[cidx] = vt[:, cidx * ML_L:(cidx + 1) * ML_L]
    p = _dot(h2, wna_ref[...])
    q = p[:, 0:W]
    k = p[:, W:2 * W]
    vn_ref[...] = p[:, 2 * W:3 * W].astype(BF16)
    qn = q * lax.rsqrt(_group_mean_sq(q, ones_ref) + EPS) * gq_ref[...] * (NA_DH ** -0.5 * LOG2E)
    kn = k * lax.rsqrt(_group_mean_sq(k, ones_ref) + EPS) * gk_ref[...]
    qn_ref[...] = qn.astype(BF16)
    kn_ref[...] = kn.astype(BF16)
    gr = vg[W:] + bgr_ref[...]
    row = lax.broadcasted_iota(jnp.int32, gr.shape, 0) % (4 * ML_HEADS)
    gr = jnp.where((row >= ML_HEADS) & (row < 3 * ML_HEADS), _log_sigmoid(gr), gr)
    for dd in range(2):
        for cidx in range(FFN_TM // ML_L):
            lanes = slice(cidx * ML_L, (cidx + 1) * ML_L)
            ga_ref[dd, cidx] = gr[dd * 16:dd * 16 + 8, lanes]
            gb_ref[dd, cidx] = gr[dd * 16 + 8:dd * 16 + 16, lanes]


def _const_spec(shape):
    nd = len(shape)
    return pl.BlockSpec(shape, lambda *_: (0,) * nd, pipeline_mode=pl.Buffered(1))


def _ffn1_inproj(x2d, g1, wg, wu, wd, gm, wc, wvg, wo, wna, bgr, ones, gq, gk):
    n = x2d.shape[0]
    tm = FFN_TM
    W = ML_WIDTH
    tok = lambda w: pl.BlockSpec((tm, w), lambda i: (i, 0))
    out_shape = (
        jax.ShapeDtypeStruct((n, D_MODEL), F32),
        jax.ShapeDtypeStruct((n, W), F32),
        jax.ShapeDtypeStruct((n // ML_L, W, ML_L), BF16),
        jax.ShapeDtypeStruct((n, W), BF16),
        jax.ShapeDtypeStruct((2, n // ML_L, SUBLANES, ML_L), F32),
        jax.ShapeDtypeStruct((2, n // ML_L, SUBLANES, ML_L), F32),
        jax.ShapeDtypeStruct((n, W), BF16),
        jax.ShapeDtypeStruct((n, W), BF16),
        jax.ShapeDtypeStruct((n, W), BF16),
    )
    out_specs = (
        tok(D_MODEL), tok(W), pl.BlockSpec((tm // ML_L, W, ML_L), lambda i: (i, 0, 0)), tok(W),
        pl.BlockSpec((2, tm // ML_L, SUBLANES, ML_L), lambda i: (0, i, 0, 0)),
        pl.BlockSpec((2, tm // ML_L, SUBLANES, ML_L), lambda i: (0, i, 0, 0)),
        tok(W), tok(W), tok(W),
    )
    hbm = pl.BlockSpec(memory_space=pl.ANY)
    in_specs = [tok(D_MODEL), _const_spec(g1.shape), hbm, hbm, hbm] + [
        _const_spec(a.shape) for a in (gm, wc, wvg, wo, wna, bgr, ones, gq, gk)]
    return pl.pallas_call(
        _ffn1_inproj_kernel,
        grid=(n // tm,),
        in_specs=in_specs,
        out_specs=out_specs,
        out_shape=out_shape,
        scratch_shapes=[pltpu.VMEM(a.shape, BF16) for a in (wg, wu, wd)]
        + _stage_scratch(D_FF) + _stage_scratch(D_MODEL),
        compiler_params=pltpu.CompilerParams(
            dimension_semantics=("arbitrary",), vmem_limit_bytes=VMEM_LIMIT),
        name="ffn1_inproj",
    )(x2d, g1, wg, wu, wd, gm, wc, wvg, wo, wna, bgr, ones, gq, gk)


def _conv_qk_kernel(c_ref, prev_ref, next_ref, cw_ref, cb_ref, wqt_ref, wk_ref, qt_ref, k_ref, pad_ref):
    i = pl.program_id(1)
    nb = pl.num_programs(1)
    tt = CONV_TT
    half = CONV_W // 2
    pad_ref[0:SUBLANES, :] = jnp.where(i > 0, prev_ref[...], 0.0)
    pad_ref[SUBLANES:SUBLANES + tt, :] = c_ref[...]
    pad_ref[SUBLANES + tt:2 * SUBLANES + tt, :] = jnp.where(i < nb - 1, next_ref[...], 0.0)
    y = cb_ref[...]
    for kk in range(CONV_W):
        start = SUBLANES + kk - half
        y = y + pad_ref[start:start + tt, :] * cw_ref[kk:kk + 1, :]
    u = y * jax.nn.sigmoid(y)
    for h in range(ML_HEADS):
        sl = slice(h * ML_DH, (h + 1) * ML_DH)
        uh = u[:, sl].astype(BF16)
        k_ref[:, sl] = (_dot(uh, wk_ref[h]) * (ML_DH ** -0.5)).astype(BF16)
        qt = _dot_nt(wqt_ref[h], uh).astype(BF16)
        for cidx in range(tt // ML_L):
            qt_ref[cidx, sl, :] = qt[:, cidx * ML_L:(cidx + 1) * ML_L]


def _conv_qk(c3, cw, cb, wqt, wk):
    b, t, w = c3.shape
    tt = CONV_TT
    nb = t // tt
    per = tt // SUBLANES
    last = t // SUBLANES - 1
    tile = pl.BlockSpec((None, tt, w), lambda bi, i: (bi, i, 0))
    prev = pl.BlockSpec((None, SUBLANES, w), lambda bi, i: (bi, jnp.maximum(i * per - 1, 0), 0))
    nxt = pl.BlockSpec((None, SUBLANES, w), lambda bi, i: (bi, jnp.minimum((i + 1) * per, last), 0))
    return pl.pallas_call(
        _conv_qk_kernel,
        grid=(b, nb),
        in_specs=[tile, prev, nxt] + [
            pl.BlockSpec(a.shape, functools.partial(lambda nd, bi, i: (0,) * nd, a.ndim))
            for a in (cw, cb, wqt, wk)],
        out_specs=(pl.BlockSpec((None, tt // ML_L, w, ML_L), lambda bi, i: (bi, i, 0, 0)), tile),
        out_shape=(jax.ShapeDtypeStruct((b, t // ML_L, w, ML_L), BF16),
                   jax.ShapeDtypeStruct((b, t, w), BF16)),
        scratch_shapes=[pltpu.VMEM((tt + 2 * SUBLANES, w), F32)],
        compiler_params=pltpu.CompilerParams(dimension_semantics=("parallel", "parallel")),
        name="conv_qk",
    )(c3, c3, c3, cw, cb, wqt, wk)


def _lane_scan(x, pos, d, combine, ident):
    L = x.shape[-1]
    k = 1
    while k < L:
        shifted = pltpu.roll(x, jnp.where(d == 0, k, L - k), axis=1)
        x = combine(x, jnp.where(pos >= k, shifted, ident))
        k *= 2
    return x


def _mlstm_gates_kernel(ga_ref, gb_ref, rowq_ref, colq_ref, b_s, g_s, mloc_s, rmax_s, w_s, vec_s):
    d = pl.program_id(1)
    R = SUBLANES
    L = ML_L
    nc = ga_ref.shape[0] // R
    xa = ga_ref[...]
    xb = gb_ref[...]
    lane = lax.broadcasted_iota(jnp.int32, xa.shape, 1)
    pos = lane + d * (L - 1 - 2 * lane)
    b = _lane_scan(xb, pos, d, jnp.add, 0.0)
    g = jnp.broadcast_to(jnp.sum(xb, axis=1, keepdims=True), xb.shape)
    w = xa - b
    a = g + w
    m_loc = jnp.broadcast_to(jnp.max(a, axis=1, keepdims=True), a.shape)
    rowq_ref[3] = jnp.exp(a - m_loc)
    b_s[...] = b
    g_s[...] = g
    mloc_s[...] = m_loc
    rmax_s[...] = _lane_scan(w, pos, d, jnp.maximum, NEG)
    w_s[...] = w
    vec_s[...] = jnp.zeros_like(vec_s)

    def chunk(k, m0):
        ck = k + d * (nc - 1 - 2 * k)
        rows = pl.ds(pl.multiple_of(ck * R, R), R)
        gk = g_s[rows, :]
        mlk = mloc_s[rows, :]
        mm = jnp.maximum(m0, rmax_s[rows, :])
        m_new = jnp.maximum(gk + m0, mlk)
        rowq_ref[0, rows, :] = mm * LOG2E
        rowq_ref[1, rows, :] = jnp.exp(m0 - mm)
        rowq_ref[2, rows, :] = jnp.exp(-(b_s[rows, :] + mm))
        rowq_ref[4, rows, :] = jnp.exp(gk + m0 - m_new)
        rowq_ref[5, rows, :] = jnp.exp(mlk - m_new)
        vec_s[ck, 0:R, :] = w_s[rows, :] * LOG2E
        colq_ref[ck] = vec_s[ck].T
        return m_new

    lax.fori_loop(0, nc, chunk, jnp.zeros((R, L), F32), unroll=8)


def _mlstm_gates(ga, gb, b, t):
    nc = t // ML_L
    R = SUBLANES
    in_spec = pl.BlockSpec((None, nc * R, ML_L), lambda bi, d: (d, bi, 0))
    return pl.pallas_call(
        _mlstm_gates_kernel,
        grid=(b, 2),
        in_specs=[in_spec, in_spec],
        out_specs=(pl.BlockSpec((None, None, ML_NROWQ, nc * R, ML_L), lambda bi, d: (bi, d, 0, 0, 0)),
                   pl.BlockSpec((None, None, nc, ML_L, LANES), lambda bi, d: (bi, d, 0, 0, 0))),
        out_shape=(jax.ShapeDtypeStruct((b, 2, ML_NROWQ, nc * R, ML_L), F32),
                   jax.ShapeDtypeStruct((b, 2, nc, ML_L, LANES), F32)),
        scratch_shapes=[pltpu.VMEM((nc * R, ML_L), F32)] * 5 + [pltpu.VMEM((nc, LANES, ML_L), F32)],
        compiler_params=pltpu.CompilerParams(dimension_semantics=("parallel", "parallel")),
        name="mlstm_gates",
    )(ga, gb)


def _mlstm_kernel(qt_ref, k_ref, vt_ref, o_ref, rowq_ref, colq_ref, gnb_ref, y_ref,
                  hf_ref, state_ref, sq_s, upd_s, *, nblk):
    d = pl.program_id(1)
    i = pl.program_id(2)
    blk = i + d * (nblk - 1 - 2 * i)
    L = ML_L
    nch = ML_TT // L
    H = ML_HEADS
    DH = ML_DH
    R = SUBLANES

    @pl.when(i == 0)
    def _():
        state_ref[...] = jnp.zeros_like(state_ref)

    s_idx = lax.broadcasted_iota(jnp.int32, (L, L), 0)
    t_idx = lax.broadcasted_iota(jnp.int32, (L, L), 1)
    fwd = d == 0
    mask = (t_idx - s_idx) * (1 - 2 * d) >= 0
    ones = jnp.ones((ML_NONES, L), BF16)
    NA = DH + ML_NONES
    c_total = hf_ref.shape[0] - nch
    hbase = jnp.where(fwd, blk * nch, c_total)

    def chunk_of(j):
        cj = j + d * (nch - 1 - 2 * j)
        return cj, pl.ds(pl.multiple_of(cj * R, R), R)

    for j in range(nch):
        cj, rows = chunk_of(j)
        off = pl.multiple_of(cj * L, L)
        mm = rowq_ref[0, rows, :]
        sc = rowq_ref[1, rows, :]
        w_end = rowq_ref[3, rows, :]
        cols = colq_ref[cj]
        for h in range(0, H, 2):
            ka = k_ref[pl.ds(off, L), h * DH:(h + 1) * DH]
            kb = k_ref[pl.ds(off, L), (h + 1) * DH:(h + 2) * DH]
            qta = qt_ref[cj, h * DH:(h + 1) * DH, :]
            qtb = qt_ref[cj, (h + 1) * DH:(h + 2) * DH, :]
            zero = jnp.zeros((DH, L), BF16)
            st2 = _dot(jnp.concatenate([ka, kb], axis=1),
                       jnp.concatenate([jnp.concatenate([qta, zero], axis=1),
                                        jnp.concatenate([zero, qtb], axis=1)], axis=0))
            vws = []
            for hh, qt in ((h, qta), (h + 1, qtb)):
                sl = slice(hh * DH, (hh + 1) * DH)
                lanes = slice((hh - h) * L, (hh - h + 1) * L)
                vaug = jnp.concatenate([vt_ref[cj, sl, :], ones], axis=0)
                e = jnp.exp2(jnp.where(mask, cols[:, hh:hh + 1] - mm[hh:hh + 1, :], NEG))
                sq_s[j * H + hh, 0:L, :] = (st2[:, lanes] * e).astype(BF16)
                sq_s[j * H + hh, L:L + DH, :] = qt * sc[hh:hh + 1, :].astype(BF16)
                vws.append(vaug * w_end[hh:hh + 1, :].astype(BF16))
            upd2 = _dot(jnp.concatenate(vws, axis=1),
                        jnp.concatenate([jnp.concatenate([ka, zero], axis=1),
                                         jnp.concatenate([zero, kb], axis=1)], axis=0))
            upd_s[j * H + h] = upd2[:, :DH]
            upd_s[j * H + h + 1] = upd2[:, DH:]

    for j in range(nch):
        cj, rows = chunk_of(j)
        em = rowq_ref[2, rows, :]
        s_old = rowq_ref[4, rows, :]
        s_loc = rowq_ref[5, rows, :]
        for h in range(H):
            sl = slice(h * DH, (h + 1) * DH)
            vaug = jnp.concatenate([vt_ref[cj, sl, :], ones], axis=0)
            st = state_ref[h]
            r = _dot(jnp.concatenate([vaug, st.astype(BF16)], axis=1), sq_s[j * H + h])
            inv = 1.0 / jnp.maximum(jnp.abs(r[DH:DH + 1, :]), em[h:h + 1, :])
            hf_ref[hbase + cj, sl, :] = r[:DH, :] * inv
            state_ref[h] = (jnp.broadcast_to(s_old[h:h + 1, :], (NA, DH)) * st
                            + jnp.broadcast_to(s_loc[h:h + 1, :], (NA, DH)) * upd_s[j * H + h])

    @pl.when(jnp.logical_not(fwd))
    def _():
        for c in range(nch):
            for h in range(H):
                sl = slice(h * DH, (h + 1) * DH)
                tot = hf_ref[blk * nch + c, sl, :] + hf_ref[c_total + c, sl, :]
                ms = jnp.mean(tot * tot, axis=0, keepdims=True)
                hn = (tot * lax.rsqrt(ms + EPS) * gnb_ref[sl, :]).T
                tok = slice(c * L, (c + 1) * L)
                y_ref[tok, sl] = (o_ref[tok, sl].astype(F32) * hn).astype(BF16)


def _mlstm(qt4, k3, vt4, o3, rowq, colq, gnb):
    b, t, w = k3.shape
    tt = ML_TT
    nblk = t // tt
    nch = tt // ML_L
    blk_of = lambda d, i: i + d * (nblk - 1 - 2 * i)
    tile = pl.BlockSpec((None, tt, w), lambda bi, d, i: (bi, blk_of(d, i), 0))
    ft_spec = pl.BlockSpec((None, nch, w, ML_L), lambda bi, d, i: (bi, blk_of(d, i), 0, 0))
    rq_spec = pl.BlockSpec((None, None, ML_NROWQ, nch * SUBLANES, ML_L),
                           lambda bi, d, i: (bi, d, 0, blk_of(d, i), 0))
    cq_spec = pl.BlockSpec((None, None, nch, ML_L, LANES), lambda bi, d, i: (bi, d, blk_of(d, i), 0, 0))
    gn_spec = pl.BlockSpec(gnb.shape, lambda bi, d, i: (0, 0))
    y_spec = pl.BlockSpec((None, tt, w), lambda bi, d, i: (bi, d * blk_of(d, i) + (1 - d) * (nblk - 1), 0))
    return pl.pallas_call(
        functools.partial(_mlstm_kernel, nblk=nblk),
        grid=(b, 2, nblk),
        in_specs=[ft_spec, tile, ft_spec, tile, rq_spec, cq_spec, gn_spec],
        out_specs=y_spec,
        out_shape=jax.ShapeDtypeStruct((b, t, w), BF16),
        scratch_shapes=[
            pltpu.VMEM((t // ML_L + nch, w, ML_L), F32),
            pltpu.VMEM((ML_HEADS, ML_DH + ML_NONES, ML_DH), F32),
            pltpu.VMEM((nch * ML_HEADS, ML_L + ML_DH, ML_L), BF16),
            pltpu.VMEM((nch * ML_HEADS, ML_DH + ML_NONES, ML_DH), F32),
        ],
        compiler_params=pltpu.CompilerParams(
            dimension_semantics=("arbitrary", "arbitrary", "arbitrary"), vmem_limit_bytes=VMEM_LIMIT),
        name="mlstm",
    )(qt4, k3, vt4, o3, rowq, colq, gnb)


def _na_kernel(q_ref, k_ref, v_ref, toep_ref, out_ref, bias_ref, *, rows):
    r0 = pl.program_id(1) * NA_R
    nkeys = NA_KH * GRID_W
    gw = NA_PACK * NA_DH

    @pl.when((pl.program_id(0) == 0) & (pl.program_id(1) == 0))
    def _():
        c = lax.broadcasted_iota(jnp.int32, (GRID_W, LANES), 0)
        lane = lax.broadcasted_iota(jnp.int32, (GRID_W, LANES), 1)
        cs = jnp.clip(c - NA_KW // 2, 0, GRID_W - NA_KW)
        in_lo = (lane >= cs) & (lane < cs + NA_KW)
        in_hi = (lane >= cs + GRID_W) & (lane < cs + GRID_W + NA_KW)
        ndr = 2 * NA_KH - 1
        for h in range(NA_HEADS):
            g, hh = divmod(h, NA_PACK)
            rows_of = [jnp.broadcast_to(toep_ref[h, dr:dr + 1, :], (GRID_W, LANES)) for dr in range(ndr)]
            lo = [pltpu.roll(t, 0, axis=1, stride=1, stride_axis=0) for t in rows_of]
            hi = [pltpu.roll(t, GRID_W, axis=1, stride=1, stride_axis=0) for t in rows_of]
            for delta in range(NA_KH):
                for jj in range(NA_KH // 2):
                    dr = 2 * jj + NA_KH - 1 - delta
                    pair = jnp.where(in_lo, lo[dr], jnp.where(in_hi, hi[dr + 1], NEG))
                    bias_ref[delta, g, hh * GRID_W:(hh + 1) * GRID_W, jj * LANES:(jj + 1) * LANES] = pair

    row_blk = lax.broadcasted_iota(jnp.int32, (NA_PACK * GRID_W, gw), 0) // GRID_W
    lane_blk = lax.broadcasted_iota(jnp.int32, (NA_PACK * GRID_W, gw), 1) // NA_DH
    diag = row_blk == lane_blk
    out_blk = lax.broadcasted_iota(jnp.int32, (GRID_W, gw), 1) // NA_DH

    ng = NA_HEADS // NA_PACK

    def window(ri):
        r = r0 + ri
        rs = jnp.clip(r - NA_KH // 2, 0, rows - NA_KH)
        return pl.ds(pl.multiple_of(rs * GRID_W, GRID_W), nkeys), r - rs

    def row_body(ri, carry):
        keys, delta = window(ri)
        qoff = pl.multiple_of(ri * GRID_W, GRID_W)
        outs = []
        for g in range(ng):
            sl = slice(g * gw, (g + 1) * gw)
            q4 = q_ref[pl.ds(qoff, GRID_W), sl]
            qbd = jnp.where(diag, jnp.concatenate([q4] * NA_PACK, axis=0), jnp.zeros((), BF16))
            s = _dot_nt(qbd, k_ref[keys, sl]) + bias_ref[delta, g]
            p = jnp.exp2(s - jnp.max(s, axis=-1, keepdims=True))
            linv = 1.0 / jnp.sum(p, axis=-1, keepdims=True)
            o = _dot(p.astype(BF16), v_ref[keys, sl]) * linv
            og = o[(NA_PACK - 1) * GRID_W:, :]
            for h in range(NA_PACK - 2, -1, -1):
                og = jnp.where(out_blk == h, o[h * GRID_W:(h + 1) * GRID_W, :], og)
            outs.append(og)
        out_ref[pl.ds(qoff, GRID_W), :] = jnp.concatenate(outs, axis=-1).astype(BF16)
        return carry

    lax.fori_loop(0, NA_R, row_body, 0, unroll=True)


def _na(q3, k3, v3, toep):
    b, t, w = q3.shape
    rows = t // GRID_W
    tq = NA_R * GRID_W
    qtile = pl.BlockSpec((None, tq, w), lambda bi, i: (bi, i, 0))
    seq = pl.BlockSpec((None, t, w), lambda bi, i: (bi, 0, 0), pipeline_mode=pl.Buffered(1))
    toep_spec = pl.BlockSpec(toep.shape, lambda bi, i: (0, 0, 0), pipeline_mode=pl.Buffered(1))
    return pl.pallas_call(
        functools.partial(_na_kernel, rows=rows),
        grid=(b, rows // NA_R),
        in_specs=[qtile, seq, seq, toep_spec],
        out_specs=qtile,
        out_shape=jax.ShapeDtypeStruct((b, t, w), BF16),
        scratch_shapes=[pltpu.VMEM((NA_KH, NA_HEADS // NA_PACK, NA_PACK * GRID_W, NA_KH * GRID_W), F32)],
        compiler_params=pltpu.CompilerParams(
            dimension_semantics=("arbitrary", "arbitrary"), vmem_limit_bytes=VMEM_LIMIT),
        name="natten",
    )(q3, k3, v3, toep)


def _na_bias_table(rpb):
    nh, ndr, ndc = rpb.shape
    mid = NA_KW - 1
    w = rpb.astype(F32) * LOG2E
    gap = jnp.zeros((nh, ndr, LANES - ndc), F32)
    return jnp.concatenate([w[..., mid:], gap, w[..., :mid]], axis=-1)


def _outproj_ffn2_kernel(x1_ref, yml_ref, yna_ref, wout_hbm, g2_ref, wg_hbm, wu_hbm, wd_hbm, gf_ref, out_ref,
                         wout_ref, wg_ref, wu_ref, wd_ref, stage_ff, sem_ff, stage_dm, sem_dm):
    @pl.when(pl.program_id(0) == 0)
    def _():
        _load_weights_as_bf16(
            [(wout_hbm, wout_ref), (wg_hbm, wg_ref), (wu_hbm, wu_ref), (wd_hbm, wd_ref)],
            {D_FF: (stage_ff, sem_ff), D_MODEL: (stage_dm, sem_dm)})

    x2 = x1_ref[...] + _dot(jnp.concatenate([yml_ref[...], yna_ref[...]], axis=1), wout_ref[...])
    h = _rms(x2, g2_ref[...]).astype(BF16)
    x3 = x2 + _swiglu_half(h, wg_ref, wu_ref, wd_ref)
    out_ref[...] = _rms(x3, gf_ref[...])


def _outproj_ffn2(x1, yml, yna, wout, g2, wg, wu, wd, gf):
    n = x1.shape[0]
    tm = FFN_TM
    tok = lambda w: pl.BlockSpec((tm, w), lambda i: (i, 0))
    hbm = pl.BlockSpec(memory_space=pl.ANY)
    return pl.pallas_call(
        _outproj_ffn2_kernel,
        grid=(n // tm,),
        in_specs=[tok(D_MODEL), tok(ML_WIDTH), tok(NA_WIDTH), hbm, _const_spec(g2.shape), hbm, hbm, hbm,
                  _const_spec(gf.shape)],
        out_specs=tok(D_MODEL),
        out_shape=jax.ShapeDtypeStruct((n, D_MODEL), F32),
        scratch_shapes=[pltpu.VMEM(a.shape, BF16) for a in (wout, wg, wu, wd)]
        + _stage_scratch(D_FF) + _stage_scratch(D_MODEL),
        compiler_params=pltpu.CompilerParams(
            dimension_semantics=("arbitrary",), vmem_limit_bytes=VMEM_LIMIT),
        name="outproj_ffn2",
    )(x1, yml, yna, wout, g2, wg, wu, wd, gf)


def _layer(x, norm_ffn1, w1_gate, w1_up, w1_down, norm_mix, w_in, b_gates, conv_w, conv_b, w_q_ml,
           w_k_ml, gn_ml, gq_na, gk_na, rpb, w_out, norm_ffn2, w2_gate, w2_up, w2_down, norm_final):
    b, t, dm = x.shape
    n = b * t
    W = ML_WIDTH
    row = lambda a: a.reshape(1, -1).astype(F32)
    g0 = 3 * W
    wc = w_in[:, :W].astype(BF16)
    wo = w_in[:, 2 * W:g0].astype(BF16)
    wna = w_in[:, g0 + N_GATES:].astype(BF16)
    wgate = w_in[:, g0:g0 + N_GATES]
    H = ML_HEADS
    order = np.concatenate([dd * 2 * H + np.r_[0:2 * H, H:2 * H, 0:H] for dd in range(2)])
    wvg = jnp.concatenate([w_in[:, W:2 * W].T, wgate.T[order]], axis=0).astype(BF16)
    bgr = b_gates[order].reshape(-1, 1).astype(F32)
    ones = jnp.asarray(np.kron(np.eye(NA_HEADS), np.ones((NA_DH, NA_DH))), BF16)
    gq = jnp.tile(gq_na.astype(F32), NA_HEADS).reshape(1, -1)
    gk = jnp.tile(gk_na.astype(F32), NA_HEADS).reshape(1, -1)

    x1, c, vt_ml, o, ga, gb, qn, kn, vn = _ffn1_inproj(
        x.reshape(n, dm), row(norm_ffn1), w1_gate.astype(F32), w1_up.astype(F32), w1_down.astype(F32),
        row(norm_mix), wc, wvg, wo, wna, bgr, ones, gq, gk)

    seq = lambda a: a.reshape(b, t, a.shape[-1])
    qt_ml, k_ml = _conv_qk(seq(c), conv_w.astype(F32), row(conv_b),
                           w_q_ml.transpose(0, 2, 1).astype(BF16), w_k_ml.astype(BF16))
    rowq, colq = _mlstm_gates(ga.reshape(2, -1, ML_L), gb.reshape(2, -1, ML_L), b, t)
    gnb = jnp.broadcast_to(gn_ml.astype(F32).reshape(W, 1), (W, ML_L))
    y_ml = _mlstm(qt_ml, k_ml, vt_ml.reshape(b, t // ML_L, W, ML_L), seq(o), rowq, colq, gnb)
    y_na = _na(seq(qn), seq(kn), seq(vn), _na_bias_table(rpb))

    out = _outproj_ffn2(x1, y_ml.reshape(n, W), y_na.reshape(n, NA_WIDTH),
                        w_out.astype(F32), row(norm_ffn2),
                        w2_gate.astype(F32), w2_up.astype(F32), w2_down.astype(F32), row(norm_final))
    return out.reshape(b, t, dm)


def kernel(x, norm_ffn1, w1_gate, w1_up, w1_down, norm_mix, w_in, b_gates, conv_w, conv_b, w_q_ml, w_k_ml,
           gn_ml, gq_na, gk_na, rpb, w_out, norm_ffn2, w2_gate, w2_up, w2_down, norm_final):
    depth = norm_ffn1.shape[0]
    for l in range(depth):
        x = _layer(x, norm_ffn1[l], w1_gate[l], w1_up[l], w1_down[l], norm_mix[l], w_in[l], b_gates[l],
                   conv_w[l], conv_b[l], w_q_ml[l], w_k_ml[l], gn_ml[l], gq_na[l], gk_na[l], rpb[l],
                   w_out[l], norm_ffn2[l], w2_gate[l], w2_up[l], w2_down[l], norm_final[l])
    return x
```

```python
import functools

import jax
import jax.numpy as jnp
import numpy as np
from jax import lax
from jax.experimental import pallas as pl
from jax.experimental.pallas import tpu as pltpu

F32 = jnp.float32
BF16 = jnp.bfloat16

D_MODEL = 1024
D_FF = 2816
GRID_W = 64
ML_HEADS = 4
ML_DH = 128
ML_WIDTH = ML_HEADS * ML_DH
CONV_W = 5
NA_HEADS = 8
NA_DH = 64
NA_WIDTH = NA_HEADS * NA_DH
NA_KH = 8
NA_KW = 16
N_GATES = 4 * ML_HEADS
EPS = 1e-6
NEG = -1e30
LOG2E = 1.4426950408889634

SUBLANES = 8
LANES = 128

FFN_TM = 512
FFN_CK = 2816
FFN_STAGE_ROWS = 64
FFN_STAGE_DEPTH = 4
ML_L = 128
ML_TT = 2048
ML_NROWQ = 6
ML_NONES = 16
CONV_TT = 1024
NA_R = 16
NA_PACK = 4
VMEM_LIMIT = 56 * 1024 * 1024


def _dot(a, b):
    return jnp.dot(a, b, preferred_element_type=F32)


def _dot_nt(a, b):
    return lax.dot_general(a, b, (((1,), (1,)), ((), ())), preferred_element_type=F32)


def _dot_tn(a, b):
    return lax.dot_general(a, b, (((0,), (0,)), ((), ())), preferred_element_type=F32)


def _dot_exact(a, b):
    return jnp.dot(a, b, preferred_element_type=F32, precision=lax.Precision.HIGHEST)


def _rms(x, g):
    ms = jnp.mean(x * x, axis=-1, keepdims=True)
    return x * lax.rsqrt(ms + EPS) * g


def _log_sigmoid(x):
    return jnp.minimum(x, 0.0) - jnp.log1p(jnp.exp(-jnp.abs(x)))


def _swiglu_half(h, wg_ref, wu_ref, wd_ref):
    acc = None
    for j in range(D_FF // FFN_CK):
        sl = slice(j * FFN_CK, (j + 1) * FFN_CK)
        g = _dot(h, wg_ref[:, sl])
        u = _dot(h, wu_ref[:, sl])
        a = (g * jax.nn.sigmoid(g) * u).astype(BF16)
        part = _dot(a, wd_ref[sl, :])
        acc = part if acc is None else acc + part
    return 0.5 * acc


def _stage_rows(cols):
    return FFN_STAGE_ROWS * D_FF // cols


def _stage_scratch(cols):
    return [pltpu.VMEM((FFN_STAGE_DEPTH, _stage_rows(cols), cols), F32),
            pltpu.SemaphoreType.DMA((FFN_STAGE_DEPTH,))]


def _load_weights_as_bf16(jobs, stages):
    chunks, issued = [], {}
    for src, dst in jobs:
        rows, cols = src.shape
        step = _stage_rows(cols)
        for r0 in range(0, rows, step):
            slot = issued.get(cols, 0) % FFN_STAGE_DEPTH
            issued[cols] = issued.get(cols, 0) + 1
            chunks.append((src, dst, r0, min(step, rows - r0), cols, slot))

    def copy_of(chunk):
        src, _, r0, nr, cols, slot = chunk
        stage, sem = stages[cols]
        return pltpu.make_async_copy(src.at[pl.ds(r0, nr), :], stage.at[slot, pl.ds(0, nr), :], sem.at[slot])

    ahead = FFN_STAGE_DEPTH - 1
    for chunk in chunks[:ahead]:
        copy_of(chunk).start()
    for k, chunk in enumerate(chunks):
        if k + ahead < len(chunks):
            copy_of(chunks[k + ahead]).start()
        copy_of(chunk).wait()
        _, dst, r0, nr, cols, slot = chunk
        dst[r0:r0 + nr, :] = stages[cols][0][slot, 0:nr, :].astype(BF16)


def _group_mean_sq(x, ones_ref):
    xx = x * x
    return _dot(xx.astype(BF16), ones_ref[...]) * (1.0 / NA_DH)


def _ffn1_inproj_kernel(x_ref, g1_ref, wg_hbm, wu_hbm, wd_hbm, gm_ref, wc_ref, wvg_ref, wo_ref, wna_ref,
                        bgr_ref, ones_ref, gq_ref, gk_ref,
                        x1_ref, c_ref, vt_ref, o_ref, ga_ref, gb_ref, qn_ref, kn_ref, vn_ref,
                        wg_ref, wu_ref, wd_ref, stage_ff, sem_ff, stage_dm, sem_dm):
    @pl.when(pl.program_id(0) == 0)
    def _():
        _load_weights_as_bf16([(wg_hbm, wg_ref), (wu_hbm, wu_ref), (wd_hbm, wd_ref)],
                              {D_FF: (stage_ff, sem_ff), D_MODEL: (stage_dm, sem_dm)})

    x = x_ref[...]
    h = _rms(x, g1_ref[...]).astype(BF16)
    x1 = x + _swiglu_half(h, wg_ref, wu_ref, wd_ref)
    x1_ref[...] = x1
    h2 = _rms(x1, gm_ref[...]).astype(BF16)
    W = ML_WIDTH
    c_ref[...] = _dot(h2, wc_ref[...])
    o_ref[...] = jax.nn.sigmoid(_dot(h2, wo_ref[...])).astype(BF16)
    vg = _dot_nt(wvg_ref[...], h2)
    vt = vg[:W].astype(BF16)
    for cidx in range(FFN_TM // ML_L):
        vt_ref[cidx] = vt[:, cidx * ML_L:(cidx + 1) * ML_L]
    p = _dot(h2, wna_ref[...])
    q = p[:, 0:W]
    k = p[:, W:2 * W]
    vn_ref[...] = p[:, 2 * W:3 * W].astype(BF16)
    qn = q * lax.rsqrt(_group_mean_sq(q, ones_ref) + EPS) * gq_ref[...] * (NA_DH ** -0.5 * LOG2E)
    kn = k * lax.rsqrt(_group_mean_sq(k, ones_ref) + EPS) * gk_ref[...]
    qn_ref[...] = qn.astype(BF16)
    kn_ref[...] = kn.astype(BF16)
    gr = vg[W:] + bgr_ref[...]
    row = lax.broadcasted_iota(jnp.int32, gr.shape, 0) % (4 * ML_HEADS)
    gr = jnp.where((row >= ML_HEADS) & (row < 3 * ML_HEADS), _log_sigmoid(gr), gr)
    for dd in range(2):
        for cidx in range(FFN_TM // ML_L):
            lanes = slice(cidx * ML_L, (cidx + 1) * ML_L)
            ga_ref[dd, cidx] = gr[dd * 16:dd * 16 + 8, lanes]
            gb_ref[dd, cidx] = gr[dd * 16 + 8:dd * 16 + 16, lanes]


def _const_spec(shape):
    nd = len(shape)
    return pl.BlockSpec(shape, lambda *_: (0,) * nd, pipeline_mode=pl.Buffered(1))


def _ffn1_inproj(x2d, g1, wg, wu, wd, gm, wc, wvg, wo, wna, bgr, ones, gq, gk):
    n = x2d.shape[0]
    tm = FFN_TM
    W = ML_WIDTH
    tok = lambda w: pl.BlockSpec((tm, w), lambda i: (i, 0))
    out_shape = (
        jax.ShapeDtypeStruct((n, D_MODEL), F32),
        jax.ShapeDtypeStruct((n, W), F32),
        jax.ShapeDtypeStruct((n // ML_L, W, ML_L), BF16),
        jax.ShapeDtypeStruct((n, W), BF16),
        jax.ShapeDtypeStruct((2, n // ML_L, SUBLANES, ML_L), F32),
        jax.ShapeDtypeStruct((2, n // ML_L, SUBLANES, ML_L), F32),
        jax.ShapeDtypeStruct((n, W), BF16),
        jax.ShapeDtypeStruct((n, W), BF16),
        jax.ShapeDtypeStruct((n, W), BF16),
    )
    out_specs = (
        tok(D_MODEL), tok(W), pl.BlockSpec((tm // ML_L, W, ML_L), lambda i: (i, 0, 0)), tok(W),
        pl.BlockSpec((2, tm // ML_L, SUBLANES, ML_L), lambda i: (0, i, 0, 0)),
        pl.BlockSpec((2, tm // ML_L, SUBLANES, ML_L), lambda i: (0, i, 0, 0)),
        tok(W), tok(W), tok(W),
    )
    hbm = pl.BlockSpec(memory_space=pl.ANY)
    in_specs = [tok(D_MODEL), _const_spec(g1.shape), hbm, hbm, hbm] + [
        _const_spec(a.shape) for a in (gm, wc, wvg, wo, wna, bgr, ones, gq, gk)]
    return pl.pallas_call(
        _ffn1_inproj_kernel,
        grid=(n // tm,),
        in_specs=in_specs,
        out_specs=out_specs,
        out_shape=out_shape,
        scratch_shapes=[pltpu.VMEM(a.shape, BF16) for a in (wg, wu, wd)]
        + _stage_scratch(D_FF) + _stage_scratch(D_MODEL),
        compiler_params=pltpu.CompilerParams(
            dimension_semantics=("arbitrary",), vmem_limit_bytes=VMEM_LIMIT),
        name="ffn1_inproj",
    )(x2d, g1, wg, wu, wd, gm, wc, wvg, wo, wna, bgr, ones, gq, gk)


def _conv_qk_kernel(c_ref, prev_ref, next_ref, cw_ref, cb_ref, wqt_ref, wk_ref, qt_ref, k_ref, pad_ref):
    i = pl.program_id(1)
    nb = pl.num_programs(1)
    tt = CONV_TT
    half = CONV_W // 2
    pad_ref[0:SUBLANES, :] = jnp.where(i > 0, prev_ref[...], 0.0)
    pad_ref[SUBLANES:SUBLANES + tt, :] = c_ref[...]
    pad_ref[SUBLANES + tt:2 * SUBLANES + tt, :] = jnp.where(i < nb - 1, next_ref[...], 0.0)
    y = cb_ref[...]
    for kk in range(CONV_W):
        start = SUBLANES + kk - half
        y = y + pad_ref[start:start + tt, :] * cw_ref[kk:kk + 1, :]
    u = y * jax.nn.sigmoid(y)
    for h in range(ML_HEADS):
        sl = slice(h * ML_DH, (h + 1) * ML_DH)
        uh = u[:, sl].astype(BF16)
        k_ref[:, sl] = (_dot(uh, wk_ref[h]) * (ML_DH ** -0.5)).astype(BF16)
        qt = _dot_nt(wqt_ref[h], uh).astype(BF16)
        for cidx in range(tt // ML_L):
            qt_ref[cidx, sl, :] = qt[:, cidx * ML_L:(cidx + 1) * ML_L]


def _conv_qk(c3, cw, cb, wqt, wk):
    b, t, w = c3.shape
    tt = CONV_TT
    nb = t // tt
    per = tt // SUBLANES
    last = t // SUBLANES - 1
    tile = pl.BlockSpec((None, tt, w), lambda bi, i: (bi, i, 0))
    prev = pl.BlockSpec((None, SUBLANES, w), lambda bi, i: (bi, jnp.maximum(i * per - 1, 0), 0))
    nxt = pl.BlockSpec((None, SUBLANES, w), lambda bi, i: (bi, jnp.minimum((i + 1) * per, last), 0))
    return pl.pallas_call(
        _conv_qk_kernel,
        grid=(b, nb),
        in_specs=[tile, prev, nxt] + [
            pl.BlockSpec(a.shape, functools.partial(lambda nd, bi, i: (0,) * nd, a.ndim))
            for a in (cw, cb, wqt, wk)],
        out_specs=(pl.BlockSpec((None, tt // ML_L, w, ML_L), lambda bi, i: (bi, i, 0, 0)), tile),
        out_shape=(jax.ShapeDtypeStruct((b, t // ML_L, w, ML_L), BF16),
                   jax.ShapeDtypeStruct((b, t, w), BF16)),
        scratch_shapes=[pltpu.VMEM((tt + 2 * SUBLANES, w), F32)],
        compiler_params=pltpu.CompilerParams(dimension_semantics=("parallel", "parallel")),
        name="conv_qk",
    )(c3, c3, c3, cw, cb, wqt, wk)


def _lane_scan(x, pos, d, combine, ident):
    L = x.shape[-1]
    k = 1
    while k < L:
        shifted = pltpu.roll(x, jnp.where(d == 0, k, L - k), axis=1)
        x = combine(x, jnp.where(pos >= k, shifted, ident))
        k *= 2
    return x


def _mlstm_gates_kernel(ga_ref, gb_ref, rowq_ref, colq_ref, b_s, g_s, mloc_s, rmax_s, w_s, vec_s):
    d = pl.program_id(1)
    R = SUBLANES
    L = ML_L
    nc = ga_ref.shape[0] // R
    xa = ga_ref[...]
    xb = gb_ref[...]
    lane = lax.broadcasted_iota(jnp.int32, xa.shape, 1)
    pos = lane + d * (L - 1 - 2 * lane)
    b = _lane_scan(xb, pos, d, jnp.add, 0.0)
    g = jnp.broadcast_to(jnp.sum(xb, axis=1, keepdims=True), xb.shape)
    w = xa - b
    a = g + w
    m_loc = jnp.broadcast_to(jnp.max(a, axis=1, keepdims=True), a.shape)
    rowq_ref[3] = jnp.exp(a - m_loc)
    b_s[...] = b
    g_s[...] = g
    mloc_s[...] = m_loc
    rmax_s[...] = _lane_scan(w, pos, d, jnp.maximum, NEG)
    w_s[...] = w
    vec_s[...] = jnp.zeros_like(vec_s)

    def chunk(k, m0):
        ck = k + d * (nc - 1 - 2 * k)
        rows = pl.ds(pl.multiple_of(ck * R, R), R)
        gk = g_s[rows, :]
        mlk = mloc_s[rows, :]
        mm = jnp.maximum(m0, rmax_s[rows, :])
        m_new = jnp.maximum(gk + m0, mlk)
        rowq_ref[0, rows, :] = mm * LOG2E
        rowq_ref[1, rows, :] = jnp.exp(m0 - mm)
        rowq_ref[2, rows, :] = jnp.exp(-(b_s[rows, :] + mm))
        rowq_ref[4, rows, :] = jnp.exp(gk + m0 - m_new)
        rowq_ref[5, rows, :] = jnp.exp(mlk - m_new)
        vec_s[ck, 0:R, :] = w_s[rows, :] * LOG2E
        colq_ref[ck] = vec_s[ck].T
        return m_new

    lax.fori_loop(0, nc, chunk, jnp.zeros((R, L), F32), unroll=8)


def _mlstm_gates(ga, gb, b, t):
    nc = t // ML_L
    R = SUBLANES
    in_spec = pl.BlockSpec((None, nc * R, ML_L), lambda bi, d: (d, bi, 0))
    return pl.pallas_call(
        _mlstm_gates_kernel,
        grid=(b, 2),
        in_specs=[in_spec, in_spec],
        out_specs=(pl.BlockSpec((None, None, ML_NROWQ, nc * R, ML_L), lambda bi, d: (bi, d, 0, 0, 0)),
                   pl.BlockSpec((None, None, nc, ML_L, LANES), lambda bi, d: (bi, d, 0, 0, 0))),
        out_shape=(jax.ShapeDtypeStruct((b, 2, ML_NROWQ, nc * R, ML_L), F32),
                   jax.ShapeDtypeStruct((b, 2, nc, ML_L, LANES), F32)),
        scratch_shapes=[pltpu.VMEM((nc * R, ML_L), F32)] * 5 + [pltpu.VMEM((nc, LANES, ML_L), F32)],
        compiler_params=pltpu.CompilerParams(dimension_semantics=("parallel", "parallel")),
        name="mlstm_gates",
    )(ga, gb)


def _mlstm_kernel(qt_ref, k_ref, vt_ref, o_ref, rowq_ref, colq_ref, gnb_ref, y_ref,
                  hf_ref, state_ref, sq_s, upd_s, *, nblk):
    d = pl.program_id(1)
    i = pl.program_id(2)
    blk = i + d * (nblk - 1 - 2 * i)
    L = ML_L
    nch = ML_TT // L
    H = ML_HEADS
    DH = ML_DH
    R = SUBLANES

    @pl.when(i == 0)
    def _():
        state_ref[...] = jnp.zeros_like(state_ref)

    s_idx = lax.broadcasted_iota(jnp.int32, (L, L), 0)
    t_idx = lax.broadcasted_iota(jnp.int32, (L, L), 1)
    fwd = d == 0
    mask = (t_idx - s_idx) * (1 - 2 * d) >= 0
    ones = jnp.ones((ML_NONES, L), BF16)
    NA = DH + ML_NONES
    c_total = hf_ref.shape[0] - nch
    hbase = jnp.where(fwd, blk * nch, c_total)

    def chunk_of(j):
        cj = j + d * (nch - 1 - 2 * j)
        return cj, pl.ds(pl.multiple_of(cj * R, R), R)

    for j in range(nch):
        cj, rows = chunk_of(j)
        off = pl.multiple_of(cj * L, L)
        mm = rowq_ref[0, rows, :]
        sc = rowq_ref[1, rows, :]
        w_end = rowq_ref[3, rows, :]
        cols = colq_ref[cj]
        for h in range(0, H, 2):
            ka = k_ref[pl.ds(off, L), h * DH:(h + 1) * DH]
            kb = k_ref[pl.ds(off, L), (h + 1) * DH:(h + 2) * DH]
            qta = qt_ref[cj, h * DH:(h + 1) * DH, :]
            qtb = qt_ref[cj, (h + 1) * DH:(h + 2) * DH, :]
            zero = jnp.zeros((DH, L), BF16)
            st2 = _dot(jnp.concatenate([ka, kb], axis=1),
                       jnp.concatenate([jnp.concatenate([qta, zero], axis=1),
                                        jnp.concatenate([zero, qtb], axis=1)], axis=0))
            vws = []
            for hh, qt in ((h, qta), (h + 1, qtb)):
                sl = slice(hh * DH, (hh + 1) * DH)
                lanes = slice((hh - h) * L, (hh - h + 1) * L)
                vaug = jnp.concatenate([vt_ref[cj, sl, :], ones], axis=0)
                e = jnp.exp2(jnp.where(mask, cols[:, hh:hh + 1] - mm[hh:hh + 1, :], NEG))
                sq_s[j * H + hh, 0:L, :] = (st2[:, lanes] * e).astype(BF16)
                sq_s[j * H + hh, L:L + DH, :] = qt * sc[hh:hh + 1, :].astype(BF16)
                vws.append(vaug * w_end[hh:hh + 1, :].astype(BF16))
            upd2 = _dot(jnp.concatenate(vws, axis=1),
                        jnp.concatenate([jnp.concatenate([ka, zero], axis=1),
                                         jnp.concatenate([zero, kb], axis=1)], axis=0))
            upd_s[j * H + h] = upd2[:, :DH]
            upd_s[j * H + h + 1] = upd2[:, DH:]

    for j in range(nch):
        cj, rows = chunk_of(j)
        em = rowq_ref[2, rows, :]
        s_old = rowq_ref[4, rows, :]
        s_loc = rowq_ref[5, rows, :]
        for h in range(H):
            sl = slice(h * DH, (h + 1) * DH)
            vaug = jnp.concatenate([vt_ref[cj, sl, :], ones], axis=0)
            st = state_ref[h]
            r = _dot(jnp.concatenate([vaug, st.astype(BF16)], axis=1), sq_s[j * H + h])
            inv = 1.0 / jnp.maximum(jnp.abs(r[DH:DH + 1, :]), em[h:h + 1, :])
            hf_ref[hbase + cj, sl, :] = r[:DH, :] * inv
            state_ref[h] = (jnp.broadcast_to(s_old[h:h + 1, :], (NA, DH)) * st
                            + jnp.broadcast_to(s_loc[h:h + 1, :], (NA, DH)) * upd_s[j * H + h])

    @pl.when(jnp.logical_not(fwd))
    def _():
        for c in range(nch):
            for h in range(H):
                sl = slice(h * DH, (h + 1) * DH)
                tot = hf_ref[blk * nch + c, sl, :] + hf_ref[c_total + c, sl, :]
                ms = jnp.mean(tot * tot, axis=0, keepdims=True)
                hn = (tot * lax.rsqrt(ms + EPS) * gnb_ref[sl, :]).T
                tok = slice(c * L, (c + 1) * L)
                y_ref[tok, sl] = (o_ref[tok, sl].astype(F32) * hn).astype(BF16)


def _mlstm(qt4, k3, vt4, o3, rowq, colq, gnb):
    b, t, w = k3.shape
    tt = ML_TT
    nblk = t // tt
    nch = tt // ML_L
    blk_of = lambda d, i: i + d * (nblk - 1 - 2 * i)
    tile = pl.BlockSpec((None, tt, w), lambda bi, d, i: (bi, blk_of(d, i), 0))
    ft_spec = pl.BlockSpec((None, nch, w, ML_L), lambda bi, d, i: (bi, blk_of(d, i), 0, 0))
    rq_spec = pl.BlockSpec((None, None, ML_NROWQ, nch * SUBLANES, ML_L),
                           lambda bi, d, i: (bi, d, 0, blk_of(d, i), 0))
    cq_spec = pl.BlockSpec((None, None, nch, ML_L, LANES), lambda bi, d, i: (bi, d, blk_of(d, i), 0, 0))
    gn_spec = pl.BlockSpec(gnb.shape, lambda bi, d, i: (0, 0))
    y_spec = pl.BlockSpec((None, tt, w), lambda bi, d, i: (bi, d * blk_of(d, i) + (1 - d) * (nblk - 1), 0))
    return pl.pallas_call(
        functools.partial(_mlstm_kernel, nblk=nblk),
        grid=(b, 2, nblk),
        in_specs=[ft_spec, tile, ft_spec, tile, rq_spec, cq_spec, gn_spec],
        out_specs=y_spec,
        out_shape=jax.ShapeDtypeStruct((b, t, w), BF16),
        scratch_shapes=[
            pltpu.VMEM((t // ML_L + nch, w, ML_L), F32),
            pltpu.VMEM((ML_HEADS, ML_DH + ML_NONES, ML_DH), F32),
            pltpu.VMEM((nch * ML_HEADS, ML_L + ML_DH, ML_L), BF16),
            pltpu.VMEM((nch * ML_HEADS, ML_DH + ML_NONES, ML_DH), F32),
        ],
        compiler_params=pltpu.CompilerParams(
            dimension_semantics=("arbitrary", "arbitrary", "arbitrary"), vmem_limit_bytes=VMEM_LIMIT),
        name="mlstm",
    )(qt4, k3, vt4, o3, rowq, colq, gnb)


def _na_kernel(q_ref, k_ref, v_ref, toep_ref, out_ref, bias_ref, *, rows):
    r0 = pl.program_id(1) * NA_R
    nkeys = NA_KH * GRID_W
    gw = NA_PACK * NA_DH

    @pl.when((pl.program_id(0) == 0) & (pl.program_id(1) == 0))
    def _():
        c = lax.broadcasted_iota(jnp.int32, (GRID_W, LANES), 0)
        lane = lax.broadcasted_iota(jnp.int32, (GRID_W, LANES), 1)
        cs = jnp.clip(c - NA_KW // 2, 0, GRID_W - NA_KW)
        in_lo = (lane >= cs) & (lane < cs + NA_KW)
        in_hi = (lane >= cs + GRID_W) & (lane < cs + GRID_W + NA_KW)
        ndr = 2 * NA_KH - 1
        for h in range(NA_HEADS):
            g, hh = divmod(h, NA_PACK)
            rows_of = [jnp.broadcast_to(toep_ref[h, dr:dr + 1, :], (GRID_W, LANES)) for dr in range(ndr)]
            lo = [pltpu.roll(t, 0, axis=1, stride=1, stride_axis=0) for t in rows_of]
            hi = [pltpu.roll(t, GRID_W, axis=1, stride=1, stride_axis=0) for t in rows_of]
            for delta in range(NA_KH):
                for jj in range(NA_KH // 2):
                    dr = 2 * jj + NA_KH - 1 - delta
                    pair = jnp.where(in_lo, lo[dr], jnp.where(in_hi, hi[dr + 1], NEG))
                    bias_ref[delta, g, hh * GRID_W:(hh + 1) * GRID_W, jj * LANES:(jj + 1) * LANES] = pair

    row_blk = lax.broadcasted_iota(jnp.int32, (NA_PACK * GRID_W, gw), 0) // GRID_W
    lane_blk = lax.broadcasted_iota(jnp.int32, (NA_PACK * GRID_W, gw), 1) // NA_DH
    diag = row_blk == lane_blk
    out_blk = lax.broadcasted_iota(jnp.int32, (GRID_W, gw), 1) // NA_DH

    ng = NA_HEADS // NA_PACK

    def window(ri):
        r = r0 + ri
        rs = jnp.clip(r - NA_KH // 2, 0, rows - NA_KH)
        return pl.ds(pl.multiple_of(rs * GRID_W, GRID_W), nkeys), r - rs

    def row_body(ri, carry):
        keys, delta = window(ri)
        qoff = pl.multiple_of(ri * GRID_W, GRID_W)
        outs = []
        for g in range(ng):
            sl = slice(g * gw, (g + 1) * gw)
            q4 = q_ref[pl.ds(qoff, GRID_W), sl]
            qbd = jnp.where(diag, jnp.concatenate([q4] * NA_PACK, axis=0), jnp.zeros((), BF16))
            s = _dot_nt(qbd, k_ref[keys, sl]) + bias_ref[delta, g]
            p = jnp.exp2(s - jnp.max(s, axis=-1, keepdims=True))
            linv = 1.0 / jnp.sum(p, axis=-1, keepdims=True)
            o = _dot(p.astype(BF16), v_ref[keys, sl]) * linv
            og = o[(NA_PACK - 1) * GRID_W:, :]
            for h in range(NA_PACK - 2, -1, -1):
                og = jnp.where(out_blk == h, o[h * GRID_W:(h + 1) * GRID_W, :], og)
            outs.append(og)
        out_ref[pl.ds(qoff, GRID_W), :] = jnp.concatenate(outs, axis=-1).astype(BF16)
        return carry

    lax.fori_loop(0, NA_R, row_body, 0, unroll=True)


def _na(q3, k3, v3, toep):
    b, t, w = q3.shape
    rows = t // GRID_W
    tq = NA_R * GRID_W
    qtile = pl.BlockSpec((None, tq, w), lambda bi, i: (bi, i, 0))
    seq = pl.BlockSpec((None, t, w), lambda bi, i: (bi, 0, 0))
    toep_spec = pl.BlockSpec(toep.shape, lambda bi, i: (0, 0, 0), pipeline_mode=pl.Buffered(1))
    return pl.pallas_call(
        functools.partial(_na_kernel, rows=rows),
        grid=(b, rows // NA_R),
        in_specs=[qtile, seq, seq, toep_spec],
        out_specs=qtile,
        out_shape=jax.ShapeDtypeStruct((b, t, w), BF16),
        scratch_shapes=[pltpu.VMEM((NA_KH, NA_HEADS // NA_PACK, NA_PACK * GRID_W, NA_KH * GRID_W), F32)],
        compiler_params=pltpu.CompilerParams(
            dimension_semantics=("arbitrary", "arbitrary"), vmem_limit_bytes=VMEM_LIMIT),
        name="natten",
    )(q3, k3, v3, toep)


def _na_bias_table(rpb):
    nh, ndr, ndc = rpb.shape
    mid = NA_KW - 1
    w = rpb.astype(F32) * LOG2E
    gap = jnp.zeros((nh, ndr, LANES - ndc), F32)
    return jnp.concatenate([w[..., mid:], gap, w[..., :mid]], axis=-1)


def _outproj_ffn2_kernel(x1_ref, yml_ref, yna_ref, wout_hbm, g2_ref, wg_hbm, wu_hbm, wd_hbm, gf_ref, out_ref,
                         wout_ref, wg_ref, wu_ref, wd_ref, stage_ff, sem_ff, stage_dm, sem_dm):
    @pl.when(pl.program_id(0) == 0)
    def _():
        _load_weights_as_bf16(
            [(wout_hbm, wout_ref), (wg_hbm, wg_ref), (wu_hbm, wu_ref), (wd_hbm, wd_ref)],
            {D_FF: (stage_ff, sem_ff), D_MODEL: (stage_dm, sem_dm)})

    x2 = x1_ref[...] + _dot(jnp.concatenate([yml_ref[...], yna_ref[...]], axis=1), wout_ref[...])
    h = _rms(x2, g2_ref[...]).astype(BF16)
    x3 = x2 + _swiglu_half(h, wg_ref, wu_ref, wd_ref)
    out_ref[...] = _rms(x3, gf_ref[...])


def _outproj_ffn2(x1, yml, yna, wout, g2, wg, wu, wd, gf):
    n = x1.shape[0]
    tm = FFN_TM
    tok = lambda w: pl.BlockSpec((tm, w), lambda i: (i, 0))
    hbm = pl.BlockSpec(memory_space=pl.ANY)
    return pl.pallas_call(
        _outproj_ffn2_kernel,
        grid=(n // tm,),
        in_specs=[tok(D_MODEL), tok(ML_WIDTH), tok(NA_WIDTH), hbm, _const_spec(g2.shape), hbm, hbm, hbm,
                  _const_spec(gf.shape)],
        out_specs=tok(D_MODEL),
        out_shape=jax.ShapeDtypeStruct((n, D_MODEL), F32),
        scratch_shapes=[pltpu.VMEM(a.shape, BF16) for a in (wout, wg, wu, wd)]
        + _stage_scratch(D_FF) + _stage_scratch(D_MODEL),
        compiler_params=pltpu.CompilerParams(
            dimension_semantics=("arbitrary",), vmem_limit_bytes=VMEM_LIMIT),
        name="outproj_ffn2",
    )(x1, yml, yna, wout, g2, wg, wu, wd, gf)


def _layer(x, norm_ffn1, w1_gate, w1_up, w1_down, norm_mix, w_in, b_gates, conv_w, conv_b, w_q_ml,
           w_k_ml, gn_ml, gq_na, gk_na, rpb, w_out, norm_ffn2, w2_gate, w2_up, w2_down, norm_final):
    b, t, dm = x.shape
    n = b * t
    W = ML_WIDTH
    row = lambda a: a.reshape(1, -1).astype(F32)
    g0 = 3 * W
    wc = w_in[:, :W].astype(BF16)
    wo = w_in[:, 2 * W:g0].astype(BF16)
    wna = w_in[:, g0 + N_GATES:].astype(BF16)
    wgate = w_in[:, g0:g0 + N_GATES]
    H = ML_HEADS
    order = np.concatenate([dd * 2 * H + np.r_[0:2 * H, H:2 * H, 0:H] for dd in range(2)])
    wvg = jnp.concatenate([w_in[:, W:2 * W].T, wgate.T[order]], axis=0).astype(BF16)
    bgr = b_gates[order].reshape(-1, 1).astype(F32)
    ones = jnp.asarray(np.kron(np.eye(NA_HEADS), np.ones((NA_DH, NA_DH))), BF16)
    gq = jnp.tile(gq_na.astype(F32), NA_HEADS).reshape(1, -1)
    gk = jnp.tile(gk_na.astype(F32), NA_HEADS).reshape(1, -1)

    x1, c, vt_ml, o, ga, gb, qn, kn, vn = _ffn1_inproj(
        x.reshape(n, dm), row(norm_ffn1), w1_gate.astype(F32), w1_up.astype(F32), w1_down.astype(F32),
        row(norm_mix), wc, wvg, wo, wna, bgr, ones, gq, gk)

    seq = lambda a: a.reshape(b, t, a.shape[-1])
    qt_ml, k_ml = _conv_qk(seq(c), conv_w.astype(F32), row(conv_b),
                           w_q_ml.transpose(0, 2, 1).astype(BF16), w_k_ml.astype(BF16))
    rowq, colq = _mlstm_gates(ga.reshape(2, -1, ML_L), gb.reshape(2, -1, ML_L), b, t)
    gnb = jnp.broadcast_to(gn_ml.astype(F32).reshape(W, 1), (W, ML_L))
    y_ml = _mlstm(qt_ml, k_ml, vt_ml.reshape(b, t // ML_L, W, ML_L), seq(o), rowq, colq, gnb)
    y_na = _na(seq(qn), seq(kn), seq(vn), _na_bias_table(rpb))

    out = _outproj_ffn2(x1, y_ml.reshape(n, W), y_na.reshape(n, NA_WIDTH),
                        w_out.astype(F32), row(norm_ffn2),
                        w2_gate.astype(F32), w2_up.astype(F32), w2_down.astype(F32), row(norm_final))
    return out.reshape(b, t, dm)


def kernel(x, norm_ffn1, w1_gate, w1_up, w1_down, norm_mix, w_in, b_gates, conv_w, conv_b, w_q_ml, w_k_ml,
           gn_ml, gq_na, gk_na, rpb, w_out, norm_ffn2, w2_gate, w2_up, w2_down, norm_final):
    depth = norm_ffn1.shape[0]
    for l in range(depth):
        x = _layer(x, norm_ffn1[l], w1_gate[l], w1_up[l], w1_down[l], norm_mix[l], w_in[l], b_gates[l],
                   conv_w[l], conv_b[l], w_q_ml[l], w_k_ml[l], gn_ml[l], gq_na[l], gk_na[l], rpb[l],
                   w_out[l], norm_ffn2[l], w2_gate[l], w2_up[l], w2_down[l], norm_final[l])
    return x
```

```python
import functools

import jax
import jax.numpy as jnp
import numpy as np
from jax import lax
from jax.experimental import pallas as pl
from jax.experimental.pallas import tpu as pltpu

F32 = jnp.float32
BF16 = jnp.bfloat16

D_MODEL = 1024
D_FF = 2816
GRID_W = 64
ML_HEADS = 4
ML_DH = 128
ML_WIDTH = ML_HEADS * ML_DH
CONV_W = 5
NA_HEADS = 8
NA_DH = 64
NA_WIDTH = NA_HEADS * NA_DH
NA_KH = 8
NA_KW = 16
N_GATES = 4 * ML_HEADS
EPS = 1e-6
NEG = -1e30
LOG2E = 1.4426950408889634

SUBLANES = 8
LANES = 128

FFN_TM = 512
FFN_CK = 2816
FFN_STAGE_ROWS = 64
FFN_STAGE_DEPTH = 4
ML_L = 128
ML_TT = 2048
ML_NROWQ = 6
ML_NONES = 16
CONV_TT = 1024
NA_R = 16
NA_PACK = 4
VMEM_LIMIT = 56 * 1024 * 1024


def _dot(a, b):
    return jnp.dot(a, b, preferred_element_type=F32)


def _dot_nt(a, b):
    return lax.dot_general(a, b, (((1,), (1,)), ((), ())), preferred_element_type=F32)


def _dot_tn(a, b):
    return lax.dot_general(a, b, (((0,), (0,)), ((), ())), preferred_element_type=F32)


def _dot_exact(a, b):
    return jnp.dot(a, b, preferred_element_type=F32, precision=lax.Precision.HIGHEST)


def _rms(x, g):
    ms = jnp.mean(x * x, axis=-1, keepdims=True)
    return x * lax.rsqrt(ms + EPS) * g


def _log_sigmoid(x):
    return jnp.minimum(x, 0.0) - jnp.log1p(jnp.exp(-jnp.abs(x)))


def _swiglu_half(h, wg_ref, wu_ref, wd_ref):
    acc = None
    for j in range(D_FF // FFN_CK):
        sl = slice(j * FFN_CK, (j + 1) * FFN_CK)
        g = _dot(h, wg_ref[:, sl])
        u = _dot(h, wu_ref[:, sl])
        a = (g * jax.nn.sigmoid(g) * u).astype(BF16)
        part = _dot(a, wd_ref[sl, :])
        acc = part if acc is None else acc + part
    return 0.5 * acc


def _stage_rows(cols):
    return FFN_STAGE_ROWS * D_FF // cols


def _stage_scratch(cols):
    return [pltpu.VMEM((FFN_STAGE_DEPTH, _stage_rows(cols), cols), F32),
            pltpu.SemaphoreType.DMA((FFN_STAGE_DEPTH,))]


def _load_weights_as_bf16(jobs, stages):
    chunks, issued = [], {}
    for src, dst, *window in jobs:
        rows, cols = src.shape
        c0, nc, skip = window[0] if window else (0, cols, 0)
        width = cols if not window else D_FF
        step = _stage_rows(width)
        for r0 in range(0, rows, step):
            slot = issued.get(width, 0) % FFN_STAGE_DEPTH
            issued[width] = issued.get(width, 0) + 1
            chunks.append((src, dst, r0, min(step, rows - r0), width, slot, c0, nc, skip))

    def copy_of(chunk):
        src, _, r0, nr, width, slot, c0, nc, _ = chunk
        stage, sem = stages[width]
        return pltpu.make_async_copy(src.at[pl.ds(r0, nr), pl.ds(c0, nc)],
                                     stage.at[slot, pl.ds(0, nr), pl.ds(0, nc)], sem.at[slot])

    ahead = FFN_STAGE_DEPTH - 1
    for chunk in chunks[:ahead]:
        copy_of(chunk).start()
    for k, chunk in enumerate(chunks):
        if k + ahead < len(chunks):
            copy_of(chunks[k + ahead]).start()
        copy_of(chunk).wait()
        _, dst, r0, nr, width, slot, _, nc, skip = chunk
        dst[r0:r0 + nr, :] = stages[width][0][slot, 0:nr, skip:nc].astype(BF16)


def _group_mean_sq(x, ones_ref):
    xx = x * x
    return _dot(xx.astype(BF16), ones_ref[...]) * (1.0 / NA_DH)


def _ffn1_inproj_kernel(x_ref, g1_ref, wg_hbm, wu_hbm, wd_hbm, gm_ref, win_hbm, wvg_ref, wna_ref,
                        bgr_ref, ones_ref, gq_ref, gk_ref,
                        x1_ref, c_ref, vt_ref, o_ref, ga_ref, gb_ref, qn_ref, kn_ref, vn_ref,
                        wg_ref, wu_ref, wd_ref, wc_ref, wo_ref, stage_ff, sem_ff, stage_dm, sem_dm):
    @pl.when(pl.program_id(0) == 0)
    def _():
        W = ML_WIDTH
        _load_weights_as_bf16(
            [(wg_hbm, wg_ref), (wu_hbm, wu_ref), (wd_hbm, wd_ref),
             (win_hbm, wc_ref, (0, W, 0)), (win_hbm, wo_ref, (2 * W, W, 0))],
            {D_FF: (stage_ff, sem_ff), D_MODEL: (stage_dm, sem_dm)})

    x = x_ref[...]
    h = _rms(x, g1_ref[...]).astype(BF16)
    x1 = x + _swiglu_half(h, wg_ref, wu_ref, wd_ref)
    x1_ref[...] = x1
    h2 = _rms(x1, gm_ref[...]).astype(BF16)
    W = ML_WIDTH
    c_ref[...] = _dot(h2, wc_ref[...])
    o_ref[...] = jax.nn.sigmoid(_dot(h2, wo_ref[...])).astype(BF16)
    vg = _dot_nt(wvg_ref[...], h2)
    vt = vg[:W].astype(BF16)
    for cidx in range(FFN_TM // ML_L):
        vt_ref[cidx] = vt[:, cidx * ML_L:(cidx + 1) * ML_L]
    p = _dot(h2, wna_ref[...])
    q = p[:, 0:W]
    k = p[:, W:2 * W]
    vn_ref[...] = p[:, 2 * W:3 * W].astype(BF16)
    qn = q * lax.rsqrt(_group_mean_sq(q, ones_ref) + EPS) * gq_ref[...] * (NA_DH ** -0.5 * LOG2E)
    kn = k * lax.rsqrt(_group_mean_sq(k, ones_ref) + EPS) * gk_ref[...]
    qn_ref[...] = qn.astype(BF16)
    kn_ref[...] = kn.astype(BF16)
    gr = vg[W:] + bgr_ref[...]
    row = lax.broadcasted_iota(jnp.int32, gr.shape, 0) % (4 * ML_HEADS)
    gr = jnp.where((row >= ML_HEADS) & (row < 3 * ML_HEADS), _log_sigmoid(gr), gr)
    for dd in range(2):
        for cidx in range(FFN_TM // ML_L):
            lanes = slice(cidx * ML_L, (cidx + 1) * ML_L)
            ga_ref[dd, cidx] = gr[dd * 16:dd * 16 + 8, lanes]
            gb_ref[dd, cidx] = gr[dd * 16 + 8:dd * 16 + 16, lanes]


def _const_spec(shape):
    nd = len(shape)
    return pl.BlockSpec(shape, lambda *_: (0,) * nd, pipeline_mode=pl.Buffered(1))


def _ffn1_inproj(x2d, g1, wg, wu, wd, gm, w_in, wvg, wna, bgr, ones, gq, gk):
    n = x2d.shape[0]
    tm = FFN_TM
    W = ML_WIDTH
    tok = lambda w: pl.BlockSpec((tm, w), lambda i: (i, 0))
    out_shape = (
        jax.ShapeDtypeStruct((n, D_MODEL), F32),
        jax.ShapeDtypeStruct((n, W), F32),
        jax.ShapeDtypeStruct((n // ML_L, W, ML_L), BF16),
        jax.ShapeDtypeStruct((n, W), BF16),
        jax.ShapeDtypeStruct((2, n // ML_L, SUBLANES, ML_L), F32),
        jax.ShapeDtypeStruct((2, n // ML_L, SUBLANES, ML_L), F32),
        jax.ShapeDtypeStruct((n, W), BF16),
        jax.ShapeDtypeStruct((n, W), BF16),
        jax.ShapeDtypeStruct((n, W), BF16),
    )
    out_specs = (
        tok(D_MODEL), tok(W), pl.BlockSpec((tm // ML_L, W, ML_L), lambda i: (i, 0, 0)), tok(W),
        pl.BlockSpec((2, tm // ML_L, SUBLANES, ML_L), lambda i: (0, i, 0, 0)),
        pl.BlockSpec((2, tm // ML_L, SUBLANES, ML_L), lambda i: (0, i, 0, 0)),
        tok(W), tok(W), tok(W),
    )
    hbm = pl.BlockSpec(memory_space=pl.ANY)
    in_specs = [tok(D_MODEL), _const_spec(g1.shape), hbm, hbm, hbm, _const_spec(gm.shape), hbm] + [
        _const_spec(a.shape) for a in (wvg, wna, bgr, ones, gq, gk)]
    dm = w_in.shape[0]
    return pl.pallas_call(
        _ffn1_inproj_kernel,
        grid=(n // tm,),
        in_specs=in_specs,
        out_specs=out_specs,
        out_shape=out_shape,
        scratch_shapes=[pltpu.VMEM(a.shape, BF16) for a in (wg, wu, wd)]
        + [pltpu.VMEM((dm, W), BF16), pltpu.VMEM((dm, W), BF16)]
        + _stage_scratch(D_FF) + _stage_scratch(D_MODEL),
        compiler_params=pltpu.CompilerParams(
            dimension_semantics=("arbitrary",), vmem_limit_bytes=VMEM_LIMIT),
        name="ffn1_inproj",
    )(x2d, g1, wg, wu, wd, gm, w_in, wvg, wna, bgr, ones, gq, gk)


def _conv_qk_kernel(c_ref, prev_ref, next_ref, cw_ref, cb_ref, wqt_ref, wk_ref, qt_ref, k_ref, pad_ref):
    i = pl.program_id(1)
    nb = pl.num_programs(1)
    tt = CONV_TT
    half = CONV_W // 2
    pad_ref[0:SUBLANES, :] = jnp.where(i > 0, prev_ref[...], 0.0)
    pad_ref[SUBLANES:SUBLANES + tt, :] = c_ref[...]
    pad_ref[SUBLANES + tt:2 * SUBLANES + tt, :] = jnp.where(i < nb - 1, next_ref[...], 0.0)
    y = cb_ref[...]
    for kk in range(CONV_W):
        start = SUBLANES + kk - half
        y = y + pad_ref[start:start + tt, :] * cw_ref[kk:kk + 1, :]
    u = y * jax.nn.sigmoid(y)
    for h in range(ML_HEADS):
        sl = slice(h * ML_DH, (h + 1) * ML_DH)
        uh = u[:, sl].astype(BF16)
        k_ref[:, sl] = (_dot(uh, wk_ref[h]) * (ML_DH ** -0.5)).astype(BF16)
        qt = _dot_nt(wqt_ref[h], uh).astype(BF16)
        for cidx in range(tt // ML_L):
            qt_ref[cidx, sl, :] = qt[:, cidx * ML_L:(cidx + 1) * ML_L]


def _conv_qk(c3, cw, cb, wqt, wk):
    b, t, w = c3.shape
    tt = CONV_TT
    nb = t // tt
    per = tt // SUBLANES
    last = t // SUBLANES - 1
    tile = pl.BlockSpec((None, tt, w), lambda bi, i: (bi, i, 0))
    prev = pl.BlockSpec((None, SUBLANES, w), lambda bi, i: (bi, jnp.maximum(i * per - 1, 0), 0))
    nxt = pl.BlockSpec((None, SUBLANES, w), lambda bi, i: (bi, jnp.minimum((i + 1) * per, last), 0))
    return pl.pallas_call(
        _conv_qk_kernel,
        grid=(b, nb),
        in_specs=[tile, prev, nxt] + [
            pl.BlockSpec(a.shape, functools.partial(lambda nd, bi, i: (0,) * nd, a.ndim))
            for a in (cw, cb, wqt, wk)],
        out_specs=(pl.BlockSpec((None, tt // ML_L, w, ML_L), lambda bi, i: (bi, i, 0, 0)), tile),
        out_shape=(jax.ShapeDtypeStruct((b, t // ML_L, w, ML_L), BF16),
                   jax.ShapeDtypeStruct((b, t, w), BF16)),
        scratch_shapes=[pltpu.VMEM((tt + 2 * SUBLANES, w), F32)],
        compiler_params=pltpu.CompilerParams(dimension_semantics=("parallel", "parallel")),
        name="conv_qk",
    )(c3, c3, c3, cw, cb, wqt, wk)


def _lane_scan(x, pos, d, combine, ident):
    L = x.shape[-1]
    k = 1
    while k < L:
        shifted = pltpu.roll(x, jnp.where(d == 0, k, L - k), axis=1)
        x = combine(x, jnp.where(pos >= k, shifted, ident))
        k *= 2
    return x


def _mlstm_gates_kernel(ga_ref, gb_ref, rowq_ref, colq_ref, b_s, g_s, mloc_s, rmax_s, w_s, pack_s):
    d = pl.program_id(1)
    R = SUBLANES
    L = ML_L
    nc = ga_ref.shape[0] // R
    xa = ga_ref[...]
    xb = gb_ref[...]
    lane = lax.broadcasted_iota(jnp.int32, xa.shape, 1)
    pos = lane + d * (L - 1 - 2 * lane)
    b = _lane_scan(xb, pos, d, jnp.add, 0.0)
    g = jnp.broadcast_to(jnp.sum(xb, axis=1, keepdims=True), xb.shape)
    w = xa - b
    a = g + w
    m_loc = jnp.broadcast_to(jnp.max(a, axis=1, keepdims=True), a.shape)
    rowq_ref[3] = jnp.exp(a - m_loc)
    b_s[...] = b
    g_s[...] = g
    mloc_s[...] = m_loc
    rmax_s[...] = _lane_scan(w, pos, d, jnp.maximum, NEG)
    w_s[...] = w * LOG2E

    P = ML_TT // L
    for blk in range(nc // P):
        for j in range(P):
            cj = blk * P + j + d * (P - 1 - 2 * j)
            pack_s[blk, j * R:(j + 1) * R, :] = w_s[pl.ds(pl.multiple_of(cj * R, R), R), :]
        colq_ref[blk] = pack_s[blk].T

    def chunk(k, m0):
        ck = k + d * (nc - 1 - 2 * k)
        rows = pl.ds(pl.multiple_of(ck * R, R), R)
        gk = g_s[rows, :]
        mlk = mloc_s[rows, :]
        mm = jnp.maximum(m0, rmax_s[rows, :])
        m_new = jnp.maximum(gk + m0, mlk)
        rowq_ref[0, rows, :] = mm * LOG2E
        rowq_ref[1, rows, :] = jnp.exp(m0 - mm)
        rowq_ref[2, rows, :] = jnp.exp(-(b_s[rows, :] + mm))
        rowq_ref[4, rows, :] = jnp.exp(gk + m0 - m_new)
        rowq_ref[5, rows, :] = jnp.exp(mlk - m_new)
        return m_new

    lax.fori_loop(0, nc, chunk, jnp.zeros((R, L), F32), unroll=8)


def _mlstm_gates(ga, gb, b, t):
    nc = t // ML_L
    R = SUBLANES
    nstep = t // ML_TT
    assert ML_TT // ML_L * R == LANES
    in_spec = pl.BlockSpec((None, nc * R, ML_L), lambda bi, d: (d, bi, 0))
    return pl.pallas_call(
        _mlstm_gates_kernel,
        grid=(b, 2),
        in_specs=[in_spec, in_spec],
        out_specs=(pl.BlockSpec((None, None, ML_NROWQ, nc * R, ML_L), lambda bi, d: (bi, d, 0, 0, 0)),
                   pl.BlockSpec((None, None, nstep, ML_L, LANES), lambda bi, d: (bi, d, 0, 0, 0))),
        out_shape=(jax.ShapeDtypeStruct((b, 2, ML_NROWQ, nc * R, ML_L), F32),
                   jax.ShapeDtypeStruct((b, 2, nstep, ML_L, LANES), F32)),
        scratch_shapes=[pltpu.VMEM((nc * R, ML_L), F32)] * 5 + [pltpu.VMEM((nstep, LANES, ML_L), F32)],
        compiler_params=pltpu.CompilerParams(dimension_semantics=("parallel", "parallel")),
        name="mlstm_gates",
    )(ga, gb)


def _mlstm_kernel(qt_ref, k_ref, vt_ref, o_ref, rowq_ref, colq_ref, gnb_ref, y_ref,
                  hf_ref, state_ref, sq_s, upd_s, *, nblk):
    d = pl.program_id(1)
    i = pl.program_id(2)
    blk = i + d * (nblk - 1 - 2 * i)
    L = ML_L
    nch = ML_TT // L
    H = ML_HEADS
    DH = ML_DH
    R = SUBLANES

    @pl.when(i == 0)
    def _():
        state_ref[...] = jnp.zeros_like(state_ref)

    s_idx = lax.broadcasted_iota(jnp.int32, (L, L), 0)
    t_idx = lax.broadcasted_iota(jnp.int32, (L, L), 1)
    fwd = d == 0
    mask = (t_idx - s_idx) * (1 - 2 * d) >= 0
    ones = jnp.ones((ML_NONES, L), BF16)
    NA = DH + ML_NONES
    c_total = hf_ref.shape[0] - nch
    hbase = jnp.where(fwd, blk * nch, c_total)

    def chunk_of(j):
        cj = j + d * (nch - 1 - 2 * j)
        return cj, pl.ds(pl.multiple_of(cj * R, R), R)

    for j in range(nch):
        cj, rows = chunk_of(j)
        off = pl.multiple_of(cj * L, L)
        mm = rowq_ref[0, rows, :]
        sc = rowq_ref[1, rows, :]
        w_end = rowq_ref[3, rows, :]
        cols = colq_ref[:, j * R:(j + 1) * R]
        for h in range(0, H, 2):
            ka = k_ref[pl.ds(off, L), h * DH:(h + 1) * DH]
            kb = k_ref[pl.ds(off, L), (h + 1) * DH:(h + 2) * DH]
            qta = qt_ref[cj, h * DH:(h + 1) * DH, :]
            qtb = qt_ref[cj, (h + 1) * DH:(h + 2) * DH, :]
            zero = jnp.zeros((DH, L), BF16)
            st2 = _dot(jnp.concatenate([ka, kb], axis=1),
                       jnp.concatenate([jnp.concatenate([qta, zero], axis=1),
                                        jnp.concatenate([zero, qtb], axis=1)], axis=0))
            vws = []
            for hh, qt in ((h, qta), (h + 1, qtb)):
                sl = slice(hh * DH, (hh + 1) * DH)
                lanes = slice((hh - h) * L, (hh - h + 1) * L)
                vaug = jnp.concatenate([vt_ref[cj, sl, :], ones], axis=0)
                e = jnp.exp2(jnp.where(mask, cols[:, hh:hh + 1] - mm[hh:hh + 1, :], NEG))
                sq_s[j * H + hh, 0:L, :] = (st2[:, lanes] * e).astype(BF16)
                sq_s[j * H + hh, L:L + DH, :] = qt * sc[hh:hh + 1, :].astype(BF16)
                vws.append(vaug * w_end[hh:hh + 1, :].astype(BF16))
            upd2 = _dot(jnp.concatenate(vws, axis=1),
                        jnp.concatenate([jnp.concatenate([ka, zero], axis=1),
                                         jnp.concatenate([zero, kb], axis=1)], axis=0))
            upd_s[j * H + h] = upd2[:, :DH]
            upd_s[j * H + h + 1] = upd2[:, DH:]

    for j in range(nch):
        cj, rows = chunk_of(j)
        em = rowq_ref[2, rows, :]
        s_old = rowq_ref[4, rows, :]
        s_loc = rowq_ref[5, rows, :]
        for h in range(H):
            sl = slice(h * DH, (h + 1) * DH)
            vaug = jnp.concatenate([vt_ref[cj, sl, :], ones], axis=0)
            st = state_ref[h]
            r = _dot(jnp.concatenate([vaug, st.astype(BF16)], axis=1), sq_s[j * H + h])
            inv = 1.0 / jnp.maximum(jnp.abs(r[DH:DH + 1, :]), em[h:h + 1, :])
            hf_ref[hbase + cj, sl, :] = r[:DH, :] * inv
            state_ref[h] = (jnp.broadcast_to(s_old[h:h + 1, :], (NA, DH)) * st
                            + jnp.broadcast_to(s_loc[h:h + 1, :], (NA, DH)) * upd_s[j * H + h])

    @pl.when(jnp.logical_not(fwd))
    def _():
        for c in range(nch):
            for h in range(H):
                sl = slice(h * DH, (h + 1) * DH)
                tot = hf_ref[blk * nch + c, sl, :] + hf_ref[c_total + c, sl, :]
                ms = jnp.mean(tot * tot, axis=0, keepdims=True)
                hn = (tot * lax.rsqrt(ms + EPS) * gnb_ref[sl, :]).T
                tok = slice(c * L, (c + 1) * L)
                y_ref[tok, sl] = (o_ref[tok, sl].astype(F32) * hn).astype(BF16)


def _mlstm(qt4, k3, vt4, o3, rowq, colq, gnb):
    b, t, w = k3.shape
    tt = ML_TT
    nblk = t // tt
    nch = tt // ML_L
    blk_of = lambda d, i: i + d * (nblk - 1 - 2 * i)
    tile = pl.BlockSpec((None, tt, w), lambda bi, d, i: (bi, blk_of(d, i), 0))
    ft_spec = pl.BlockSpec((None, nch, w, ML_L), lambda bi, d, i: (bi, blk_of(d, i), 0, 0))
    rq_spec = pl.BlockSpec((None, None, ML_NROWQ, nch * SUBLANES, ML_L),
                           lambda bi, d, i: (bi, d, 0, blk_of(d, i), 0))
    cq_spec = pl.BlockSpec((None, None, None, ML_L, LANES), lambda bi, d, i: (bi, d, blk_of(d, i), 0, 0))
    o_spec = pl.BlockSpec((None, tt, w), lambda bi, d, i: (bi, d * blk_of(d, i) + (1 - d) * (nblk - 1), 0))
    gn_spec = pl.BlockSpec(gnb.shape, lambda bi, d, i: (0, 0))
    y_spec = pl.BlockSpec((None, tt, w), lambda bi, d, i: (bi, d * blk_of(d, i) + (1 - d) * (nblk - 1), 0))
    return pl.pallas_call(
        functools.partial(_mlstm_kernel, nblk=nblk),
        grid=(b, 2, nblk),
        in_specs=[ft_spec, tile, ft_spec, o_spec, rq_spec, cq_spec, gn_spec],
        out_specs=y_spec,
        out_shape=jax.ShapeDtypeStruct((b, t, w), BF16),
        scratch_shapes=[
            pltpu.VMEM((t // ML_L + nch, w, ML_L), F32),
            pltpu.VMEM((ML_HEADS, ML_DH + ML_NONES, ML_DH), F32),
            pltpu.VMEM((nch * ML_HEADS, ML_L + ML_DH, ML_L), BF16),
            pltpu.VMEM((nch * ML_HEADS, ML_DH + ML_NONES, ML_DH), F32),
        ],
        compiler_params=pltpu.CompilerParams(
            dimension_semantics=("arbitrary", "arbitrary", "arbitrary"), vmem_limit_bytes=VMEM_LIMIT),
        name="mlstm",
    )(qt4, k3, vt4, o3, rowq, colq, gnb)


def _na_kernel(q_ref, k_ref, v_ref, toep_ref, out_ref, bias_ref, *, rows):
    r0 = pl.program_id(1) * NA_R
    nkeys = NA_KH * GRID_W
    gw = NA_PACK * NA_DH

    @pl.when((pl.program_id(0) == 0) & (pl.program_id(1) == 0))
    def _():
        c = lax.broadcasted_iota(jnp.int32, (GRID_W, LANES), 0)
        lane = lax.broadcasted_iota(jnp.int32, (GRID_W, LANES), 1)
        cs = jnp.clip(c - NA_KW // 2, 0, GRID_W - NA_KW)
        in_lo = (lane >= cs) & (lane < cs + NA_KW)
        in_hi = (lane >= cs + GRID_W) & (lane < cs + GRID_W + NA_KW)
        ndr = 2 * NA_KH - 1
        for h in range(NA_HEADS):
            g, hh = divmod(h, NA_PACK)
            rows_of = [jnp.broadcast_to(toep_ref[h, dr:dr + 1, :], (GRID_W, LANES)) for dr in range(ndr)]
            lo = [pltpu.roll(t, 0, axis=1, stride=1, stride_axis=0) for t in rows_of]
            hi = [pltpu.roll(t, GRID_W, axis=1, stride=1, stride_axis=0) for t in rows_of]
            for delta in range(NA_KH):
                for jj in range(NA_KH // 2):
                    dr = 2 * jj + NA_KH - 1 - delta
                    pair = jnp.where(in_lo, lo[dr], jnp.where(in_hi, hi[dr + 1], NEG))
                    bias_ref[delta, g, hh * GRID_W:(hh + 1) * GRID_W, jj * LANES:(jj + 1) * LANES] = pair

    row_blk = lax.broadcasted_iota(jnp.int32, (NA_PACK * GRID_W, gw), 0) // GRID_W
    lane_blk = lax.broadcasted_iota(jnp.int32, (NA_PACK * GRID_W, gw), 1) // NA_DH
    diag = row_blk == lane_blk
    out_blk = lax.broadcasted_iota(jnp.int32, (GRID_W, gw), 1) // NA_DH

    ng = NA_HEADS // NA_PACK

    def window(ri):
        r = r0 + ri
        rs = jnp.clip(r - NA_KH // 2, 0, rows - NA_KH)
        return pl.ds(pl.multiple_of(rs * GRID_W, GRID_W), nkeys), r - rs

    def row_body(ri, carry):
        keys, delta = window(ri)
        qoff = pl.multiple_of(ri * GRID_W, GRID_W)
        outs = []
        for g in range(ng):
            sl = slice(g * gw, (g + 1) * gw)
            q4 = q_ref[pl.ds(qoff, GRID_W), sl]
            qbd = jnp.where(diag, jnp.concatenate([q4] * NA_PACK, axis=0), jnp.zeros((), BF16))
            s = _dot_nt(qbd, k_ref[keys, sl]) + bias_ref[delta, g]
            p = jnp.exp2(s - jnp.max(s, axis=-1, keepdims=True))
            linv = 1.0 / jnp.sum(p, axis=-1, keepdims=True)
            o = _dot(p.astype(BF16), v_ref[keys, sl]) * linv
            og = o[(NA_PACK - 1) * GRID_W:, :]
            for h in range(NA_PACK - 2, -1, -1):
                og = jnp.where(out_blk == h, o[h * GRID_W:(h + 1) * GRID_W, :], og)
            outs.append(og)
        out_ref[pl.ds(qoff, GRID_W), :] = jnp.concatenate(outs, axis=-1).astype(BF16)
        return carry

    lax.fori_loop(0, NA_R, row_body, 0, unroll=True)


def _na(q3, k3, v3, toep):
    b, t, w = q3.shape
    rows = t // GRID_W
    tq = NA_R * GRID_W
    qtile = pl.BlockSpec((None, tq, w), lambda bi, i: (bi, i, 0))
    seq = pl.BlockSpec((None, t, w), lambda bi, i: (bi, 0, 0))
    toep_spec = pl.BlockSpec(toep.shape, lambda bi, i: (0, 0, 0), pipeline_mode=pl.Buffered(1))
    return pl.pallas_call(
        functools.partial(_na_kernel, rows=rows),
        grid=(b, rows // NA_R),
        in_specs=[qtile, seq, seq, toep_spec],
        out_specs=qtile,
        out_shape=jax.ShapeDtypeStruct((b, t, w), BF16),
        scratch_shapes=[pltpu.VMEM((NA_KH, NA_HEADS // NA_PACK, NA_PACK * GRID_W, NA_KH * GRID_W), F32)],
        compiler_params=pltpu.CompilerParams(
            dimension_semantics=("arbitrary", "arbitrary"), vmem_limit_bytes=VMEM_LIMIT),
        name="natten",
    )(q3, k3, v3, toep)


def _na_bias_table(rpb):
    nh, ndr, ndc = rpb.shape
    mid = NA_KW - 1
    w = rpb.astype(F32) * LOG2E
    gap = jnp.zeros((nh, ndr, LANES - ndc), F32)
    return jnp.concatenate([w[..., mid:], gap, w[..., :mid]], axis=-1)


def _outproj_ffn2_kernel(x1_ref, yml_ref, yna_ref, wout_hbm, g2_ref, wg_hbm, wu_hbm, wd_hbm, gf_ref, out_ref,
                         wout_ref, wg_ref, wu_ref, wd_ref, stage_ff, sem_ff, stage_dm, sem_dm):
    @pl.when(pl.program_id(0) == 0)
    def _():
        _load_weights_as_bf16(
            [(wout_hbm, wout_ref), (wg_hbm, wg_ref), (wu_hbm, wu_ref), (wd_hbm, wd_ref)],
            {D_FF: (stage_ff, sem_ff), D_MODEL: (stage_dm, sem_dm)})

    x2 = x1_ref[...] + _dot(jnp.concatenate([yml_ref[...], yna_ref[...]], axis=1), wout_ref[...])
    h = _rms(x2, g2_ref[...]).astype(BF16)
    x3 = x2 + _swiglu_half(h, wg_ref, wu_ref, wd_ref)
    out_ref[...] = _rms(x3, gf_ref[...])


def _outproj_ffn2(x1, yml, yna, wout, g2, wg, wu, wd, gf):
    n = x1.shape[0]
    tm = FFN_TM
    tok = lambda w: pl.BlockSpec((tm, w), lambda i: (i, 0))
    hbm = pl.BlockSpec(memory_space=pl.ANY)
    return pl.pallas_call(
        _outproj_ffn2_kernel,
        grid=(n // tm,),
        in_specs=[tok(D_MODEL), tok(ML_WIDTH), tok(NA_WIDTH), hbm, _const_spec(g2.shape), hbm, hbm, hbm,
                  _const_spec(gf.shape)],
        out_specs=tok(D_MODEL),
        out_shape=jax.ShapeDtypeStruct((n, D_MODEL), F32),
        scratch_shapes=[pltpu.VMEM(a.shape, BF16) for a in (wout, wg, wu, wd)]
        + _stage_scratch(D_FF) + _stage_scratch(D_MODEL),
        compiler_params=pltpu.CompilerParams(
            dimension_semantics=("arbitrary",), vmem_limit_bytes=VMEM_LIMIT),
        name="outproj_ffn2",
    )(x1, yml, yna, wout, g2, wg, wu, wd, gf)


def _layer(x, norm_ffn1, w1_gate, w1_up, w1_down, norm_mix, w_in, b_gates, conv_w, conv_b, w_q_ml,
           w_k_ml, gn_ml, gq_na, gk_na, rpb, w_out, norm_ffn2, w2_gate, w2_up, w2_down, norm_final):
    b, t, dm = x.shape
    n = b * t
    W = ML_WIDTH
    row = lambda a: a.reshape(1, -1).astype(F32)
    g0 = 3 * W
    wgate = w_in[:, g0:g0 + N_GATES]
    H = ML_HEADS
    order = np.concatenate([dd * 2 * H + np.r_[0:2 * H, H:2 * H, 0:H] for dd in range(2)])
    wvg = jnp.concatenate([w_in[:, W:2 * W].T, wgate.T[order]], axis=0).astype(BF16)
    bgr = b_gates[order].reshape(-1, 1).astype(F32)
    ones = jnp.asarray(np.kron(np.eye(NA_HEADS), np.ones((NA_DH, NA_DH))), BF16)
    gq = jnp.tile(gq_na.astype(F32), NA_HEADS).reshape(1, -1)
    gk = jnp.tile(gk_na.astype(F32), NA_HEADS).reshape(1, -1)

    x1, c, vt_ml, o, ga, gb, qn, kn, vn = _ffn1_inproj(
        x.reshape(n, dm), row(norm_ffn1), w1_gate.astype(F32), w1_up.astype(F32), w1_down.astype(F32),
        row(norm_mix), w_in.astype(F32), wvg, w_in[:, g0 + N_GATES:].astype(BF16), bgr, ones, gq, gk)

    seq = lambda a: a.reshape(b, t, a.shape[-1])
    qt_ml, k_ml = _conv_qk(seq(c), conv_w.astype(F32), row(conv_b),
                           w_q_ml.transpose(0, 2, 1).astype(BF16), w_k_ml.astype(BF16))
    rowq, colq = _mlstm_gates(ga.reshape(2, -1, ML_L), gb.reshape(2, -1, ML_L), b, t)
    gnb = jnp.broadcast_to(gn_ml.astype(F32).reshape(W, 1), (W, ML_L))
    y_ml = _mlstm(qt_ml, k_ml, vt_ml.reshape(b, t // ML_L, W, ML_L), seq(o), rowq, colq, gnb)
    y_na = _na(seq(qn), seq(kn), seq(vn), _na_bias_table(rpb))

    out = _outproj_ffn2(x1, y_ml.reshape(n, W), y_na.reshape(n, NA_WIDTH),
                        w_out.astype(F32), row(norm_ffn2),
                        w2_gate.astype(F32), w2_up.astype(F32), w2_down.astype(F32), row(norm_final))
    return out.reshape(b, t, dm)


def kernel(x, norm_ffn1, w1_gate, w1_up, w1_down, norm_mix, w_in, b_gates, conv_w, conv_b, w_q_ml, w_k_ml,
           gn_ml, gq_na, gk_na, rpb, w_out, norm_ffn2, w2_gate, w2_up, w2_down, norm_final):
    depth = norm_ffn1.shape[0]
    for l in range(depth):
        x = _layer(x, norm_ffn1[l], w1_gate[l], w1_up[l], w1_down[l], norm_mix[l], w_in[l], b_gates[l],
                   conv_w[l], conv_b[l], w_q_ml[l], w_k_ml[l], gn_ml[l], gq_na[l], gk_na[l], rpb[l],
                   w_out[l], norm_ffn2[l], w2_gate[l], w2_up[l], w2_down[l], norm_final[l])
    return x
```

```python
import functools

import jax
import jax.numpy as jnp
import numpy as np
from jax import lax
from jax.experimental import pallas as pl
from jax.experimental.pallas import tpu as pltpu

F32 = jnp.float32
BF16 = jnp.bfloat16

D_MODEL = 1024
D_FF = 2816
GRID_W = 64
ML_HEADS = 4
ML_DH = 128
ML_WIDTH = ML_HEADS * ML_DH
CONV_W = 5
NA_HEADS = 8
NA_DH = 64
NA_WIDTH = NA_HEADS * NA_DH
NA_KH = 8
NA_KW = 16
N_GATES = 4 * ML_HEADS
EPS = 1e-6
NEG = -1e30
LOG2E = 1.4426950408889634

SUBLANES = 8
LANES = 128

FFN_TM = 512
FFN_CK = 2816
FFN_STAGE_ROWS = 64
FFN_STAGE_DEPTH = 6
ML_L = 128
ML_TT = 2048
ML_NROWQ = 6
ML_NONES = 16
CONV_TT = 1024
NA_R = 16
NA_PACK = 4
VMEM_LIMIT = 56 * 1024 * 1024


def _dot(a, b):
    return jnp.dot(a, b, preferred_element_type=F32)


def _dot_nt(a, b):
    return lax.dot_general(a, b, (((1,), (1,)), ((), ())), preferred_element_type=F32)


def _rms(x, g):
    ms = jnp.mean(x * x, axis=-1, keepdims=True)
    return x * lax.rsqrt(ms + EPS) * g


def _log_sigmoid(x):
    return jnp.minimum(x, 0.0) - jnp.log1p(jnp.exp(-jnp.abs(x)))


def _swiglu_half(h, wg_ref, wu_ref, wd_ref):
    acc = None
    for j in range(D_FF // FFN_CK):
        sl = slice(j * FFN_CK, (j + 1) * FFN_CK)
        g = _dot(h, wg_ref[:, sl])
        u = _dot(h, wu_ref[:, sl])
        a = (g * jax.nn.sigmoid(g) * u).astype(BF16)
        part = _dot(a, wd_ref[sl, :])
        acc = part if acc is None else acc + part
    return 0.5 * acc


def _stage_rows(cols):
    return FFN_STAGE_ROWS * D_FF // cols


def _stage_scratch(cols):
    return [pltpu.VMEM((FFN_STAGE_DEPTH, _stage_rows(cols), cols), F32),
            pltpu.SemaphoreType.DMA((FFN_STAGE_DEPTH,))]


def _load_weights_as_bf16(jobs, stages):
    chunks, issued = [], {}
    for src, dst in jobs:
        rows, cols = src.shape
        step = _stage_rows(cols)
        for r0 in range(0, rows, step):
            slot = issued.get(cols, 0) % FFN_STAGE_DEPTH
            issued[cols] = issued.get(cols, 0) + 1
            chunks.append((src, dst, r0, min(step, rows - r0), cols, slot))

    def copy_of(chunk):
        src, _, r0, nr, cols, slot = chunk
        stage, sem = stages[cols]
        return pltpu.make_async_copy(src.at[pl.ds(r0, nr), :], stage.at[slot, pl.ds(0, nr), :], sem.at[slot])

    ahead = FFN_STAGE_DEPTH - 1
    for chunk in chunks[:ahead]:
        copy_of(chunk).start()
    for k, chunk in enumerate(chunks):
        if k + ahead < len(chunks):
            copy_of(chunks[k + ahead]).start()
        copy_of(chunk).wait()
        _, dst, r0, nr, cols, slot = chunk
        dst[r0:r0 + nr, :] = stages[cols][0][slot, 0:nr, :].astype(BF16)


def _group_mean_sq(x, ones_ref):
    xx = x * x
    return _dot(xx.astype(BF16), ones_ref[...]) * (1.0 / NA_DH)


def _ffn1_inproj_kernel(x_ref, g1_ref, wg_hbm, wu_hbm, wd_hbm, gm_ref, wc_ref, wvg_ref, wo_ref, wna_ref,
                        bgr_ref, ones_ref, gq_ref, gk_ref,
                        x1_ref, c_ref, vt_ref, o_ref, ga_ref, gb_ref, qn_ref, kn_ref, vn_ref,
                        wg_ref, wu_ref, wd_ref, stage_ff, sem_ff, stage_dm, sem_dm):
    @pl.when(pl.program_id(0) == 0)
    def _():
        _load_weights_as_bf16([(wg_hbm, wg_ref), (wu_hbm, wu_ref), (wd_hbm, wd_ref)],
                              {D_FF: (stage_ff, sem_ff), D_MODEL: (stage_dm, sem_dm)})

    x = x_ref[...]
    h = _rms(x, g1_ref[...]).astype(BF16)
    x1 = x + _swiglu_half(h, wg_ref, wu_ref, wd_ref)
    x1_ref[...] = x1
    h2 = _rms(x1, gm_ref[...]).astype(BF16)
    W = ML_WIDTH
    c_ref[...] = _dot(h2, wc_ref[...])
    o_ref[...] = jax.nn.sigmoid(_dot(h2, wo_ref[...])).astype(BF16)
    vg = _dot_nt(wvg_ref[...], h2)
    vt = vg[:W].astype(BF16)
    for cidx in range(FFN_TM // ML_L):
        vt_ref[cidx] = vt[:, cidx * ML_L:(cidx + 1) * ML_L]
    p = _dot(h2, wna_ref[...])
    q = p[:, 0:W]
    k = p[:, W:2 * W]
    vn_ref[...] = p[:, 2 * W:3 * W].astype(BF16)
    qn = q * lax.rsqrt(_group_mean_sq(q, ones_ref) + EPS) * gq_ref[...] * (NA_DH ** -0.5 * LOG2E)
    kn = k * lax.rsqrt(_group_mean_sq(k, ones_ref) + EPS) * gk_ref[...]
    qn_ref[...] = qn.astype(BF16)
    kn_ref[...] = kn.astype(BF16)
    gr = vg[W:] + bgr_ref[...]
    row = lax.broadcasted_iota(jnp.int32, gr.shape, 0) % (4 * ML_HEADS)
    gr = jnp.where((row >= ML_HEADS) & (row < 3 * ML_HEADS), _log_sigmoid(gr), gr)
    R = SUBLANES
    for dd in range(2):
        for cidx in range(FFN_TM // ML_L):
            lanes = slice(cidx * ML_L, (cidx + 1) * ML_L)
            ga_ref[dd, cidx] = gr[2 * dd * R:(2 * dd + 1) * R, lanes]
            gb_ref[dd, cidx] = gr[(2 * dd + 1) * R:(2 * dd + 2) * R, lanes]


def _const_spec(shape):
    nd = len(shape)
    return pl.BlockSpec(shape, lambda *_: (0,) * nd, pipeline_mode=pl.Buffered(1))


def _ffn1_inproj(x2d, g1, wg, wu, wd, gm, wc, wvg, wo, wna, bgr, ones, gq, gk):
    n = x2d.shape[0]
    tm = FFN_TM
    W = ML_WIDTH
    tok = lambda w: pl.BlockSpec((tm, w), lambda i: (i, 0))
    out_shape = (
        jax.ShapeDtypeStruct((n, D_MODEL), F32),
        jax.ShapeDtypeStruct((n, W), F32),
        jax.ShapeDtypeStruct((n // ML_L, W, ML_L), BF16),
        jax.ShapeDtypeStruct((n, W), BF16),
        jax.ShapeDtypeStruct((2, n // ML_L, SUBLANES, ML_L), F32),
        jax.ShapeDtypeStruct((2, n // ML_L, SUBLANES, ML_L), F32),
        jax.ShapeDtypeStruct((n, W), BF16),
        jax.ShapeDtypeStruct((n, W), BF16),
        jax.ShapeDtypeStruct((n, W), BF16),
    )
    out_specs = (
        tok(D_MODEL), tok(W), pl.BlockSpec((tm // ML_L, W, ML_L), lambda i: (i, 0, 0)), tok(W),
        pl.BlockSpec((2, tm // ML_L, SUBLANES, ML_L), lambda i: (0, i, 0, 0)),
        pl.BlockSpec((2, tm // ML_L, SUBLANES, ML_L), lambda i: (0, i, 0, 0)),
        tok(W), tok(W), tok(W),
    )
    hbm = pl.BlockSpec(memory_space=pl.ANY)
    in_specs = [tok(D_MODEL), _const_spec(g1.shape), hbm, hbm, hbm] + [
        _const_spec(a.shape) for a in (gm, wc, wvg, wo, wna, bgr, ones, gq, gk)]
    return pl.pallas_call(
        _ffn1_inproj_kernel,
        grid=(n // tm,),
        in_specs=in_specs,
        out_specs=out_specs,
        out_shape=out_shape,
        scratch_shapes=[pltpu.VMEM(a.shape, BF16) for a in (wg, wu, wd)]
        + _stage_scratch(D_FF) + _stage_scratch(D_MODEL),
        compiler_params=pltpu.CompilerParams(
            dimension_semantics=("arbitrary",), vmem_limit_bytes=VMEM_LIMIT),
        name="ffn1_inproj",
    )(x2d, g1, wg, wu, wd, gm, wc, wvg, wo, wna, bgr, ones, gq, gk)


def _conv_qk_kernel(c_ref, prev_ref, next_ref, cw_ref, cb_ref, wqt_ref, wk_ref, qt_ref, k_ref, pad_ref):
    i = pl.program_id(1)
    nb = pl.num_programs(1)
    tt = CONV_TT
    half = CONV_W // 2
    pad_ref[0:SUBLANES, :] = jnp.where(i > 0, prev_ref[...], 0.0)
    pad_ref[SUBLANES:SUBLANES + tt, :] = c_ref[...]
    pad_ref[SUBLANES + tt:2 * SUBLANES + tt, :] = jnp.where(i < nb - 1, next_ref[...], 0.0)
    y = cb_ref[...]
    for kk in range(CONV_W):
        start = SUBLANES + kk - half
        y = y + pad_ref[start:start + tt, :] * cw_ref[kk:kk + 1, :]
    hy = 0.5 * y
    u = hy + hy * jnp.tanh(hy)
    for h in range(ML_HEADS):
        sl = slice(h * ML_DH, (h + 1) * ML_DH)
        uh = u[:, sl].astype(BF16)
        k_ref[:, sl] = (_dot(uh, wk_ref[h]) * (ML_DH ** -0.5)).astype(BF16)
        qt = _dot_nt(wqt_ref[h], uh).astype(BF16)
        for cidx in range(tt // ML_L):
            qt_ref[cidx, sl, :] = qt[:, cidx * ML_L:(cidx + 1) * ML_L]


def _conv_qk(c3, cw, cb, wqt, wk):
    b, t, w = c3.shape
    tt = CONV_TT
    nb = t // tt
    per = tt // SUBLANES
    last = t // SUBLANES - 1
    tile = pl.BlockSpec((None, tt, w), lambda bi, i: (bi, i, 0))
    prev = pl.BlockSpec((None, SUBLANES, w), lambda bi, i: (bi, jnp.maximum(i * per - 1, 0), 0))
    nxt = pl.BlockSpec((None, SUBLANES, w), lambda bi, i: (bi, jnp.minimum((i + 1) * per, last), 0))
    return pl.pallas_call(
        _conv_qk_kernel,
        grid=(b, nb),
        in_specs=[tile, prev, nxt] + [
            pl.BlockSpec(a.shape, functools.partial(lambda nd, bi, i: (0,) * nd, a.ndim))
            for a in (cw, cb, wqt, wk)],
        out_specs=(pl.BlockSpec((None, tt // ML_L, w, ML_L), lambda bi, i: (bi, i, 0, 0)), tile),
        out_shape=(jax.ShapeDtypeStruct((b, t // ML_L, w, ML_L), BF16),
                   jax.ShapeDtypeStruct((b, t, w), BF16)),
        scratch_shapes=[pltpu.VMEM((tt + 2 * SUBLANES, w), F32)],
        compiler_params=pltpu.CompilerParams(dimension_semantics=("parallel", "parallel")),
        name="conv_qk",
    )(c3, c3, c3, cw, cb, wqt, wk)


def _lane_scan(x, pos, d, combine, ident):
    L = x.shape[-1]
    k = 1
    while k < L:
        shifted = pltpu.roll(x, jnp.where(d == 0, k, L - k), axis=1)
        x = combine(x, jnp.where(pos >= k, shifted, ident))
        k *= 2
    return x


def _mlstm_gates_kernel(ga_ref, gb_ref, rowq_ref, colq_ref, b_s, g_s, mloc_s, rmax_s, w_s, pack_s):
    d = pl.program_id(1)
    R = SUBLANES
    L = ML_L
    nc = ga_ref.shape[0] // R
    xa = ga_ref[...]
    xb = gb_ref[...]
    lane = lax.broadcasted_iota(jnp.int32, xa.shape, 1)
    pos = lane + d * (L - 1 - 2 * lane)
    b = _lane_scan(xb, pos, d, jnp.add, 0.0)
    g = jnp.broadcast_to(jnp.sum(xb, axis=1, keepdims=True), xb.shape)
    w = xa - b
    a = g + w
    m_loc = jnp.broadcast_to(jnp.max(a, axis=1, keepdims=True), a.shape)
    rowq_ref[3] = jnp.exp(a - m_loc)
    b_s[...] = b
    g_s[...] = g
    mloc_s[...] = m_loc
    rmax_s[...] = _lane_scan(w, pos, d, jnp.maximum, NEG)
    w_s[...] = w * LOG2E

    P = ML_TT // L
    for blk in range(nc // P):
        for j in range(P):
            cj = blk * P + j + d * (P - 1 - 2 * j)
            pack_s[blk, j * R:(j + 1) * R, :] = w_s[pl.ds(pl.multiple_of(cj * R, R), R), :]
        colq_ref[blk] = pack_s[blk].T

    def chunk(k, m0):
        ck = k + d * (nc - 1 - 2 * k)
        rows = pl.ds(pl.multiple_of(ck * R, R), R)
        gk = g_s[rows, :]
        mlk = mloc_s[rows, :]
        mm = jnp.maximum(m0, rmax_s[rows, :])
        m_new = jnp.maximum(gk + m0, mlk)
        rowq_ref[0, rows, :] = mm * LOG2E
        rowq_ref[1, rows, :] = jnp.exp(m0 - mm)
        rowq_ref[2, rows, :] = jnp.exp(-(b_s[rows, :] + mm))
        rowq_ref[4, rows, :] = jnp.exp(gk + m0 - m_new)
        rowq_ref[5, rows, :] = jnp.exp(mlk - m_new)
        return m_new

    lax.fori_loop(0, nc, chunk, jnp.zeros((R, L), F32), unroll=8)


def _mlstm_gates(ga, gb, b, t):
    nc = t // ML_L
    R = SUBLANES
    nstep = t // ML_TT
    assert ML_TT // ML_L * R == LANES
    in_spec = pl.BlockSpec((None, nc * R, ML_L), lambda bi, d: (d, bi, 0))
    return pl.pallas_call(
        _mlstm_gates_kernel,
        grid=(b, 2),
        in_specs=[in_spec, in_spec],
        out_specs=(pl.BlockSpec((None, None, ML_NROWQ, nc * R, ML_L), lambda bi, d: (bi, d, 0, 0, 0)),
                   pl.BlockSpec((None, None, nstep, ML_L, LANES), lambda bi, d: (bi, d, 0, 0, 0))),
        out_shape=(jax.ShapeDtypeStruct((b, 2, ML_NROWQ, nc * R, ML_L), F32),
                   jax.ShapeDtypeStruct((b, 2, nstep, ML_L, LANES), F32)),
        scratch_shapes=[pltpu.VMEM((nc * R, ML_L), F32)] * 5 + [pltpu.VMEM((nstep, LANES, ML_L), F32)],
        compiler_params=pltpu.CompilerParams(dimension_semantics=("parallel", "parallel")),
        name="mlstm_gates",
    )(ga, gb)


def _mlstm_kernel(qt_ref, k_ref, vt_ref, o_ref, rowq_ref, colq_ref, gnb_ref, y_ref,
                  hf_ref, state_ref, sq_s, upd_s, *, nblk):
    d = pl.program_id(1)
    i = pl.program_id(2)
    blk = i + d * (nblk - 1 - 2 * i)
    L = ML_L
    nch = ML_TT // L
    H = ML_HEADS
    DH = ML_DH
    R = SUBLANES

    @pl.when(i == 0)
    def _():
        state_ref[...] = jnp.zeros_like(state_ref)

    s_idx = lax.broadcasted_iota(jnp.int32, (L, L), 0)
    t_idx = lax.broadcasted_iota(jnp.int32, (L, L), 1)
    fwd = d == 0
    mask = (t_idx - s_idx) * (1 - 2 * d) >= 0
    ones = jnp.ones((ML_NONES, L), BF16)
    NA = DH + ML_NONES
    c_total = hf_ref.shape[0] - nch
    hbase = jnp.where(fwd, blk * nch, c_total)

    def chunk_of(j):
        cj = j + d * (nch - 1 - 2 * j)
        return cj, pl.ds(pl.multiple_of(cj * R, R), R)

    for j in range(nch):
        cj, rows = chunk_of(j)
        off = pl.multiple_of(cj * L, L)
        mm = rowq_ref[0, rows, :]
        sc = rowq_ref[1, rows, :]
        w_end = rowq_ref[3, rows, :]
        cols = colq_ref[:, j * R:(j + 1) * R]
        for h in range(0, H, 2):
            ka = k_ref[pl.ds(off, L), h * DH:(h + 1) * DH]
            kb = k_ref[pl.ds(off, L), (h + 1) * DH:(h + 2) * DH]
            qta = qt_ref[cj, h * DH:(h + 1) * DH, :]
            qtb = qt_ref[cj, (h + 1) * DH:(h + 2) * DH, :]
            zero = jnp.zeros((DH, L), BF16)
            st2 = _dot(jnp.concatenate([ka, kb], axis=1),
                       jnp.concatenate([jnp.concatenate([qta, zero], axis=1),
                                        jnp.concatenate([zero, qtb], axis=1)], axis=0))
            vws = []
            for hh, qt in ((h, qta), (h + 1, qtb)):
                sl = slice(hh * DH, (hh + 1) * DH)
                lanes = slice((hh - h) * L, (hh - h + 1) * L)
                vaug = jnp.concatenate([vt_ref[cj, sl, :], ones], axis=0)
                e = jnp.exp2(jnp.where(mask, cols[:, hh:hh + 1] - mm[hh:hh + 1, :], NEG))
                sq_s[j * H + hh, 0:L, :] = (st2[:, lanes] * e).astype(BF16)
                sq_s[j * H + hh, L:L + DH, :] = qt * sc[hh:hh + 1, :].astype(BF16)
                vws.append(vaug * w_end[hh:hh + 1, :].astype(BF16))
            upd2 = _dot(jnp.concatenate(vws, axis=1),
                        jnp.concatenate([jnp.concatenate([ka, zero], axis=1),
                                         jnp.concatenate([zero, kb], axis=1)], axis=0))
            upd_s[j * H + h] = upd2[:, :DH]
            upd_s[j * H + h + 1] = upd2[:, DH:]

    for j in range(nch):
        cj, rows = chunk_of(j)
        em = rowq_ref[2, rows, :]
        s_old = rowq_ref[4, rows, :]
        s_loc = rowq_ref[5, rows, :]
        for h in range(H):
            sl = slice(h * DH, (h + 1) * DH)
            vaug = jnp.concatenate([vt_ref[cj, sl, :], ones], axis=0)
            st = state_ref[h]
            r = _dot(jnp.concatenate([vaug, st.astype(BF16)], axis=1), sq_s[j * H + h])
            inv = 1.0 / jnp.maximum(jnp.abs(r[DH:DH + 1, :]), em[h:h + 1, :])
            hf_ref[hbase + cj, sl, :] = r[:DH, :] * inv
            state_ref[h] = (jnp.broadcast_to(s_old[h:h + 1, :], (NA, DH)) * st
                            + jnp.broadcast_to(s_loc[h:h + 1, :], (NA, DH)) * upd_s[j * H + h])

    @pl.when(jnp.logical_not(fwd))
    def _():
        for c in range(nch):
            for h in range(H):
                sl = slice(h * DH, (h + 1) * DH)
                tot = hf_ref[blk * nch + c, sl, :] + hf_ref[c_total + c, sl, :]
                ms = jnp.mean(tot * tot, axis=0, keepdims=True)
                hn = (tot * lax.rsqrt(ms + EPS) * gnb_ref[sl, :]).T
                tok = slice(c * L, (c + 1) * L)
                y_ref[tok, sl] = (o_ref[tok, sl].astype(F32) * hn).astype(BF16)


def _mlstm(qt4, k3, vt4, o3, rowq, colq, gnb):
    b, t, w = k3.shape
    tt = ML_TT
    nblk = t // tt
    nch = tt // ML_L
    blk_of = lambda d, i: i + d * (nblk - 1 - 2 * i)
    tile = pl.BlockSpec((None, tt, w), lambda bi, d, i: (bi, blk_of(d, i), 0))
    ft_spec = pl.BlockSpec((None, nch, w, ML_L), lambda bi, d, i: (bi, blk_of(d, i), 0, 0))
    rq_spec = pl.BlockSpec((None, None, ML_NROWQ, nch * SUBLANES, ML_L),
                           lambda bi, d, i: (bi, d, 0, blk_of(d, i), 0))
    cq_spec = pl.BlockSpec((None, None, None, ML_L, LANES), lambda bi, d, i: (bi, d, blk_of(d, i), 0, 0))
    o_spec = pl.BlockSpec((None, tt, w), lambda bi, d, i: (bi, d * blk_of(d, i) + (1 - d) * (nblk - 1), 0))
    gn_spec = pl.BlockSpec(gnb.shape, lambda bi, d, i: (0, 0))
    y_spec = pl.BlockSpec((None, tt, w), lambda bi, d, i: (bi, d * blk_of(d, i) + (1 - d) * (nblk - 1), 0))
    return pl.pallas_call(
        functools.partial(_mlstm_kernel, nblk=nblk),
        grid=(b, 2, nblk),
        in_specs=[ft_spec, tile, ft_spec, o_spec, rq_spec, cq_spec, gn_spec],
        out_specs=y_spec,
        out_shape=jax.ShapeDtypeStruct((b, t, w), BF16),
        scratch_shapes=[
            pltpu.VMEM((t // ML_L + nch, w, ML_L), F32),
            pltpu.VMEM((ML_HEADS, ML_DH + ML_NONES, ML_DH), F32),
            pltpu.VMEM((nch * ML_HEADS, ML_L + ML_DH, ML_L), BF16),
            pltpu.VMEM((nch * ML_HEADS, ML_DH + ML_NONES, ML_DH), F32),
        ],
        compiler_params=pltpu.CompilerParams(
            dimension_semantics=("arbitrary", "arbitrary", "arbitrary"), vmem_limit_bytes=VMEM_LIMIT),
        name="mlstm",
    )(qt4, k3, vt4, o3, rowq, colq, gnb)


def _na_kernel(q_ref, k_ref, v_ref, toep_ref, out_ref, bias_ref, *, rows):
    r0 = pl.program_id(1) * NA_R
    nkeys = NA_KH * GRID_W
    gw = NA_PACK * NA_DH

    @pl.when((pl.program_id(0) == 0) & (pl.program_id(1) == 0))
    def _():
        c = lax.broadcasted_iota(jnp.int32, (GRID_W, LANES), 0)
        lane = lax.broadcasted_iota(jnp.int32, (GRID_W, LANES), 1)
        cs = jnp.clip(c - NA_KW // 2, 0, GRID_W - NA_KW)
        in_lo = (lane >= cs) & (lane < cs + NA_KW)
        in_hi = (lane >= cs + GRID_W) & (lane < cs + GRID_W + NA_KW)
        ndr = 2 * NA_KH - 1
        for h in range(NA_HEADS):
            g, hh = divmod(h, NA_PACK)
            rows_of = [jnp.broadcast_to(toep_ref[h, dr:dr + 1, :], (GRID_W, LANES)) for dr in range(ndr)]
            lo = [pltpu.roll(t, 0, axis=1, stride=1, stride_axis=0) for t in rows_of]
            hi = [pltpu.roll(t, GRID_W, axis=1, stride=1, stride_axis=0) for t in rows_of]
            for delta in range(NA_KH):
                for jj in range(NA_KH // 2):
                    dr = 2 * jj + NA_KH - 1 - delta
                    pair = jnp.where(in_lo, lo[dr], jnp.where(in_hi, hi[dr + 1], NEG))
                    bias_ref[delta, g, hh * GRID_W:(hh + 1) * GRID_W, jj * LANES:(jj + 1) * LANES] = pair

    row_blk = lax.broadcasted_iota(jnp.int32, (NA_PACK * GRID_W, gw), 0) // GRID_W
    lane_blk = lax.broadcasted_iota(jnp.int32, (NA_PACK * GRID_W, gw), 1) // NA_DH
    diag = row_blk == lane_blk
    out_blk = lax.broadcasted_iota(jnp.int32, (GRID_W, gw), 1) // NA_DH

    ng = NA_HEADS // NA_PACK

    def window(ri):
        r = r0 + ri
        rs = jnp.clip(r - NA_KH // 2, 0, rows - NA_KH)
        return pl.ds(pl.multiple_of(rs * GRID_W, GRID_W), nkeys), r - rs

    def row_body(ri, carry):
        keys, delta = window(ri)
        qoff = pl.multiple_of(ri * GRID_W, GRID_W)
        outs = []
        for g in range(ng):
            sl = slice(g * gw, (g + 1) * gw)
            q4 = q_ref[pl.ds(qoff, GRID_W), sl]
            qbd = jnp.where(diag, jnp.concatenate([q4] * NA_PACK, axis=0), jnp.zeros((), BF16))
            s = _dot_nt(qbd, k_ref[keys, sl]) + bias_ref[delta, g]
            p = jnp.exp2(s - jnp.max(s, axis=-1, keepdims=True))
            linv = 1.0 / jnp.sum(p, axis=-1, keepdims=True)
            o = _dot(p.astype(BF16), v_ref[keys, sl]) * linv
            og = o[(NA_PACK - 1) * GRID_W:, :]
            for h in range(NA_PACK - 2, -1, -1):
                og = jnp.where(out_blk == h, o[h * GRID_W:(h + 1) * GRID_W, :], og)
            outs.append(og)
        out_ref[pl.ds(qoff, GRID_W), :] = jnp.concatenate(outs, axis=-1).astype(BF16)
        return carry

    lax.fori_loop(0, NA_R, row_body, 0, unroll=True)


def _na(q3, k3, v3, toep):
    b, t, w = q3.shape
    rows = t // GRID_W
    tq = NA_R * GRID_W
    qtile = pl.BlockSpec((None, tq, w), lambda bi, i: (bi, i, 0))
    seq = pl.BlockSpec((None, t, w), lambda bi, i: (bi, 0, 0))
    toep_spec = pl.BlockSpec(toep.shape, lambda bi, i: (0, 0, 0), pipeline_mode=pl.Buffered(1))
    return pl.pallas_call(
        functools.partial(_na_kernel, rows=rows),
        grid=(b, rows // NA_R),
        in_specs=[qtile, seq, seq, toep_spec],
        out_specs=qtile,
        out_shape=jax.ShapeDtypeStruct((b, t, w), BF16),
        scratch_shapes=[pltpu.VMEM((NA_KH, NA_HEADS // NA_PACK, NA_PACK * GRID_W, NA_KH * GRID_W), F32)],
        compiler_params=pltpu.CompilerParams(
            dimension_semantics=("arbitrary", "arbitrary"), vmem_limit_bytes=VMEM_LIMIT),
        name="natten",
    )(q3, k3, v3, toep)


def _na_bias_table(rpb):
    nh, ndr, ndc = rpb.shape
    mid = NA_KW - 1
    w = rpb.astype(F32) * LOG2E
    gap = jnp.zeros((nh, ndr, LANES - ndc), F32)
    return jnp.concatenate([w[..., mid:], gap, w[..., :mid]], axis=-1)


def _outproj_ffn2_kernel(x1_ref, yml_ref, yna_ref, wout_hbm, g2_ref, wg_hbm, wu_hbm, wd_hbm, gf_ref, out_ref,
                         wout_ref, wg_ref, wu_ref, wd_ref, stage_ff, sem_ff, stage_dm, sem_dm):
    @pl.when(pl.program_id(0) == 0)
    def _():
        _load_weights_as_bf16(
            [(wout_hbm, wout_ref), (wg_hbm, wg_ref), (wu_hbm, wu_ref), (wd_hbm, wd_ref)],
            {D_FF: (stage_ff, sem_ff), D_MODEL: (stage_dm, sem_dm)})

    x2 = x1_ref[...] + _dot(jnp.concatenate([yml_ref[...], yna_ref[...]], axis=1), wout_ref[...])
    h = _rms(x2, g2_ref[...]).astype(BF16)
    x3 = x2 + _swiglu_half(h, wg_ref, wu_ref, wd_ref)
    out_ref[...] = _rms(x3, gf_ref[...])


def _outproj_ffn2(x1, yml, yna, wout, g2, wg, wu, wd, gf):
    n = x1.shape[0]
    tm = FFN_TM
    tok = lambda w: pl.BlockSpec((tm, w), lambda i: (i, 0))
    hbm = pl.BlockSpec(memory_space=pl.ANY)
    return pl.pallas_call(
        _outproj_ffn2_kernel,
        grid=(n // tm,),
        in_specs=[tok(D_MODEL), tok(ML_WIDTH), tok(NA_WIDTH), hbm, _const_spec(g2.shape), hbm, hbm, hbm,
                  _const_spec(gf.shape)],
        out_specs=tok(D_MODEL),
        out_shape=jax.ShapeDtypeStruct((n, D_MODEL), F32),
        scratch_shapes=[pltpu.VMEM(a.shape, BF16) for a in (wout, wg, wu, wd)]
        + _stage_scratch(D_FF) + _stage_scratch(D_MODEL),
        compiler_params=pltpu.CompilerParams(
            dimension_semantics=("arbitrary",), vmem_limit_bytes=VMEM_LIMIT),
        name="outproj_ffn2",
    )(x1, yml, yna, wout, g2, wg, wu, wd, gf)


def _layer(x, norm_ffn1, w1_gate, w1_up, w1_down, norm_mix, w_in, b_gates, conv_w, conv_b, w_q_ml,
           w_k_ml, gn_ml, gq_na, gk_na, rpb, w_out, norm_ffn2, w2_gate, w2_up, w2_down, norm_final):
    b, t, dm = x.shape
    n = b * t
    W = ML_WIDTH
    row = lambda a: a.reshape(1, -1).astype(F32)
    g0 = 3 * W
    wc = w_in[:, :W].astype(BF16)
    wo = w_in[:, 2 * W:g0].astype(BF16)
    wna = w_in[:, g0 + N_GATES:].astype(BF16)
    wgate = w_in[:, g0:g0 + N_GATES]
    H = ML_HEADS
    order = [slice(dd * 2 * H + lo, dd * 2 * H + hi)
             for dd in range(2) for lo, hi in ((0, 2 * H), (H, 2 * H), (0, H))]
    wgate_t = wgate.T
    wvg = jnp.concatenate([w_in[:, W:2 * W].T] + [wgate_t[s] for s in order], axis=0).astype(BF16)
    bgr = jnp.concatenate([b_gates[s] for s in order]).reshape(-1, 1).astype(F32)
    ones = jnp.asarray(np.kron(np.eye(NA_HEADS), np.ones((NA_DH, NA_DH))), BF16)
    gq = jnp.tile(gq_na.astype(F32), NA_HEADS).reshape(1, -1)
    gk = jnp.tile(gk_na.astype(F32), NA_HEADS).reshape(1, -1)

    x1, c, vt_ml, o, ga, gb, qn, kn, vn = _ffn1_inproj(
        x.reshape(n, dm), row(norm_ffn1), w1_gate.astype(F32), w1_up.astype(F32), w1_down.astype(F32),
        row(norm_mix), wc, wvg, wo, wna, bgr, ones, gq, gk)

    seq = lambda a: a.reshape(b, t, a.shape[-1])
    qt_ml, k_ml = _conv_qk(seq(c), conv_w.astype(F32), row(conv_b),
                           w_q_ml.transpose(0, 2, 1).astype(BF16), w_k_ml.astype(BF16))
    rowq, colq = _mlstm_gates(ga.reshape(2, -1, ML_L), gb.reshape(2, -1, ML_L), b, t)
    gnb = jnp.broadcast_to(gn_ml.astype(F32).reshape(W, 1), (W, ML_L))
    y_ml = _mlstm(qt_ml, k_ml, vt_ml.reshape(b, t // ML_L, W, ML_L), seq(o), rowq, colq, gnb)
    y_na = _na(seq(qn), seq(kn), seq(vn), _na_bias_table(rpb))

    out = _outproj_ffn2(x1, y_ml.reshape(n, W), y_na.reshape(n, NA_WIDTH),
                        w_out.astype(F32), row(norm_ffn2),
                        w2_gate.astype(F32), w2_up.astype(F32), w2_down.astype(F32), row(norm_final))
    return out.reshape(b, t, dm)


def kernel(x, norm_ffn1, w1_gate, w1_up, w1_down, norm_mix, w_in, b_gates, conv_w, conv_b, w_q_ml, w_k_ml,
           gn_ml, gq_na, gk_na, rpb, w_out, norm_ffn2, w2_gate, w2_up, w2_down, norm_final):
    depth = norm_ffn1.shape[0]
    for l in range(depth):
        x = _layer(x, norm_ffn1[l], w1_gate[l], w1_up[l], w1_down[l], norm_mix[l], w_in[l], b_gates[l],
                   conv_w[l], conv_b[l], w_q_ml[l], w_k_ml[l], gn_ml[l], gq_na[l], gk_na[l], rpb[l],
                   w_out[l], norm_ffn2[l], w2_gate[l], w2_up[l], w2_down[l], norm_final[l])
    return x
```
